```python
import jax, jax.numpy as jnp
from jax import lax
import numpy as np

D_MODEL = 1024
BATCH = 8
SEQ = 2048
DEPTH = 1
DEC_BATCH = 128
DEC_SEQ = 1
PAST_LEN = 16384
PAGE_SIZE = 128

D_MIX = D_MODEL
D_CONV = D_MIX // 2
D_LRU = D_MIX - D_CONV
N_LRU_HEADS = 8
LRU_HEAD_DIM = D_LRU // N_LRU_HEADS
CONV_A_WIDTH = 3
CONV_B_WIDTH = 4
LRU_C = 8.0
IN_COLS = 3 * D_CONV + 2 * D_LRU
FFN_DIM = ((8 * D_MODEL // 3 + 127) // 128) * 128
N_MEM = 256
N_XHEADS = 4
XHEAD_DIM = D_MODEL // N_XHEADS
RMS_EPS = 1e-6

kernel_name = "hymba_conv_rglru_macaron_step"


def rms_norm(x, g):
    xf = x.astype(jnp.float32)
    y = xf * lax.rsqrt(jnp.mean(xf * xf, axis=-1, keepdims=True) + RMS_EPS)
    return (y * g.astype(jnp.float32)).astype(x.dtype)


def swiglu(h, w_gate, w_up, w_down):
    return (jax.nn.silu(h @ w_gate) * (h @ w_up)) @ w_down


def causal_dwconv(buf, u, w):
    width = w.shape[0]
    t = u.shape[1]
    up = jnp.concatenate([buf.astype(u.dtype), u], axis=1)
    y = up[:, 0:t] * w[0]
    for k in range(1, width):
        y = y + up[:, k:k + t] * w[k]
    return y, up[:, t:]


def rg_lru(u, h0, w_a, b_a, w_x, b_x, lam, reset_first):
    n, t, _ = u.shape
    uh = u.reshape(n, t, N_LRU_HEADS, LRU_HEAD_DIM)
    gate_a = jnp.einsum('nthi,hij->nthj', uh, w_a).reshape(n, t, D_LRU) + b_a
    gate_x = jnp.einsum('nthi,hij->nthj', uh, w_x).reshape(n, t, D_LRU) + b_x
    r = jax.nn.sigmoid(gate_a.astype(jnp.float32))
    i = jax.nn.sigmoid(gate_x.astype(jnp.float32))
    log_a = -LRU_C * r * jax.nn.softplus(-lam.astype(jnp.float32))
    a = jnp.exp(log_a)
    mult = jnp.sqrt(-jnp.expm1(2.0 * log_a))
    if reset_first:
        mult = mult.at[:, 0].set(1.0)
    b = mult * i * u.astype(jnp.float32)

    def combine(left, right):
        a1, b1 = left
        a2, b2 = right
        return a1 * a2, a2 * b1 + b2

    a_cum, b_cum = lax.associative_scan(combine, (a, b), axis=1)
    h = a_cum * h0.astype(jnp.float32)[:, None, :] + b_cum
    return h.astype(u.dtype), h[:, -1].astype(h0.dtype)


def memory_kv(mem, g_mem, w_k, w_v):
    m = rms_norm(mem, g_mem)
    n, s, _ = mem.shape
    k = (m @ w_k).reshape(n, s, N_XHEADS, XHEAD_DIM)
    v = (m @ w_v).reshape(n, s, N_XHEADS, XHEAD_DIM)
    return k, v


def cross_attn(h, mem_k, mem_v, w_q, w_o):
    n, t, _ = h.shape
    q = (h @ w_q).reshape(n, t, N_XHEADS, XHEAD_DIM)
    s = jnp.einsum('nthd,nmhd->nhtm', q, mem_k.astype(q.dtype)).astype(jnp.float32) * (XHEAD_DIM ** -0.5)
    p = jax.nn.softmax(s, axis=-1).astype(h.dtype)
    o = jnp.einsum('nhtm,nmhd->nthd', p, mem_v.astype(h.dtype)).reshape(n, t, D_MODEL)
    return o @ w_o


def token_mixing(h, conv_a_buf, conv_b_buf, lru_h0, reset_first, p):
    z = h @ p['w_in']
    gb, gc, xa, xb, gg = jnp.split(z, [D_CONV, 2 * D_CONV, 3 * D_CONV, 3 * D_CONV + D_LRU], axis=-1)
    ca, new_a = causal_dwconv(conv_a_buf, gc * xa, p['conv_a_w'])
    ya = gb * ca
    cb, new_b = causal_dwconv(conv_b_buf, xb, p['conv_b_w'])
    cb = cb + p['conv_b_b']
    hb, h_last = rg_lru(cb, lru_h0, p['lru_wa'], p['lru_ba'], p['lru_wx'], p['lru_bx'], p['lru_lam'], reset_first)
    yb = jax.nn.gelu(gg) * hb
    y = jnp.concatenate([ya, yb], axis=-1) @ p['w_out']
    return y, new_a, new_b, h_last


def decoder_layer(x, mem_k, mem_v, conv_a_buf, conv_b_buf, lru_h0, reset_first, p):
    x = x + 0.5 * rms_norm(swiglu(rms_norm(x, p['g_ffn1_pre']), p['ffn1_wg'], p['ffn1_wu'], p['ffn1_wd']), p['g_ffn1_post'])
    mix, new_a, new_b, h_last = token_mixing(rms_norm(x, p['g_mix_pre']), conv_a_buf, conv_b_buf, lru_h0, reset_first, p)
    x = x + rms_norm(mix, p['g_mix_post'])
    x = x + rms_norm(cross_attn(rms_norm(x, p['g_xattn_pre']), mem_k, mem_v, p['xattn_wq'], p['xattn_wo']), p['g_xattn_post'])
    x = x + 0.5 * rms_norm(swiglu(rms_norm(x, p['g_ffn2_pre']), p['ffn2_wg'], p['ffn2_wu'], p['ffn2_wd']), p['g_ffn2_post'])
    return x, new_a, new_b, h_last


def setup_inputs(seed: int = 0) -> dict:
    key = jax.random.key(seed)
    keys = iter(jax.random.split(key, 48))
    f32 = jnp.float32

    def nrm(shape, scale=1.0):
        return jax.random.normal(next(keys), shape, f32) * scale

    def gain():
        return 1.0 + nrm((DEPTH, D_MODEL), 0.05)

    u = jax.random.uniform(next(keys), (DEPTH, D_LRU), f32, minval=0.9, maxval=0.999)
    a0 = u ** (1.0 / LRU_C)
    lru_lam = jnp.log(a0) - jnp.log1p(-a0)
    return {
        "x_prompt": nrm((BATCH, SEQ, D_MODEL)),
        "x_sample": nrm((DEC_BATCH, DEC_SEQ, D_MODEL)),
        "mem_prompt": nrm((BATCH, N_MEM, D_MODEL)),
        "cache_mem_k": nrm((DEPTH, DEC_BATCH, N_MEM, N_XHEADS, XHEAD_DIM)),
        "cache_mem_v": nrm((DEPTH, DEC_BATCH, N_MEM, N_XHEADS, XHEAD_DIM)),
        "state_conv_a": nrm((DEPTH, DEC_BATCH, CONV_A_WIDTH - 1, D_CONV)),
        "state_conv_b": nrm((DEPTH, DEC_BATCH, CONV_B_WIDTH - 1, D_LRU)),
        "state_lru": nrm((DEPTH, DEC_BATCH, D_LRU), 0.5),
        "g_ffn1_pre": gain(), "g_ffn1_post": gain(),
        "ffn1_wg": nrm((DEPTH, D_MODEL, FFN_DIM), D_MODEL ** -0.5),
        "ffn1_wu": nrm((DEPTH, D_MODEL, FFN_DIM), D_MODEL ** -0.5),
        "ffn1_wd": nrm((DEPTH, FFN_DIM, D_MODEL), FFN_DIM ** -0.5),
        "g_mix_pre": gain(), "g_mix_post": gain(),
        "w_in": nrm((DEPTH, D_MODEL, IN_COLS), D_MODEL ** -0.5),
        "conv_a_w": nrm((DEPTH, CONV_A_WIDTH, D_CONV), CONV_A_WIDTH ** -0.5),
        "conv_b_w": nrm((DEPTH, CONV_B_WIDTH, D_LRU), CONV_B_WIDTH ** -0.5),
        "conv_b_b": nrm((DEPTH, D_LRU), 0.01),
        "lru_wa": nrm((DEPTH, N_LRU_HEADS, LRU_HEAD_DIM, LRU_HEAD_DIM), LRU_HEAD_DIM ** -0.5),
        "lru_ba": nrm((DEPTH, D_LRU), 0.01),
        "lru_wx": nrm((DEPTH, N_LRU_HEADS, LRU_HEAD_DIM, LRU_HEAD_DIM), LRU_HEAD_DIM ** -0.5),
        "lru_bx": nrm((DEPTH, D_LRU), 0.01),
        "lru_lam": lru_lam,
        "w_out": nrm((DEPTH, D_MIX, D_MODEL), D_MIX ** -0.5),
        "g_xattn_pre": gain(), "g_xattn_post": gain(), "g_mem": gain(),
        "xattn_wq": nrm((DEPTH, D_MODEL, D_MODEL), D_MODEL ** -0.5),
        "xattn_wk": nrm((DEPTH, D_MODEL, D_MODEL), D_MODEL ** -0.5),
        "xattn_wv": nrm((DEPTH, D_MODEL, D_MODEL), D_MODEL ** -0.5),
        "xattn_wo": nrm((DEPTH, D_MODEL, D_MODEL), D_MODEL ** -0.5),
        "g_ffn2_pre": gain(), "g_ffn2_post": gain(),
        "ffn2_wg": nrm((DEPTH, D_MODEL, FFN_DIM), D_MODEL ** -0.5),
        "ffn2_wu": nrm((DEPTH, D_MODEL, FFN_DIM), D_MODEL ** -0.5),
        "ffn2_wd": nrm((DEPTH, FFN_DIM, D_MODEL), FFN_DIM ** -0.5),
    }


def reference(x_prompt, x_sample, mem_prompt, cache_mem_k, cache_mem_v, state_conv_a, state_conv_b, state_lru,
              g_ffn1_pre, g_ffn1_post, ffn1_wg, ffn1_wu, ffn1_wd,
              g_mix_pre, g_mix_post, w_in, conv_a_w, conv_b_w, conv_b_b,
              lru_wa, lru_ba, lru_wx, lru_bx, lru_lam, w_out,
              g_xattn_pre, g_xattn_post, g_mem, xattn_wq, xattn_wk, xattn_wv, xattn_wo,
              g_ffn2_pre, g_ffn2_post, ffn2_wg, ffn2_wu, ffn2_wd):
    yp, ys = x_prompt, x_sample
    nb = x_prompt.shape[0]
    mk_p_l, mv_p_l, ca_p_l, cb_p_l, h_p_l = [], [], [], [], []
    ca_s_l, cb_s_l, h_s_l = [], [], []
    for l in range(DEPTH):
        p = {
            'g_ffn1_pre': g_ffn1_pre[l], 'g_ffn1_post': g_ffn1_post[l],
            'ffn1_wg': ffn1_wg[l], 'ffn1_wu': ffn1_wu[l], 'ffn1_wd': ffn1_wd[l],
            'g_mix_pre': g_mix_pre[l], 'g_mix_post': g_mix_post[l], 'w_in': w_in[l],
            'conv_a_w': conv_a_w[l], 'conv_b_w': conv_b_w[l], 'conv_b_b': conv_b_b[l],
            'lru_wa': lru_wa[l], 'lru_ba': lru_ba[l], 'lru_wx': lru_wx[l], 'lru_bx': lru_bx[l],
            'lru_lam': lru_lam[l], 'w_out': w_out[l],
            'g_xattn_pre': g_xattn_pre[l], 'g_xattn_post': g_xattn_post[l],
            'xattn_wq': xattn_wq[l], 'xattn_wo': xattn_wo[l],
            'g_ffn2_pre': g_ffn2_pre[l], 'g_ffn2_post': g_ffn2_post[l],
            'ffn2_wg': ffn2_wg[l], 'ffn2_wu': ffn2_wu[l], 'ffn2_wd': ffn2_wd[l],
        }
        mk_p, mv_p = memory_kv(mem_prompt, g_mem[l], xattn_wk[l], xattn_wv[l])
        za = jnp.zeros((nb, CONV_A_WIDTH - 1, D_CONV), yp.dtype)
        zb = jnp.zeros((nb, CONV_B_WIDTH - 1, D_LRU), yp.dtype)
        zh = jnp.zeros((nb, D_LRU), yp.dtype)
        yp, ca_p, cb_p, h_p = decoder_layer(yp, mk_p, mv_p, za, zb, zh, True, p)
        ys, ca_s, cb_s, h_s = decoder_layer(ys, cache_mem_k[l], cache_mem_v[l], state_conv_a[l], state_conv_b[l],
                                            state_lru[l], False, p)
        mk_p_l.append(mk_p); mv_p_l.append(mv_p); ca_p_l.append(ca_p); cb_p_l.append(cb_p); h_p_l.append(h_p)
        ca_s_l.append(ca_s); cb_s_l.append(cb_s); h_s_l.append(h_s)
    return (yp, ys, jnp.stack(mk_p_l), jnp.stack(mv_p_l), jnp.stack(ca_p_l), jnp.stack(cb_p_l), jnp.stack(h_p_l),
            jnp.stack(ca_s_l), jnp.stack(cb_s_l), jnp.stack(h_s_l))
```

```python
import functools
import math

import jax
import jax.numpy as jnp
from jax import lax
from jax.experimental import pallas as pl
from jax.experimental.pallas import tpu as pltpu

D_MODEL = 1024
D_CONV = 512
D_LRU = 512
N_LRU_HEADS = 8
LRU_HEAD_DIM = D_LRU // N_LRU_HEADS
LRU_C = 8.0
IN_COLS = 3 * D_CONV + 2 * D_LRU
FFN_DIM = 2816
N_MEM = 256
N_XHEADS = 4
XHEAD_DIM = D_MODEL // N_XHEADS
RMS_EPS = 1e-6

F32 = jnp.float32
BF16 = jnp.bfloat16

V7X_VMEM_LIMIT_BYTES = 56 * 1024 * 1024
SUBLANES = 8

FFN_ROW_TILE = 512
FFN_COL_CHUNK = 256
MIX_ROW_TILE = 256
XATTN_ROW_TILE = 512
KV_ROW_TILE = 512
SAMPLE_XATTN_BLOCK = 4


def _rms(x, g):
    y = x * lax.rsqrt(jnp.mean(x * x, axis=-1, keepdims=True) + RMS_EPS)
    return y * g


def _dot(a, b):
    return jnp.dot(a, b, preferred_element_type=F32)


def _sigmoid(x):
    return 1.0 / (1.0 + jnp.exp(-x))


def _gelu_tanh(x):
    c = math.sqrt(2.0 / math.pi)
    return x * (0.5 * (1.0 + jnp.tanh(c * (x + 0.044715 * (x * x * x)))))


def _expm1(y):
    u = jnp.exp(y)
    stable = (u - 1.0) * y / jnp.log(u)
    return jnp.where(u == 1.0, y, jnp.where(y < -1.0, u - 1.0, stable))


def _log1p(w):
    u = 1.0 + w
    return jnp.where(u == 1.0, w, jnp.log(u) * w / (u - 1.0))


def _softplus(x):
    return jnp.maximum(x, 0.0) + _log1p(jnp.exp(-jnp.abs(x)))


def _const_spec(shape):
    zeros = (0,) * len(shape)
    return pl.BlockSpec(shape, lambda *_: zeros, pipeline_mode=pl.Buffered(1))


def _params(*sem):
    return pltpu.CompilerParams(dimension_semantics=sem, vmem_limit_bytes=V7X_VMEM_LIMIT_BYTES)


def _ffn_kernel(x_ref, gpre_ref, gpost_ref, wg_ref, wu_ref, wd_ref, o_ref, act_ref):
    x = x_ref[...]
    h = _rms(x, gpre_ref[...]).astype(BF16)
    for c in range(FFN_DIM // FFN_COL_CHUNK):
        sl = slice(c * FFN_COL_CHUNK, (c + 1) * FFN_COL_CHUNK)
        g = _dot(h, wg_ref[:, sl])
        u = _dot(h, wu_ref[:, sl])
        act_ref[:, sl] = ((g * _sigmoid(g)) * u).astype(BF16)
    y = _dot(act_ref[...], wd_ref[...])
    o_ref[...] = x + 0.5 * _rms(y, gpost_ref[...])


def _ffn(x, g_pre, g_post, wg, wu, wd, row_tile):
    rows = x.shape[0]
    row_spec = pl.BlockSpec((row_tile, D_MODEL), lambda i: (i, 0))
    return pl.pallas_call(
        _ffn_kernel,
        grid=(rows // row_tile,),
        in_specs=[row_spec, _const_spec((1, D_MODEL)), _const_spec((1, D_MODEL)),
                  _const_spec((D_MODEL, FFN_DIM)), _const_spec((D_MODEL, FFN_DIM)),
                  _const_spec((FFN_DIM, D_MODEL))],
        out_specs=row_spec,
        out_shape=jax.ShapeDtypeStruct((rows, D_MODEL), F32),
        scratch_shapes=[pltpu.VMEM((row_tile, FFN_DIM), BF16)],
        compiler_params=_params("parallel"),
        name="ffn",
    )(x, g_pre, g_post, wg, wu, wd)


def _lru_coeffs(cb, wax_ref, ba_ref, bx_ref, lam_ref):
    gates = _dot(cb.astype(BF16), wax_ref[...])
    r = _sigmoid(gates[:, :D_LRU] + ba_ref[...])
    i = _sigmoid(gates[:, D_LRU:] + bx_ref[...])
    log_a = (-LRU_C * r) * _softplus(-lam_ref[...])
    a = jnp.exp(log_a)
    mult = jnp.sqrt(-_expm1(2.0 * log_a))
    return a, mult, i


def _mix_prompt_kernel(x_ref, gpre_ref, gpost_ref, win_ref, caw_ref, cbw_ref, cbb_ref, wax_ref, ba_ref,
                       bx_ref, lam_ref, wout_ref, o_ref, ta_ref, tb_ref, hl_ref, exta_ref, extb_ref, hc_ref):
    t = pl.program_id(1)
    tt = x_ref.shape[0]

    @pl.when(t == 0)
    def _():
        exta_ref[0:SUBLANES, :] = jnp.zeros((SUBLANES, D_CONV), F32)
        extb_ref[0:SUBLANES, :] = jnp.zeros((SUBLANES, D_LRU), F32)
        hc_ref[...] = jnp.zeros_like(hc_ref)

    x = x_ref[...]
    h = _rms(x, gpre_ref[...]).astype(BF16)
    z = _dot(h, win_ref[...])
    gb = z[:, 0:D_CONV]
    gc = z[:, D_CONV:2 * D_CONV]
    xa = z[:, 2 * D_CONV:3 * D_CONV]
    xb = z[:, 3 * D_CONV:3 * D_CONV + D_LRU]
    gg = z[:, 3 * D_CONV + D_LRU:]

    v = gc * xa
    exta_ref[SUBLANES:SUBLANES + tt, :] = v
    ca = (exta_ref[SUBLANES - 2:SUBLANES - 2 + tt, :] * caw_ref[0:1, :]
          + exta_ref[SUBLANES - 1:SUBLANES - 1 + tt, :] * caw_ref[1:2, :]
          + v * caw_ref[2:3, :])
    ya = gb * ca
    tail_a = v[tt - SUBLANES:tt, :]
    exta_ref[0:SUBLANES, :] = tail_a
    ta_ref[0] = tail_a

    extb_ref[SUBLANES:SUBLANES + tt, :] = xb
    cb = (extb_ref[SUBLANES - 3:SUBLANES - 3 + tt, :] * cbw_ref[0:1, :]
          + extb_ref[SUBLANES - 2:SUBLANES - 2 + tt, :] * cbw_ref[1:2, :]
          + extb_ref[SUBLANES - 1:SUBLANES - 1 + tt, :] * cbw_ref[2:3, :]
          + xb * cbw_ref[3:4, :])
    cb = cb + cbb_ref[...]
    tail_b = xb[tt - SUBLANES:tt, :]
    extb_ref[0:SUBLANES, :] = tail_b
    tb_ref[0] = tail_b

    a, mult, i = _lru_coeffs(cb, wax_ref, ba_ref, bx_ref, lam_ref)
    row = lax.broadcasted_iota(jnp.int32, (tt, D_LRU), 0)
    first_row = jnp.where(t == 0, 0, -1)
    mult = jnp.where(row == first_row, 1.0, mult)
    b = (mult * i) * cb

    s = 1
    while s < tt:
        keep = row >= s
        a_prev = jnp.where(keep, pltpu.roll(a, s, 0), 1.0)
        b_prev = jnp.where(keep, pltpu.roll(b, s, 0), 0.0)
        b = a * b_prev + b
        a = a * a_prev
        s *= 2
    hseq = a * hc_ref[0:1, :] + b
    h_last = hseq[tt - SUBLANES:tt, :]
    hc_ref[...] = jnp.broadcast_to(h_last[SUBLANES - 1:SUBLANES, :], hc_ref.shape)
    hl_ref[0] = h_last

    yb = _gelu_tanh(gg) * hseq
    y = _dot(jnp.concatenate([ya, yb], axis=-1).astype(BF16), wout_ref[...])
    o_ref[...] = x + _rms(y, gpost_ref[...])


def _mix_weight_specs():
    return [_const_spec((1, D_MODEL)), _const_spec((1, D_MODEL)), _const_spec((D_MODEL, IN_COLS)),
            _const_spec((3, D_CONV)), _const_spec((4, D_LRU)), _const_spec((1, D_LRU)),
            _const_spec((D_LRU, 2 * D_LRU)), _const_spec((1, D_LRU)), _const_spec((1, D_LRU)),
            _const_spec((1, D_LRU)), _const_spec((D_MODEL, D_MODEL))]


def _mix_prompt(x, weights, batch, seq):
    n_t = seq // MIX_ROW_TILE
    row_spec = pl.BlockSpec((MIX_ROW_TILE, D_MODEL), lambda b, t: (b * n_t + t, 0))
    tail_spec = pl.BlockSpec((1, SUBLANES, D_CONV), lambda b, t: (b, 0, 0))
    tail_shape = jax.ShapeDtypeStruct((batch, SUBLANES, D_CONV), F32)
    return pl.pallas_call(
        _mix_prompt_kernel,
        grid=(batch, n_t),
        in_specs=[row_spec] + _mix_weight_specs(),
        out_specs=[row_spec, tail_spec, tail_spec, tail_spec],
        out_shape=[jax.ShapeDtypeStruct(x.shape, F32), tail_shape, tail_shape, tail_shape],
        scratch_shapes=[pltpu.VMEM((SUBLANES + MIX_ROW_TILE, D_CONV), F32),
                        pltpu.VMEM((SUBLANES + MIX_ROW_TILE, D_LRU), F32),
                        pltpu.VMEM((SUBLANES, D_LRU), F32)],
        compiler_params=_params("parallel", "arbitrary"),
        name="mix_prompt",
    )(x, *weights)


def _mix_sample_kernel(x_ref, sa_ref, sb_ref, h0_ref, gpre_ref, gpost_ref, win_ref, caw_ref, cbw_ref, cbb_ref,
                       wax_ref, ba_ref, bx_ref, lam_ref, wout_ref, o_ref, na_ref, nb_ref, hl_ref):
    x = x_ref[...]
    h = _rms(x, gpre_ref[...]).astype(BF16)
    z = _dot(h, win_ref[...])
    gb = z[:, 0:D_CONV]
    gc = z[:, D_CONV:2 * D_CONV]
    xa = z[:, 2 * D_CONV:3 * D_CONV]
    xb = z[:, 3 * D_CONV:3 * D_CONV + D_LRU]
    gg = z[:, 3 * D_CONV + D_LRU:]

    v = gc * xa
    a1 = sa_ref[:, D_CONV:2 * D_CONV]
    ca = sa_ref[:, 0:D_CONV] * caw_ref[0:1, :] + a1 * caw_ref[1:2, :] + v * caw_ref[2:3, :]
    ya = gb * ca
    na_ref[:, 0:D_CONV] = a1
    na_ref[:, D_CONV:2 * D_CONV] = v

    b1 = sb_ref[:, D_LRU:2 * D_LRU]
    b2 = sb_ref[:, 2 * D_LRU:3 * D_LRU]
    cb = (sb_ref[:, 0:D_LRU] * cbw_ref[0:1, :] + b1 * cbw_ref[1:2, :] + b2 * cbw_ref[2:3, :]
          + xb * cbw_ref[3:4, :])
    cb = cb + cbb_ref[...]
    nb_ref[:, 0:D_LRU] = b1
    nb_ref[:, D_LRU:2 * D_LRU] = b2
    nb_ref[:, 2 * D_LRU:3 * D_LRU] = xb

    a, mult, i = _lru_coeffs(cb, wax_ref, ba_ref, bx_ref, lam_ref)
    hnew = a * h0_ref[...] + (mult * i) * cb
    hl_ref[...] = hnew

    yb = _gelu_tanh(gg) * hnew
    y = _dot(jnp.concatenate([ya, yb], axis=-1).astype(BF16), wout_ref[...])
    o_ref[...] = x + _rms(y, gpost_ref[...])


def _mix_sample(x, sa, sb, h0, weights):
    n = x.shape[0]
    full = lambda a: _const_spec(a.shape)
    return pl.pallas_call(
        _mix_sample_kernel,
        grid=(1,),
        in_specs=[full(x), full(sa), full(sb), full(h0)] + _mix_weight_specs(),
        out_specs=[full(x), full(sa), full(sb), full(h0)],
        out_shape=[jax.ShapeDtypeStruct(x.shape, F32), jax.ShapeDtypeStruct(sa.shape, F32),
                   jax.ShapeDtypeStruct(sb.shape, F32), jax.ShapeDtypeStruct(h0.shape, F32)],
        compiler_params=_params("arbitrary"),
        name="mix_sample",
    )(x, sa, sb, h0, *weights)


def _kv_kernel(m_ref, g_ref, wk_ref, wv_ref, k_ref, v_ref):
    m = _rms(m_ref[...], g_ref[...]).astype(BF16)
    k_ref[...] = _dot(m, wk_ref[...])
    v_ref[...] = _dot(m, wv_ref[...])


def _memory_kv(mem, g_mem, wk, wv):
    rows = mem.shape[0]
    row_spec = pl.BlockSpec((KV_ROW_TILE, D_MODEL), lambda i: (i, 0))
    out = jax.ShapeDtypeStruct((rows, D_MODEL), F32)
    return pl.pallas_call(
        _kv_kernel,
        grid=(rows // KV_ROW_TILE,),
        in_specs=[row_spec, _const_spec((1, D_MODEL)), _const_spec((D_MODEL, D_MODEL)),
                  _const_spec((D_MODEL, D_MODEL))],
        out_specs=[row_spec, row_spec],
        out_shape=[out, out],
        compiler_params=_params("parallel"),
        name="memory_kv",
    )(mem, g_mem, wk, wv)


def _softmax_rows(s):
    e = jnp.exp(s - jnp.max(s, axis=-1, keepdims=True))
    return e * (1.0 / jnp.sum(e, axis=-1, keepdims=True))


def _xattn_prompt_kernel(x_ref, k_ref, v_ref, gpre_ref, gpost_ref, wq_ref, wo_ref, o_ref):
    x = x_ref[...]
    h = _rms(x, gpre_ref[...]).astype(BF16)
    q = _dot(h, wq_ref[...]).astype(BF16)
    heads = []
    for hd in range(N_XHEADS):
        sl = slice(hd * XHEAD_DIM, (hd + 1) * XHEAD_DIM)
        kh = k_ref[0, :, sl].astype(BF16)
        vh = v_ref[0, :, sl].astype(BF16)
        s = lax.dot_general(q[:, sl], kh, (((1,), (1,)), ((), ())), preferred_element_type=F32)
        p = _softmax_rows(s * (XHEAD_DIM ** -0.5)).astype(BF16)
        heads.append(_dot(p, vh).astype(BF16))
    y = _dot(jnp.concatenate(heads, axis=-1), wo_ref[...])
    o_ref[...] = x + _rms(y, gpost_ref[...])


def _xattn_prompt(x, mem_k, mem_v, g_pre, g_post, wq, wo, batch, seq):
    n_t = seq // XATTN_ROW_TILE
    row_spec = pl.BlockSpec((XATTN_ROW_TILE, D_MODEL), lambda b, t: (b * n_t + t, 0))
    kv_spec = pl.BlockSpec((1, N_MEM, D_MODEL), lambda b, t: (b, 0, 0))
    return pl.pallas_call(
        _xattn_prompt_kernel,
        grid=(batch, n_t),
        in_specs=[row_spec, kv_spec, kv_spec, _const_spec((1, D_MODEL)), _const_spec((1, D_MODEL)),
                  _const_spec((D_MODEL, D_MODEL)), _const_spec((D_MODEL, D_MODEL))],
        out_specs=row_spec,
        out_shape=jax.ShapeDtypeStruct(x.shape, F32),
        compiler_params=_params("parallel", "arbitrary"),
        name="xattn_prompt",
    )(x, mem_k, mem_v, g_pre, g_post, wq, wo)


def _q_sample_kernel(x_ref, gpre_ref, wq_ref, q_ref):
    q_ref[...] = _dot(_rms(x_ref[...], gpre_ref[...]).astype(BF16), wq_ref[...])


def _attend_sample_kernel(q_ref, k_ref, v_ref, o_ref):
    for hd in range(N_XHEADS):
        sl = slice(hd * XHEAD_DIM, (hd + 1) * XHEAD_DIM)
        qh = q_ref[:, :, sl].astype(BF16)
        kh = k_ref[:, :, sl].astype(BF16)
        vh = v_ref[:, :, sl].astype(BF16)
        s = jnp.einsum('bqd,bkd->bqk', qh, kh, preferred_element_type=F32)
        p = _softmax_rows(s * (XHEAD_DIM ** -0.5)).astype(BF16)
        o_ref[:, :, sl] = jnp.einsum('bqk,bkd->bqd', p, vh, preferred_element_type=F32)


def _out_sample_kernel(x_ref, o_ref, gpost_ref, wo_ref, y_ref):
    y = _dot(o_ref[...].astype(BF16), wo_ref[...])
    y_ref[...] = x_ref[...] + _rms(y, gpost_ref[...])


def _xattn_sample(x, cache_k, cache_v, g_pre, g_post, wq, wo):
    n = x.shape[0]
    full = lambda shape: _const_spec(shape)
    q = pl.pallas_call(
        _q_sample_kernel,
        grid=(1,),
        in_specs=[full(x.shape), full((1, D_MODEL)), full((D_MODEL, D_MODEL))],
        out_specs=full(x.shape),
        out_shape=jax.ShapeDtypeStruct(x.shape, F32),
        compiler_params=_params("arbitrary"),
        name="q_sample",
    )(x, g_pre, wq)
    nb = SAMPLE_XATTN_BLOCK
    q_spec = pl.BlockSpec((nb, 1, D_MODEL), lambda i: (i, 0, 0))
    kv_spec = pl.BlockSpec((nb, N_MEM, D_MODEL), lambda i: (i, 0, 0))
    o = pl.pallas_call(
        _attend_sample_kernel,
        grid=(n // nb,),
        in_specs=[q_spec, kv_spec, kv_spec],
        out_specs=q_spec,
        out_shape=jax.ShapeDtypeStruct((n, 1, D_MODEL), F32),
        compiler_params=_params("parallel"),
        name="attend_sample",
    )(q.reshape(n, 1, D_MODEL), cache_k, cache_v)
    return pl.pallas_call(
        _out_sample_kernel,
        grid=(1,),
        in_specs=[full(x.shape), full(x.shape), full((1, D_MODEL)), full((D_MODEL, D_MODEL))],
        out_specs=full(x.shape),
        out_shape=jax.ShapeDtypeStruct(x.shape, F32),
        compiler_params=_params("arbitrary"),
        name="out_sample",
    )(x, o.reshape(n, D_MODEL), g_post, wo)


def _block_diag(w):
    heads, d, _ = w.shape
    eye = jnp.eye(heads, dtype=w.dtype)
    return (eye[:, None, :, None] * w[:, :, None, :]).reshape(heads * d, heads * d)


def kernel(x_prompt, x_sample, mem_prompt, cache_mem_k, cache_mem_v, state_conv_a, state_conv_b, state_lru, g_ffn1_pre, g_ffn1_post, ffn1_wg, ffn1_wu, ffn1_wd, g_mix_pre, g_mix_post, w_in, conv_a_w, conv_b_w, conv_b_b, lru_wa, lru_ba, lru_wx, lru_bx, lru_lam, w_out, g_xattn_pre, g_xattn_post, g_mem, xattn_wq, xattn_wk, xattn_wv, xattn_wo, g_ffn2_pre, g_ffn2_post, ffn2_wg, ffn2_wu, ffn2_wd):
    batch, seq, _ = x_prompt.shape
    n_s = x_sample.shape[0]
    depth = g_ffn1_pre.shape[0]
    assert depth == 1 and x_sample.shape[1] == 1
    l = 0
    bf = lambda w: w.astype(BF16)
    row = lambda p: p[l].reshape(1, -1)

    yp = x_prompt.reshape(batch * seq, D_MODEL)
    ys = x_sample.reshape(n_s, D_MODEL)

    ffn1 = (row(g_ffn1_pre), row(g_ffn1_post), bf(ffn1_wg[l]), bf(ffn1_wu[l]), bf(ffn1_wd[l]))
    ffn2 = (row(g_ffn2_pre), row(g_ffn2_post), bf(ffn2_wg[l]), bf(ffn2_wu[l]), bf(ffn2_wd[l]))
    wax = jnp.concatenate([_block_diag(lru_wa[l]), _block_diag(lru_wx[l])], axis=1)
    mix_w = (row(g_mix_pre), row(g_mix_post), bf(w_in[l]), conv_a_w[l], conv_b_w[l], row(conv_b_b),
             bf(wax), row(lru_ba), row(lru_bx), row(lru_lam), bf(w_out[l]))

    mk, mv = _memory_kv(mem_prompt.reshape(batch * N_MEM, D_MODEL), row(g_mem), bf(xattn_wk[l]), bf(xattn_wv[l]))

    yp = _ffn(yp, *ffn1, row_tile=FFN_ROW_TILE)
    yp, tail_a, tail_b, tail_h = _mix_prompt(yp, mix_w, batch, seq)
    yp = _xattn_prompt(yp, mk.reshape(batch, N_MEM, D_MODEL), mv.reshape(batch, N_MEM, D_MODEL),
                       row(g_xattn_pre), row(g_xattn_post), bf(xattn_wq[l]), bf(xattn_wo[l]), batch, seq)
    yp = _ffn(yp, *ffn2, row_tile=FFN_ROW_TILE)

    ys = _ffn(ys, *ffn1, row_tile=n_s)
    ys, new_a, new_b, new_h = _mix_sample(ys, state_conv_a[l].reshape(n_s, 2 * D_CONV),
                                          state_conv_b[l].reshape(n_s, 3 * D_LRU), state_lru[l], mix_w)
    ys = _xattn_sample(ys, cache_mem_k[l].reshape(n_s, N_MEM, D_MODEL), cache_mem_v[l].reshape(n_s, N_MEM, D_MODEL),
                       row(g_xattn_pre), row(g_xattn_post), bf(xattn_wq[l]), bf(xattn_wo[l]))
    ys = _ffn(ys, *ffn2, row_tile=n_s)

    kv_shape = (1, batch, N_MEM, N_XHEADS, XHEAD_DIM)
    return (yp.reshape(batch, seq, D_MODEL), ys.reshape(n_s, 1, D_MODEL),
            mk.reshape(kv_shape), mv.reshape(kv_shape),
            tail_a[None, :, SUBLANES - 2:, :], tail_b[None, :, SUBLANES - 3:, :], tail_h[None, :, SUBLANES - 1, :],
            new_a.reshape(1, n_s, 2, D_CONV), new_b.reshape(1, n_s, 3, D_LRU), new_h[None])
```

```python
import functools
import math

import jax
import jax.numpy as jnp
from jax import lax
from jax.experimental import pallas as pl
from jax.experimental.pallas import tpu as pltpu

D_MODEL = 1024
D_CONV = 512
D_LRU = 512
N_LRU_HEADS = 8
LRU_HEAD_DIM = D_LRU // N_LRU_HEADS
LRU_C = 8.0
IN_COLS = 3 * D_CONV + 2 * D_LRU
FFN_DIM = 2816
N_MEM = 256
N_XHEADS = 4
XHEAD_DIM = D_MODEL // N_XHEADS
RMS_EPS = 1e-6

F32 = jnp.float32
BF16 = jnp.bfloat16

V7X_VMEM_LIMIT_BYTES = 56 * 1024 * 1024
SUBLANES = 8
LANES = 128
HEAD_ROWS = N_XHEADS * XHEAD_DIM // LANES

FFN_ROW_TILE = 512
FFN_COL_CHUNK = 256
MIX_ROW_TILE = 256
XATTN_ROW_TILE = 512
KV_ROW_TILE = 512
SAMPLE_XATTN_BLOCK = 4


def _rms(x, g):
    y = x * lax.rsqrt(jnp.mean(x * x, axis=-1, keepdims=True) + RMS_EPS)
    return y * g


def _dot(a, b):
    return jnp.dot(a, b, preferred_element_type=F32)


def _sigmoid(x):
    return 1.0 / (1.0 + jnp.exp(-x))


def _gelu_tanh(x):
    c = math.sqrt(2.0 / math.pi)
    return x * (0.5 * (1.0 + jnp.tanh(c * (x + 0.044715 * (x * x * x)))))


def _expm1(y):
    u = jnp.exp(y)
    stable = (u - 1.0) * y / jnp.log(u)
    return jnp.where(u == 1.0, y, jnp.where(y < -1.0, u - 1.0, stable))


def _log1p(w):
    u = 1.0 + w
    return jnp.where(u == 1.0, w, jnp.log(u) * w / (u - 1.0))


def _softplus(x):
    return jnp.maximum(x, 0.0) + _log1p(jnp.exp(-jnp.abs(x)))


def _const_spec(shape):
    zeros = (0,) * len(shape)
    return pl.BlockSpec(shape, lambda *_: zeros, pipeline_mode=pl.Buffered(1))


def _params(*sem):
    return pltpu.CompilerParams(dimension_semantics=sem, vmem_limit_bytes=V7X_VMEM_LIMIT_BYTES)


def _ffn_kernel(x_ref, gpre_ref, gpost_ref, wg_ref, wu_ref, wd_ref, o_ref, act_ref):
    x = x_ref[...]
    h = _rms(x, gpre_ref[...]).astype(BF16)
    for c in range(FFN_DIM // FFN_COL_CHUNK):
        sl = slice(c * FFN_COL_CHUNK, (c + 1) * FFN_COL_CHUNK)
        g = _dot(h, wg_ref[:, sl])
        u = _dot(h, wu_ref[:, sl])
        act_ref[:, sl] = ((g * _sigmoid(g)) * u).astype(BF16)
    y = _dot(act_ref[...], wd_ref[...])
    o_ref[...] = x + 0.5 * _rms(y, gpost_ref[...])


def _ffn(x, g_pre, g_post, wg, wu, wd, row_tile):
    rows = x.shape[0]
    row_spec = pl.BlockSpec((row_tile, D_MODEL), lambda i: (i, 0))
    return pl.pallas_call(
        _ffn_kernel,
        grid=(rows // row_tile,),
        in_specs=[row_spec, _const_spec((1, D_MODEL)), _const_spec((1, D_MODEL)),
                  _const_spec((D_MODEL, FFN_DIM)), _const_spec((D_MODEL, FFN_DIM)),
                  _const_spec((FFN_DIM, D_MODEL))],
        out_specs=row_spec,
        out_shape=jax.ShapeDtypeStruct((rows, D_MODEL), F32),
        scratch_shapes=[pltpu.VMEM((row_tile, FFN_DIM), BF16)],
        compiler_params=_params("parallel"),
        name="ffn",
    )(x, g_pre, g_post, wg, wu, wd)


def _lru_coeffs(cb, wax_ref, ba_ref, bx_ref, lam_ref):
    gates = _dot(cb.astype(BF16), wax_ref[...])
    r = _sigmoid(gates[:, :D_LRU] + ba_ref[...])
    i = _sigmoid(gates[:, D_LRU:] + bx_ref[...])
    log_a = (-LRU_C * r) * _softplus(-lam_ref[...])
    a = jnp.exp(log_a)
    mult = jnp.sqrt(-_expm1(2.0 * log_a))
    return a, mult, i


def _mix_prompt_kernel(x_ref, gpre_ref, gpost_ref, win_ref, caw_ref, cbw_ref, cbb_ref, wax_ref, ba_ref,
                       bx_ref, lam_ref, wout_ref, o_ref, ta_ref, tb_ref, hl_ref, exta_ref, extb_ref, hc_ref):
    t = pl.program_id(1)
    tt = x_ref.shape[0]

    @pl.when(t == 0)
    def _():
        exta_ref[0:SUBLANES, :] = jnp.zeros((SUBLANES, D_CONV), F32)
        extb_ref[0:SUBLANES, :] = jnp.zeros((SUBLANES, D_LRU), F32)
        hc_ref[...] = jnp.zeros_like(hc_ref)

    x = x_ref[...]
    h = _rms(x, gpre_ref[...]).astype(BF16)
    z = _dot(h, win_ref[...])
    gb = z[:, 0:D_CONV]
    gc = z[:, D_CONV:2 * D_CONV]
    xa = z[:, 2 * D_CONV:3 * D_CONV]
    xb = z[:, 3 * D_CONV:3 * D_CONV + D_LRU]
    gg = z[:, 3 * D_CONV + D_LRU:]

    v = gc * xa
    exta_ref[SUBLANES:SUBLANES + tt, :] = v
    ca = (exta_ref[SUBLANES - 2:SUBLANES - 2 + tt, :] * caw_ref[0:1, :]
          + exta_ref[SUBLANES - 1:SUBLANES - 1 + tt, :] * caw_ref[1:2, :]
          + v * caw_ref[2:3, :])
    ya = gb * ca
    tail_a = v[tt - SUBLANES:tt, :]
    exta_ref[0:SUBLANES, :] = tail_a
    ta_ref[0] = tail_a

    extb_ref[SUBLANES:SUBLANES + tt, :] = xb
    cb = (extb_ref[SUBLANES - 3:SUBLANES - 3 + tt, :] * cbw_ref[0:1, :]
          + extb_ref[SUBLANES - 2:SUBLANES - 2 + tt, :] * cbw_ref[1:2, :]
          + extb_ref[SUBLANES - 1:SUBLANES - 1 + tt, :] * cbw_ref[2:3, :]
          + xb * cbw_ref[3:4, :])
    cb = cb + cbb_ref[...]
    tail_b = xb[tt - SUBLANES:tt, :]
    extb_ref[0:SUBLANES, :] = tail_b
    tb_ref[0] = tail_b

    a, mult, i = _lru_coeffs(cb, wax_ref, ba_ref, bx_ref, lam_ref)
    row = lax.broadcasted_iota(jnp.int32, (tt, D_LRU), 0)
    first_row = jnp.where(t == 0, 0, -1)
    mult = jnp.where(row == first_row, 1.0, mult)
    b = (mult * i) * cb

    s = 1
    while s < tt:
        keep = row >= s
        a_prev = jnp.where(keep, pltpu.roll(a, s, 0), 1.0)
        b_prev = jnp.where(keep, pltpu.roll(b, s, 0), 0.0)
        b = a * b_prev + b
        a = a * a_prev
        s *= 2
    hseq = a * hc_ref[0:1, :] + b
    h_last = hseq[tt - SUBLANES:tt, :]
    hc_ref[...] = jnp.broadcast_to(h_last[SUBLANES - 1:SUBLANES, :], hc_ref.shape)
    hl_ref[0] = h_last

    yb = _gelu_tanh(gg) * hseq
    y = _dot(jnp.concatenate([ya, yb], axis=-1).astype(BF16), wout_ref[...])
    o_ref[...] = x + _rms(y, gpost_ref[...])


def _mix_weight_specs():
    return [_const_spec((1, D_MODEL)), _const_spec((1, D_MODEL)), _const_spec((D_MODEL, IN_COLS)),
            _const_spec((3, D_CONV)), _const_spec((4, D_LRU)), _const_spec((1, D_LRU)),
            _const_spec((D_LRU, 2 * D_LRU)), _const_spec((1, D_LRU)), _const_spec((1, D_LRU)),
            _const_spec((1, D_LRU)), _const_spec((D_MODEL, D_MODEL))]


def _mix_prompt(x, weights, batch, seq):
    n_t = seq // MIX_ROW_TILE
    row_spec = pl.BlockSpec((MIX_ROW_TILE, D_MODEL), lambda b, t: (b * n_t + t, 0))
    tail_spec = pl.BlockSpec((1, SUBLANES, D_CONV), lambda b, t: (b, 0, 0))
    tail_shape = jax.ShapeDtypeStruct((batch, SUBLANES, D_CONV), F32)
    return pl.pallas_call(
        _mix_prompt_kernel,
        grid=(batch, n_t),
        in_specs=[row_spec] + _mix_weight_specs(),
        out_specs=[row_spec, tail_spec, tail_spec, tail_spec],
        out_shape=[jax.ShapeDtypeStruct(x.shape, F32), tail_shape, tail_shape, tail_shape],
        scratch_shapes=[pltpu.VMEM((SUBLANES + MIX_ROW_TILE, D_CONV), F32),
                        pltpu.VMEM((SUBLANES + MIX_ROW_TILE, D_LRU), F32),
                        pltpu.VMEM((SUBLANES, D_LRU), F32)],
        compiler_params=_params("parallel", "arbitrary"),
        name="mix_prompt",
    )(x, *weights)


def _mix_sample_kernel(x_ref, sa_ref, sb_ref, h0_ref, gpre_ref, gpost_ref, win_ref, caw_ref, cbw_ref, cbb_ref,
                       wax_ref, ba_ref, bx_ref, lam_ref, wout_ref, o_ref, na_ref, nb_ref, hl_ref):
    x = x_ref[...]
    h = _rms(x, gpre_ref[...]).astype(BF16)
    z = _dot(h, win_ref[...])
    gb = z[:, 0:D_CONV]
    gc = z[:, D_CONV:2 * D_CONV]
    xa = z[:, 2 * D_CONV:3 * D_CONV]
    xb = z[:, 3 * D_CONV:3 * D_CONV + D_LRU]
    gg = z[:, 3 * D_CONV + D_LRU:]

    v = gc * xa
    a1 = sa_ref[:, D_CONV:2 * D_CONV]
    ca = sa_ref[:, 0:D_CONV] * caw_ref[0:1, :] + a1 * caw_ref[1:2, :] + v * caw_ref[2:3, :]
    ya = gb * ca
    na_ref[:, 0:D_CONV] = a1
    na_ref[:, D_CONV:2 * D_CONV] = v

    b1 = sb_ref[:, D_LRU:2 * D_LRU]
    b2 = sb_ref[:, 2 * D_LRU:3 * D_LRU]
    cb = (sb_ref[:, 0:D_LRU] * cbw_ref[0:1, :] + b1 * cbw_ref[1:2, :] + b2 * cbw_ref[2:3, :]
          + xb * cbw_ref[3:4, :])
    cb = cb + cbb_ref[...]
    nb_ref[:, 0:D_LRU] = b1
    nb_ref[:, D_LRU:2 * D_LRU] = b2
    nb_ref[:, 2 * D_LRU:3 * D_LRU] = xb

    a, mult, i = _lru_coeffs(cb, wax_ref, ba_ref, bx_ref, lam_ref)
    hnew = a * h0_ref[...] + (mult * i) * cb
    hl_ref[...] = hnew

    yb = _gelu_tanh(gg) * hnew
    y = _dot(jnp.concatenate([ya, yb], axis=-1).astype(BF16), wout_ref[...])
    o_ref[...] = x + _rms(y, gpost_ref[...])


def _mix_sample(x, sa, sb, h0, weights):
    n = x.shape[0]
    full = lambda a: _const_spec(a.shape)
    return pl.pallas_call(
        _mix_sample_kernel,
        grid=(1,),
        in_specs=[full(x), full(sa), full(sb), full(h0)] + _mix_weight_specs(),
        out_specs=[full(x), full(sa), full(sb), full(h0)],
        out_shape=[jax.ShapeDtypeStruct(x.shape, F32), jax.ShapeDtypeStruct(sa.shape, F32),
                   jax.ShapeDtypeStruct(sb.shape, F32), jax.ShapeDtypeStruct(h0.shape, F32)],
        compiler_params=_params("arbitrary"),
        name="mix_sample",
    )(x, sa, sb, h0, *weights)


def _kv_kernel(m_ref, g_ref, wk_ref, wv_ref, k_ref, v_ref):
    m = _rms(m_ref[...], g_ref[...]).astype(BF16)
    k_ref[...] = _dot(m, wk_ref[...])
    v_ref[...] = _dot(m, wv_ref[...])


def _memory_kv(mem, g_mem, wk, wv):
    rows = mem.shape[0]
    row_spec = pl.BlockSpec((KV_ROW_TILE, D_MODEL), lambda i: (i, 0))
    out = jax.ShapeDtypeStruct((rows, D_MODEL), F32)
    return pl.pallas_call(
        _kv_kernel,
        grid=(rows // KV_ROW_TILE,),
        in_specs=[row_spec, _const_spec((1, D_MODEL)), _const_spec((D_MODEL, D_MODEL)),
                  _const_spec((D_MODEL, D_MODEL))],
        out_specs=[row_spec, row_spec],
        out_shape=[out, out],
        compiler_params=_params("parallel"),
        name="memory_kv",
    )(mem, g_mem, wk, wv)


def _softmax_rows(s):
    e = jnp.exp(s - jnp.max(s, axis=-1, keepdims=True))
    return e * (1.0 / jnp.sum(e, axis=-1, keepdims=True))


def _xattn_prompt_kernel(x_ref, k_ref, v_ref, gpre_ref, gpost_ref, wq_ref, wo_ref, o_ref):
    x = x_ref[...]
    h = _rms(x, gpre_ref[...]).astype(BF16)
    q = _dot(h, wq_ref[...]).astype(BF16)
    heads = []
    for hd in range(N_XHEADS):
        sl = slice(hd * XHEAD_DIM, (hd + 1) * XHEAD_DIM)
        kh = k_ref[0, :, sl].astype(BF16)
        vh = v_ref[0, :, sl].astype(BF16)
        s = lax.dot_general(q[:, sl], kh, (((1,), (1,)), ((), ())), preferred_element_type=F32)
        p = _softmax_rows(s * (XHEAD_DIM ** -0.5)).astype(BF16)
        heads.append(_dot(p, vh).astype(BF16))
    y = _dot(jnp.concatenate(heads, axis=-1), wo_ref[...])
    o_ref[...] = x + _rms(y, gpost_ref[...])


def _xattn_prompt(x, mem_k, mem_v, g_pre, g_post, wq, wo, batch, seq):
    n_t = seq // XATTN_ROW_TILE
    row_spec = pl.BlockSpec((XATTN_ROW_TILE, D_MODEL), lambda b, t: (b * n_t + t, 0))
    kv_spec = pl.BlockSpec((1, N_MEM, D_MODEL), lambda b, t: (b, 0, 0))
    return pl.pallas_call(
        _xattn_prompt_kernel,
        grid=(batch, n_t),
        in_specs=[row_spec, kv_spec, kv_spec, _const_spec((1, D_MODEL)), _const_spec((1, D_MODEL)),
                  _const_spec((D_MODEL, D_MODEL)), _const_spec((D_MODEL, D_MODEL))],
        out_specs=row_spec,
        out_shape=jax.ShapeDtypeStruct(x.shape, F32),
        compiler_params=_params("parallel", "arbitrary"),
        name="xattn_prompt",
    )(x, mem_k, mem_v, g_pre, g_post, wq, wo)


def _q_sample_kernel(x_ref, gpre_ref, wq_ref, q_ref):
    q_ref[...] = _dot(_rms(x_ref[...], gpre_ref[...]).astype(BF16), wq_ref[...])


def _split_heads(a):
    n = a.shape[0]
    a = a.reshape(n, -1, N_XHEADS, 2, LANES)
    return jnp.swapaxes(a, 2, 3).reshape(n, -1, LANES)


def _attend_sample_kernel(q_ref, k_ref, v_ref, ones_ref, o_ref):
    rows = N_MEM * HEAD_ROWS
    for b in range(q_ref.shape[0]):
        qv = q_ref[b] * (XHEAD_DIM ** -0.5)
        prod = k_ref[b].reshape(N_MEM, HEAD_ROWS, LANES) * qv[None]
        part = _dot(prod.reshape(rows, LANES).astype(BF16), ones_ref[...]).reshape(N_MEM, HEAD_ROWS, LANES)
        s = part + pltpu.roll(part, N_XHEADS, 1)
        e = jnp.exp(s - jnp.max(s, axis=0, keepdims=True))
        den = jnp.sum(e, axis=0)
        num = jnp.sum(e * v_ref[b].reshape(N_MEM, HEAD_ROWS, LANES), axis=0)
        o_ref[b] = num * (1.0 / den)


def _out_sample_kernel(x_ref, o_ref, gpost_ref, wo_ref, y_ref):
    y = _dot(o_ref[...].astype(BF16), wo_ref[...])
    y_ref[...] = x_ref[...] + _rms(y, gpost_ref[...])


def _xattn_sample(x, cache_k, cache_v, g_pre, g_post, wq, wo):
    n = x.shape[0]
    full = lambda shape: _const_spec(shape)
    q = pl.pallas_call(
        _q_sample_kernel,
        grid=(1,),
        in_specs=[full(x.shape), full((1, D_MODEL)), full((D_MODEL, D_MODEL))],
        out_specs=full(x.shape),
        out_shape=jax.ShapeDtypeStruct(x.shape, F32),
        compiler_params=_params("arbitrary"),
        name="q_sample",
    )(x, g_pre, wq)
    nb = SAMPLE_XATTN_BLOCK
    rows = N_MEM * HEAD_ROWS
    q_spec = pl.BlockSpec((nb, HEAD_ROWS, LANES), lambda i: (i, 0, 0))
    kv_spec = pl.BlockSpec((nb, rows, LANES), lambda i: (i, 0, 0))
    o = pl.pallas_call(
        _attend_sample_kernel,
        grid=(n // nb,),
        in_specs=[q_spec, kv_spec, kv_spec, _const_spec((LANES, LANES))],
        out_specs=q_spec,
        out_shape=jax.ShapeDtypeStruct((n, HEAD_ROWS, LANES), F32),
        compiler_params=_params("parallel"),
        name="attend_sample",
    )(_split_heads(q), cache_k, cache_v, jnp.ones((LANES, LANES), BF16))
    o = o.reshape(n, 2, N_XHEADS, LANES).transpose(0, 2, 1, 3).reshape(n, D_MODEL)
    return pl.pallas_call(
        _out_sample_kernel,
        grid=(1,),
        in_specs=[full(x.shape), full(x.shape), full((1, D_MODEL)), full((D_MODEL, D_MODEL))],
        out_specs=full(x.shape),
        out_shape=jax.ShapeDtypeStruct(x.shape, F32),
        compiler_params=_params("arbitrary"),
        name="out_sample",
    )(x, o, g_post, wo)


def _block_diag(w):
    heads, d, _ = w.shape
    eye = jnp.eye(heads, dtype=w.dtype)
    return (eye[:, None, :, None] * w[:, :, None, :]).reshape(heads * d, heads * d)


def kernel(x_prompt, x_sample, mem_prompt, cache_mem_k, cache_mem_v, state_conv_a, state_conv_b, state_lru, g_ffn1_pre, g_ffn1_post, ffn1_wg, ffn1_wu, ffn1_wd, g_mix_pre, g_mix_post, w_in, conv_a_w, conv_b_w, conv_b_b, lru_wa, lru_ba, lru_wx, lru_bx, lru_lam, w_out, g_xattn_pre, g_xattn_post, g_mem, xattn_wq, xattn_wk, xattn_wv, xattn_wo, g_ffn2_pre, g_ffn2_post, ffn2_wg, ffn2_wu, ffn2_wd):
    batch, seq, _ = x_prompt.shape
    n_s = x_sample.shape[0]
    depth = g_ffn1_pre.shape[0]
    assert depth == 1 and x_sample.shape[1] == 1
    l = 0
    bf = lambda w: w.astype(BF16)
    row = lambda p: p[l].reshape(1, -1)

    yp = x_prompt.reshape(batch * seq, D_MODEL)
    ys = x_sample.reshape(n_s, D_MODEL)

    ffn1 = (row(g_ffn1_pre), row(g_ffn1_post), bf(ffn1_wg[l]), bf(ffn1_wu[l]), bf(ffn1_wd[l]))
    ffn2 = (row(g_ffn2_pre), row(g_ffn2_post), bf(ffn2_wg[l]), bf(ffn2_wu[l]), bf(ffn2_wd[l]))
    wax = jnp.concatenate([_block_diag(lru_wa[l]), _block_diag(lru_wx[l])], axis=1)
    mix_w = (row(g_mix_pre), row(g_mix_post), bf(w_in[l]), conv_a_w[l], conv_b_w[l], row(conv_b_b),
             bf(wax), row(lru_ba), row(lru_bx), row(lru_lam), bf(w_out[l]))

    mk, mv = _memory_kv(mem_prompt.reshape(batch * N_MEM, D_MODEL), row(g_mem), bf(xattn_wk[l]), bf(xattn_wv[l]))

    yp = _ffn(yp, *ffn1, row_tile=FFN_ROW_TILE)
    yp, tail_a, tail_b, tail_h = _mix_prompt(yp, mix_w, batch, seq)
    yp = _xattn_prompt(yp, mk.reshape(batch, N_MEM, D_MODEL), mv.reshape(batch, N_MEM, D_MODEL),
                       row(g_xattn_pre), row(g_xattn_post), bf(xattn_wq[l]), bf(xattn_wo[l]), batch, seq)
    yp = _ffn(yp, *ffn2, row_tile=FFN_ROW_TILE)

    ys = _ffn(ys, *ffn1, row_tile=n_s)
    ys, new_a, new_b, new_h = _mix_sample(ys, state_conv_a[l].reshape(n_s, 2 * D_CONV),
                                          state_conv_b[l].reshape(n_s, 3 * D_LRU), state_lru[l], mix_w)
    ys = _xattn_sample(ys, _split_heads(cache_mem_k[l]), _split_heads(cache_mem_v[l]),
                       row(g_xattn_pre), row(g_xattn_post), bf(xattn_wq[l]), bf(xattn_wo[l]))
    ys = _ffn(ys, *ffn2, row_tile=n_s)

    kv_shape = (1, batch, N_MEM, N_XHEADS, XHEAD_DIM)
    return (yp.reshape(batch, seq, D_MODEL), ys.reshape(n_s, 1, D_MODEL),
            mk.reshape(kv_shape), mv.reshape(kv_shape),
            tail_a[None, :, SUBLANES - 2:, :], tail_b[None, :, SUBLANES - 3:, :], tail_h[None, :, SUBLANES - 1, :],
            new_a.reshape(1, n_s, 2, D_CONV), new_b.reshape(1, n_s, 3, D_LRU), new_h[None])
```

```python
import math

import jax
import jax.numpy as jnp
from jax import lax
from jax.experimental import pallas as pl
from jax.experimental.pallas import tpu as pltpu

D_MODEL = 1024
D_CONV = 512
D_LRU = 512
N_LRU_HEADS = 8
LRU_HEAD_DIM = D_LRU // N_LRU_HEADS
LRU_C = 8.0
CONV_A_WIDTH = 3
CONV_B_WIDTH = 4
N_MIX_PARTS = 5
FFN_DIM = 2816
N_MEM = 256
N_XHEADS = 4
XHEAD_DIM = D_MODEL // N_XHEADS
RMS_EPS = 1e-6

F32 = jnp.float32
BF16 = jnp.bfloat16

V7X_VMEM_LIMIT_BYTES = 56 * 1024 * 1024
SUBLANES = 8
LANES = 128
HEAD_ROWS = N_XHEADS * XHEAD_DIM // LANES

FFN_ROW_TILE = 512
FFN_COL_CHUNK = 256
MIX_STEPS = 32
MIX_BLOCK_STEPS = 128
MIX_CHUNK = 256
N_MIX_CHUNKS = D_CONV // MIX_CHUNK
XATTN_ROW_TILE = 512
KV_ROW_TILE = 512
SAMPLE_XATTN_BLOCK = 4


def _rms(x, g):
    y = x * lax.rsqrt(jnp.mean(x * x, axis=-1, keepdims=True) + RMS_EPS)
    return y * g


def _dot(a, b):
    return jnp.dot(a, b, preferred_element_type=F32)


def _sigmoid(x):
    return 1.0 / (1.0 + jnp.exp(-x))


def _gelu_tanh(x):
    c = math.sqrt(2.0 / math.pi)
    return x * (0.5 * (1.0 + jnp.tanh(c * (x + 0.044715 * (x * x * x)))))


def _expm1(y):
    u = jnp.exp(y)
    stable = (u - 1.0) * y / jnp.log(u)
    return jnp.where(u == 1.0, y, jnp.where(y < -1.0, u - 1.0, stable))


def _log1p(w):
    u = 1.0 + w
    return jnp.where(u == 1.0, w, jnp.log(u) * w / (u - 1.0))


def _softplus(x):
    return jnp.maximum(x, 0.0) + _log1p(jnp.exp(-jnp.abs(x)))


def _const_spec(shape):
    zeros = (0,) * len(shape)
    return pl.BlockSpec(shape, lambda *_: zeros, pipeline_mode=pl.Buffered(1))


def _params(*sem):
    return pltpu.CompilerParams(dimension_semantics=sem, vmem_limit_bytes=V7X_VMEM_LIMIT_BYTES)


def _ffn_kernel(x_ref, gpre_ref, gpost_ref, wg_ref, wu_ref, wd_ref, o_ref, act_ref):
    x = x_ref[...]
    h = _rms(x, gpre_ref[...]).astype(BF16)
    for c in range(FFN_DIM // FFN_COL_CHUNK):
        sl = slice(c * FFN_COL_CHUNK, (c + 1) * FFN_COL_CHUNK)
        g = _dot(h, wg_ref[:, sl])
        u = _dot(h, wu_ref[:, sl])
        act_ref[:, sl] = ((g * _sigmoid(g)) * u).astype(BF16)
    y = _dot(act_ref[...], wd_ref[...])
    o_ref[...] = x + 0.5 * _rms(y, gpost_ref[...])


def _ffn(x, g_pre, g_post, wg, wu, wd, row_tile, time_major_out=None):
    rows = x.shape[0]
    row_spec = pl.BlockSpec((row_tile, D_MODEL), lambda i: (i, 0))
    out_spec, out_shape = row_spec, (rows, D_MODEL)
    if time_major_out is not None:
        batch, seq = time_major_out
        n_t = seq // row_tile
        out_spec = pl.BlockSpec((row_tile, D_MODEL), lambda i: (i % n_t, i // n_t))
        out_shape = (seq, batch * D_MODEL)
    out = pl.pallas_call(
        _ffn_kernel,
        grid=(rows // row_tile,),
        in_specs=[row_spec, _const_spec((1, D_MODEL)), _const_spec((1, D_MODEL)),
                  _const_spec((D_MODEL, FFN_DIM)), _const_spec((D_MODEL, FFN_DIM)),
                  _const_spec((FFN_DIM, D_MODEL))],
        out_specs=out_spec,
        out_shape=jax.ShapeDtypeStruct(out_shape, F32),
        scratch_shapes=[pltpu.VMEM((row_tile, FFN_DIM), BF16)],
        compiler_params=_params("parallel"),
        name="ffn",
    )(x, g_pre, g_post, wg, wu, wd)
    return out.reshape(rows, D_MODEL)


def _chunk(ref, r, c):
    return ref[r:r + 1, c * MIX_CHUNK:(c + 1) * MIX_CHUNK]


def _mix_front(zc, c, prev_a, prev_b, caw_ref, cbw_ref, cbb_ref, wax_ref):
    gb, gc, xa, xb, gg = (zc[:, s * MIX_CHUNK:(s + 1) * MIX_CHUNK] for s in range(N_MIX_PARTS))
    v = gc * xa
    ca = prev_a(v, 2) * _chunk(caw_ref, 0, c) + prev_a(v, 1) * _chunk(caw_ref, 1, c) + v * _chunk(caw_ref, 2, c)
    ya = gb * ca
    cb = (prev_b(xb, 3) * _chunk(cbw_ref, 0, c) + prev_b(xb, 2) * _chunk(cbw_ref, 1, c)
          + prev_b(xb, 1) * _chunk(cbw_ref, 2, c) + xb * _chunk(cbw_ref, 3, c))
    cb = cb + _chunk(cbb_ref, 0, c)
    gates = _dot(cb.astype(BF16), wax_ref[c])
    return v, xb, ya, cb, gates, gg


def _lru_coeffs(gates, c, ba_ref, bx_ref, softplus_neg_lam):
    r = _sigmoid(gates[:, :MIX_CHUNK] + _chunk(ba_ref, 0, c))
    i = _sigmoid(gates[:, MIX_CHUNK:] + _chunk(bx_ref, 0, c))
    log_a = (-LRU_C * r) * softplus_neg_lam[:, c * MIX_CHUNK:(c + 1) * MIX_CHUNK]
    a = jnp.exp(log_a)
    mult = jnp.sqrt(-_expm1(2.0 * log_a))
    return a, mult, i


def _mix_prompt_kernel(x_ref, gpre_ref, gpost_ref, win_ref, caw_ref, cbw_ref, cbb_ref, wax_ref, ba_ref, bx_ref,
                       lam_ref, wout_ref, o_ref, ta_ref, tb_ref, hl_ref, hista_ref, histb_ref, hc_ref, z_ref):
    t = pl.program_id(0)
    nb = SUBLANES
    rows = MIX_STEPS * nb
    hist_rows = (CONV_B_WIDTH - 1) * nb

    @pl.when(t == 0)
    def _():
        hista_ref[...] = jnp.zeros_like(hista_ref)
        histb_ref[...] = jnp.zeros_like(histb_ref)
        hc_ref[...] = jnp.zeros_like(hc_ref)

    def prev(hist):
        return lambda cur, k: jnp.concatenate([hist[hist_rows - k * nb:, :], cur[:rows - k * nb, :]], axis=0)

    sp = _softplus(-lam_ref[...])
    row = lax.broadcasted_iota(jnp.int32, (rows, MIX_CHUNK), 0)
    first_rows = jnp.where(t == 0, nb, 0)
    hist_a = hista_ref[...]
    hist_b = histb_ref[...]
    hcur = hc_ref[...]
    n_sub = x_ref.shape[0] // rows
    items = [(u, c) for u in range(n_sub) for c in range(N_MIX_CHUNKS)]
    xs, normed = {}, {}

    def project(k):
        u, c = items[k]
        if u not in xs:
            xs[u] = x_ref[u * rows:(u + 1) * rows, :]
            normed[u] = _rms(xs[u], gpre_ref[...]).astype(BF16)
        wsl = slice(c * N_MIX_PARTS * MIX_CHUNK, (c + 1) * N_MIX_PARTS * MIX_CHUNK)
        z_ref[k % 2] = _dot(normed[u], win_ref[:, wsl])

    project(0)
    for u in range(n_sub):
        rsl = slice(u * rows, (u + 1) * rows)
        y = None
        tails_a, tails_b, h_last = [], [], []
        for c in range(N_MIX_CHUNKS):
            k = u * N_MIX_CHUNKS + c
            if k + 1 < len(items):
                project(k + 1)
            x = xs[u]
            csl = slice(c * MIX_CHUNK, (c + 1) * MIX_CHUNK)
            zc = z_ref[k % 2]
            v, xb, ya, cb, gates, gg = _mix_front(zc, c, prev(hist_a[:, csl]), prev(hist_b[:, csl]),
                                                  caw_ref, cbw_ref, cbb_ref, wax_ref)
            a, mult, i = _lru_coeffs(gates, c, ba_ref, bx_ref, sp)
            if u == 0:
                mult = jnp.where(row < first_rows, 1.0, mult)
            b = (mult * i) * cb
            hc = hcur[:, csl]
            hs = []
            for j in range(MIX_STEPS):
                sl = slice(j * nb, (j + 1) * nb)
                hc = a[sl, :] * hc + b[sl, :]
                hs.append(hc)
            yb = _gelu_tanh(gg) * jnp.concatenate(hs, axis=0)
            yc = _dot(jnp.concatenate([ya, yb], axis=-1).astype(BF16), wout_ref[c])
            y = yc if y is None else y + yc
            tails_a.append(v[rows - hist_rows:, :])
            tails_b.append(xb[rows - hist_rows:, :])
            h_last.append(hc)
        hist_a = jnp.concatenate(tails_a, axis=-1)
        hist_b = jnp.concatenate(tails_b, axis=-1)
        hcur = jnp.concatenate(h_last, axis=-1)
        o_ref[rsl, :] = x + _rms(y, gpost_ref[...])

    hista_ref[...] = hist_a
    histb_ref[...] = hist_b
    hc_ref[...] = hcur
    ta_ref[...] = hist_a
    tb_ref[...] = hist_b
    hl_ref[...] = hcur


def _mix_weight_specs():
    return [_const_spec((1, D_MODEL)), _const_spec((1, D_MODEL)),
            _const_spec((D_MODEL, N_MIX_PARTS * D_CONV)),
            _const_spec((CONV_A_WIDTH, D_CONV)), _const_spec((CONV_B_WIDTH, D_LRU)), _const_spec((1, D_LRU)),
            _const_spec((N_MIX_CHUNKS, MIX_CHUNK, 2 * MIX_CHUNK)), _const_spec((1, D_LRU)), _const_spec((1, D_LRU)),
            _const_spec((1, D_LRU)), _const_spec((N_MIX_CHUNKS, 2 * MIX_CHUNK, D_MODEL))]


def _mix_prompt(x, weights, batch, seq):
    assert batch == SUBLANES
    block_rows = MIX_BLOCK_STEPS * batch
    hist_rows = (CONV_B_WIDTH - 1) * batch
    x_spec = pl.BlockSpec((block_rows, D_MODEL), lambda t: (t, 0))
    hist_shape = jax.ShapeDtypeStruct((hist_rows, D_CONV), F32)
    h_shape = jax.ShapeDtypeStruct((batch, D_LRU), F32)
    return pl.pallas_call(
        _mix_prompt_kernel,
        grid=(seq // MIX_BLOCK_STEPS,),
        in_specs=[x_spec] + _mix_weight_specs(),
        out_specs=[x_spec, _const_spec(hist_shape.shape), _const_spec(hist_shape.shape), _const_spec(h_shape.shape)],
        out_shape=[jax.ShapeDtypeStruct(x.shape, F32), hist_shape, hist_shape, h_shape],
        scratch_shapes=[pltpu.VMEM(hist_shape.shape, F32), pltpu.VMEM(hist_shape.shape, F32),
                        pltpu.VMEM(h_shape.shape, F32),
                        pltpu.VMEM((2, MIX_STEPS * batch, N_MIX_PARTS * MIX_CHUNK), F32)],
        compiler_params=_params("arbitrary"),
        name="mix_prompt",
    )(x, *weights)


def _mix_sample_kernel(x_ref, sa_ref, sb_ref, h0_ref, gpre_ref, gpost_ref, win_ref, caw_ref, cbw_ref, cbb_ref,
                       wax_ref, ba_ref, bx_ref, lam_ref, wout_ref, o_ref, na_ref, nb_ref, hl_ref):
    x = x_ref[...]
    h = _rms(x, gpre_ref[...]).astype(BF16)
    sp = _softplus(-lam_ref[...])
    y = None
    for c in range(N_MIX_CHUNKS):
        csl = slice(c * MIX_CHUNK, (c + 1) * MIX_CHUNK)
        wsl = slice(c * N_MIX_PARTS * MIX_CHUNK, (c + 1) * N_MIX_PARTS * MIX_CHUNK)
        zc = _dot(h, win_ref[:, wsl])

        def state_a(k):
            o = (CONV_A_WIDTH - 1 - k) * D_CONV + c * MIX_CHUNK
            return sa_ref[:, o:o + MIX_CHUNK]

        def state_b(k):
            o = (CONV_B_WIDTH - 1 - k) * D_LRU + c * MIX_CHUNK
            return sb_ref[:, o:o + MIX_CHUNK]

        v, xb, ya, cb, gates, gg = _mix_front(zc, c, lambda cur, k: state_a(k), lambda cur, k: state_b(k),
                                              caw_ref, cbw_ref, cbb_ref, wax_ref)
        a, mult, i = _lru_coeffs(gates, c, ba_ref, bx_ref, sp)
        hnew = a * h0_ref[:, csl] + (mult * i) * cb
        yb = _gelu_tanh(gg) * hnew
        yc = _dot(jnp.concatenate([ya, yb], axis=-1).astype(BF16), wout_ref[c])
        y = yc if y is None else y + yc

        na_ref[:, c * MIX_CHUNK:(c + 1) * MIX_CHUNK] = state_a(1)
        na_ref[:, D_CONV + c * MIX_CHUNK:D_CONV + (c + 1) * MIX_CHUNK] = v
        nb_ref[:, c * MIX_CHUNK:(c + 1) * MIX_CHUNK] = state_b(2)
        nb_ref[:, D_LRU + c * MIX_CHUNK:D_LRU + (c + 1) * MIX_CHUNK] = state_b(1)
        nb_ref[:, 2 * D_LRU + c * MIX_CHUNK:2 * D_LRU + (c + 1) * MIX_CHUNK] = xb
        hl_ref[:, csl] = hnew
    o_ref[...] = x + _rms(y, gpost_ref[...])


def _mix_sample(x, sa, sb, h0, weights):
    full = lambda a: _const_spec(a.shape)
    return pl.pallas_call(
        _mix_sample_kernel,
        grid=(1,),
        in_specs=[full(x), full(sa), full(sb), full(h0)] + _mix_weight_specs(),
        out_specs=[full(x), full(sa), full(sb), full(h0)],
        out_shape=[jax.ShapeDtypeStruct(x.shape, F32), jax.ShapeDtypeStruct(sa.shape, F32),
                   jax.ShapeDtypeStruct(sb.shape, F32), jax.ShapeDtypeStruct(h0.shape, F32)],
        compiler_params=_params("arbitrary"),
        name="mix_sample",
    )(x, sa, sb, h0, *weights)


def _kv_kernel(m_ref, g_ref, wk_ref, wv_ref, k_ref, v_ref):
    m = _rms(m_ref[...], g_ref[...]).astype(BF16)
    k_ref[...] = _dot(m, wk_ref[...])
    v_ref[...] = _dot(m, wv_ref[...])


def _memory_kv(mem, g_mem, wk, wv):
    rows = mem.shape[0]
    row_spec = pl.BlockSpec((KV_ROW_TILE, D_MODEL), lambda i: (i, 0))
    out = jax.ShapeDtypeStruct((rows, D_MODEL), F32)
    return pl.pallas_call(
        _kv_kernel,
        grid=(rows // KV_ROW_TILE,),
        in_specs=[row_spec, _const_spec((1, D_MODEL)), _const_spec((D_MODEL, D_MODEL)),
                  _const_spec((D_MODEL, D_MODEL))],
        out_specs=[row_spec, row_spec],
        out_shape=[out, out],
        compiler_params=_params("parallel"),
        name="memory_kv",
    )(mem, g_mem, wk, wv)


def _softmax_rows(s):
    e = jnp.exp(s - jnp.max(s, axis=-1, keepdims=True))
    return e * (1.0 / jnp.sum(e, axis=-1, keepdims=True))


def _xattn_prompt_kernel(x_ref, k_ref, v_ref, gpre_ref, gpost_ref, wq_ref, wo_ref, o_ref):
    x = x_ref[...]
    h = _rms(x, gpre_ref[...]).astype(BF16)
    q = _dot(h, wq_ref[...]).astype(BF16)
    heads = []
    for hd in range(N_XHEADS):
        sl = slice(hd * XHEAD_DIM, (hd + 1) * XHEAD_DIM)
        kh = k_ref[0, :, sl].astype(BF16)
        vh = v_ref[0, :, sl].astype(BF16)
        s = lax.dot_general(q[:, sl], kh, (((1,), (1,)), ((), ())), preferred_element_type=F32)
        p = _softmax_rows(s * (XHEAD_DIM ** -0.5)).astype(BF16)
        heads.append(_dot(p, vh).astype(BF16))
    y = _dot(jnp.concatenate(heads, axis=-1), wo_ref[...])
    o_ref[...] = x + _rms(y, gpost_ref[...])


def _xattn_prompt(x, mem_k, mem_v, g_pre, g_post, wq, wo, batch, seq):
    n_t = seq // XATTN_ROW_TILE
    in_spec = pl.BlockSpec((XATTN_ROW_TILE, D_MODEL), lambda b, t: (t, b))
    out_spec = pl.BlockSpec((XATTN_ROW_TILE, D_MODEL), lambda b, t: (b * n_t + t, 0))
    kv_spec = pl.BlockSpec((1, N_MEM, D_MODEL), lambda b, t: (b, 0, 0))
    return pl.pallas_call(
        _xattn_prompt_kernel,
        grid=(batch, n_t),
        in_specs=[in_spec, kv_spec, kv_spec, _const_spec((1, D_MODEL)), _const_spec((1, D_MODEL)),
                  _const_spec((D_MODEL, D_MODEL)), _const_spec((D_MODEL, D_MODEL))],
        out_specs=out_spec,
        out_shape=jax.ShapeDtypeStruct((batch * seq, D_MODEL), F32),
        compiler_params=_params("parallel", "arbitrary"),
        name="xattn_prompt",
    )(x.reshape(seq, batch * D_MODEL), mem_k, mem_v, g_pre, g_post, wq, wo)


def _q_sample_kernel(x_ref, gpre_ref, wq_ref, q_ref):
    q_ref[...] = _dot(_rms(x_ref[...], gpre_ref[...]).astype(BF16), wq_ref[...])


def _split_heads(a):
    n = a.shape[0]
    a = a.reshape(n, -1, N_XHEADS, 2, LANES)
    return jnp.swapaxes(a, 2, 3).reshape(n, -1, LANES)


def _attend_sample_kernel(q_ref, k_ref, v_ref, ones_ref, o_ref):
    rows = N_MEM * HEAD_ROWS
    for b in range(q_ref.shape[0]):
        qv = q_ref[b] * (XHEAD_DIM ** -0.5)
        prod = k_ref[b].reshape(N_MEM, HEAD_ROWS, LANES) * qv[None]
        part = _dot(prod.reshape(rows, LANES).astype(BF16), ones_ref[...]).reshape(N_MEM, HEAD_ROWS, LANES)
        s = part + pltpu.roll(part, N_XHEADS, 1)
        e = jnp.exp(s - jnp.max(s, axis=0, keepdims=True))
        den = jnp.sum(e, axis=0)
        num = jnp.sum(e * v_ref[b].reshape(N_MEM, HEAD_ROWS, LANES), axis=0)
        o_ref[b] = num * (1.0 / den)


def _out_sample_kernel(x_ref, o_ref, gpost_ref, wo_ref, y_ref):
    y = _dot(o_ref[...].astype(BF16), wo_ref[...])
    y_ref[...] = x_ref[...] + _rms(y, gpost_ref[...])


def _xattn_sample(x, cache_k, cache_v, g_pre, g_post, wq, wo):
    n = x.shape[0]
    full = lambda shape: _const_spec(shape)
    q = pl.pallas_call(
        _q_sample_kernel,
        grid=(1,),
        in_specs=[full(x.shape), full((1, D_MODEL)), full((D_MODEL, D_MODEL))],
        out_specs=full(x.shape),
        out_shape=jax.ShapeDtypeStruct(x.shape, F32),
        compiler_params=_params("arbitrary"),
        name="q_sample",
    )(x, g_pre, wq)
    nb = SAMPLE_XATTN_BLOCK
    rows = N_MEM * HEAD_ROWS
    q_spec = pl.BlockSpec((nb, HEAD_ROWS, LANES), lambda i: (i, 0, 0))
    kv_spec = pl.BlockSpec((nb, rows, LANES), lambda i: (i, 0, 0))
    o = pl.pallas_call(
        _attend_sample_kernel,
        grid=(n // nb,),
        in_specs=[q_spec, kv_spec, kv_spec, _const_spec((LANES, LANES))],
        out_specs=q_spec,
        out_shape=jax.ShapeDtypeStruct((n, HEAD_ROWS, LANES), F32),
        compiler_params=_params("parallel"),
        name="attend_sample",
    )(_split_heads(q), cache_k, cache_v, jnp.ones((LANES, LANES), BF16))
    o = o.reshape(n, 2, N_XHEADS, LANES).transpose(0, 2, 1, 3).reshape(n, D_MODEL)
    return pl.pallas_call(
        _out_sample_kernel,
        grid=(1,),
        in_specs=[full(x.shape), full(x.shape), full((1, D_MODEL)), full((D_MODEL, D_MODEL))],
        out_specs=full(x.shape),
        out_shape=jax.ShapeDtypeStruct(x.shape, F32),
        compiler_params=_params("arbitrary"),
        name="out_sample",
    )(x, o, g_post, wo)


def _block_diag(w):
    groups, heads, d, _ = w.shape
    eye = jnp.eye(heads, dtype=w.dtype)
    return (eye[None, :, None, :, None] * w[:, :, :, None, :]).reshape(groups, heads * d, heads * d)


def _mix_weights(g_pre, g_post, w_in, conv_a_w, conv_b_w, conv_b_b, lru_wa, lru_ba, lru_wx, lru_bx, lru_lam, w_out):
    row = lambda p: p.reshape(1, -1)
    win = w_in.reshape(D_MODEL, N_MIX_PARTS, N_MIX_CHUNKS, MIX_CHUNK).transpose(0, 2, 1, 3)
    win = win.reshape(D_MODEL, N_MIX_PARTS * D_CONV).astype(BF16)
    heads_per_chunk = MIX_CHUNK // LRU_HEAD_DIM
    grouped = lambda w: _block_diag(w.reshape(N_MIX_CHUNKS, heads_per_chunk, LRU_HEAD_DIM, LRU_HEAD_DIM))
    wax = jnp.concatenate([grouped(lru_wa), grouped(lru_wx)], axis=-1).astype(BF16)
    wout = w_out.reshape(2, N_MIX_CHUNKS, MIX_CHUNK, D_MODEL).transpose(1, 0, 2, 3)
    wout = wout.reshape(N_MIX_CHUNKS, 2 * MIX_CHUNK, D_MODEL).astype(BF16)
    return (row(g_pre), row(g_post), win, conv_a_w, conv_b_w, row(conv_b_b), wax, row(lru_ba), row(lru_bx),
            row(lru_lam), wout)


def kernel(x_prompt, x_sample, mem_prompt, cache_mem_k, cache_mem_v, state_conv_a, state_conv_b, state_lru, g_ffn1_pre, g_ffn1_post, ffn1_wg, ffn1_wu, ffn1_wd, g_mix_pre, g_mix_post, w_in, conv_a_w, conv_b_w, conv_b_b, lru_wa, lru_ba, lru_wx, lru_bx, lru_lam, w_out, g_xattn_pre, g_xattn_post, g_mem, xattn_wq, xattn_wk, xattn_wv, xattn_wo, g_ffn2_pre, g_ffn2_post, ffn2_wg, ffn2_wu, ffn2_wd):
    batch, seq, _ = x_prompt.shape
    n_s = x_sample.shape[0]
    depth = g_ffn1_pre.shape[0]
    assert depth == 1 and x_sample.shape[1] == 1
    l = 0
    bf = lambda w: w.astype(BF16)
    row = lambda p: p[l].reshape(1, -1)

    yp = x_prompt.reshape(batch * seq, D_MODEL)
    ys = x_sample.reshape(n_s, D_MODEL)

    ffn1 = (row(g_ffn1_pre), row(g_ffn1_post), bf(ffn1_wg[l]), bf(ffn1_wu[l]), bf(ffn1_wd[l]))
    ffn2 = (row(g_ffn2_pre), row(g_ffn2_post), bf(ffn2_wg[l]), bf(ffn2_wu[l]), bf(ffn2_wd[l]))
    mix_w = _mix_weights(g_mix_pre[l], g_mix_post[l], w_in[l], conv_a_w[l], conv_b_w[l], conv_b_b[l], lru_wa[l],
                         lru_ba[l], lru_wx[l], lru_bx[l], lru_lam[l], w_out[l])

    mk, mv = _memory_kv(mem_prompt.reshape(batch * N_MEM, D_MODEL), row(g_mem), bf(xattn_wk[l]), bf(xattn_wv[l]))

    yp = _ffn(yp, *ffn1, row_tile=FFN_ROW_TILE, time_major_out=(batch, seq))
    yp, tail_a, tail_b, tail_h = _mix_prompt(yp, mix_w, batch, seq)
    tail_a = tail_a.reshape(CONV_B_WIDTH - 1, batch, D_CONV)[CONV_B_WIDTH - CONV_A_WIDTH:].transpose(1, 0, 2)
    tail_b = tail_b.reshape(CONV_B_WIDTH - 1, batch, D_LRU).transpose(1, 0, 2)
    yp = _xattn_prompt(yp, mk.reshape(batch, N_MEM, D_MODEL), mv.reshape(batch, N_MEM, D_MODEL),
                       row(g_xattn_pre), row(g_xattn_post), bf(xattn_wq[l]), bf(xattn_wo[l]), batch, seq)
    yp = _ffn(yp, *ffn2, row_tile=FFN_ROW_TILE)

    ys = _ffn(ys, *ffn1, row_tile=n_s)
    ys, new_a, new_b, new_h = _mix_sample(ys, state_conv_a[l].reshape(n_s, 2 * D_CONV),
                                          state_conv_b[l].reshape(n_s, 3 * D_LRU), state_lru[l], mix_w)
    ys = _xattn_sample(ys, _split_heads(cache_mem_k[l]), _split_heads(cache_mem_v[l]),
                       row(g_xattn_pre), row(g_xattn_post), bf(xattn_wq[l]), bf(xattn_wo[l]))
    ys = _ffn(ys, *ffn2, row_tile=n_s)

    kv_shape = (1, batch, N_MEM, N_XHEADS, XHEAD_DIM)
    return (yp.reshape(batch, seq, D_MODEL), ys.reshape(n_s, 1, D_MODEL),
            mk.reshape(kv_shape), mv.reshape(kv_shape),
            tail_a[None], tail_b[None], tail_h[None],
            new_a.reshape(1, n_s, 2, D_CONV), new_b.reshape(1, n_s, 3, D_LRU), new_h[None])
```

```python
import math

import jax
import jax.numpy as jnp
from jax import lax
from jax.experimental import pallas as pl
from jax.experimental.pallas import tpu as pltpu

D_MODEL = 1024
D_CONV = 512
D_LRU = 512
N_LRU_HEADS = 8
LRU_HEAD_DIM = D_LRU // N_LRU_HEADS
LRU_C = 8.0
CONV_A_WIDTH = 3
CONV_B_WIDTH = 4
N_MIX_PARTS = 5
FFN_DIM = 2816
N_MEM = 256
N_XHEADS = 4
XHEAD_DIM = D_MODEL // N_XHEADS
RMS_EPS = 1e-6

F32 = jnp.float32
BF16 = jnp.bfloat16

V7X_VMEM_LIMIT_BYTES = 56 * 1024 * 1024
SUBLANES = 8
LANES = 128
HEAD_ROWS = N_XHEADS * XHEAD_DIM // LANES

FFN_ROW_TILE = 512
FFN_COL_CHUNK = 256
MIX_STEPS = 32
MIX_BLOCK_STEPS = 128
MIX_CHUNK = 256
N_MIX_CHUNKS = D_CONV // MIX_CHUNK
XATTN_ROW_TILE = 512
KV_ROW_TILE = 512
SAMPLE_XATTN_BLOCK = 4


def _rms(x, g):
    y = x * lax.rsqrt(jnp.mean(x * x, axis=-1, keepdims=True) + RMS_EPS)
    return y * g


def _dot(a, b):
    return jnp.dot(a, b, preferred_element_type=F32)


def _sigmoid(x):
    return 1.0 / (1.0 + jnp.exp(-x))


def _gelu_tanh(x):
    c = math.sqrt(2.0 / math.pi)
    return x * (0.5 * (1.0 + jnp.tanh(c * (x + 0.044715 * (x * x * x)))))


def _expm1(y):
    u = jnp.exp(y)
    stable = (u - 1.0) * y / jnp.log(u)
    return jnp.where(u == 1.0, y, jnp.where(y < -1.0, u - 1.0, stable))


def _log1p(w):
    u = 1.0 + w
    return jnp.where(u == 1.0, w, jnp.log(u) * w / (u - 1.0))


def _softplus(x):
    return jnp.maximum(x, 0.0) + _log1p(jnp.exp(-jnp.abs(x)))


def _const_spec(shape):
    zeros = (0,) * len(shape)
    return pl.BlockSpec(shape, lambda *_: zeros, pipeline_mode=pl.Buffered(1))


def _params(*sem):
    return pltpu.CompilerParams(dimension_semantics=sem, vmem_limit_bytes=V7X_VMEM_LIMIT_BYTES)


def _ffn_kernel(x_ref, gpre_ref, gpost_ref, wg_ref, wu_ref, wd_ref, o_ref, act_ref):
    x = x_ref[...]
    h = _rms(x, gpre_ref[...]).astype(BF16)
    for c in range(FFN_DIM // FFN_COL_CHUNK):
        sl = slice(c * FFN_COL_CHUNK, (c + 1) * FFN_COL_CHUNK)
        g = _dot(h, wg_ref[:, sl])
        u = _dot(h, wu_ref[:, sl])
        act_ref[:, sl] = ((g * _sigmoid(g)) * u).astype(BF16)
    y = _dot(act_ref[...], wd_ref[...])
    o_ref[...] = x + 0.5 * _rms(y, gpost_ref[...])


def _ffn(x, g_pre, g_post, wg, wu, wd, row_tile):
    rows = x.shape[0]
    row_spec = pl.BlockSpec((row_tile, D_MODEL), lambda i: (i, 0))
    return pl.pallas_call(
        _ffn_kernel,
        grid=(rows // row_tile,),
        in_specs=[row_spec, _const_spec((1, D_MODEL)), _const_spec((1, D_MODEL)),
                  _const_spec((D_MODEL, FFN_DIM)), _const_spec((D_MODEL, FFN_DIM)),
                  _const_spec((FFN_DIM, D_MODEL))],
        out_specs=row_spec,
        out_shape=jax.ShapeDtypeStruct((rows, D_MODEL), F32),
        scratch_shapes=[pltpu.VMEM((row_tile, FFN_DIM), BF16)],
        compiler_params=_params("parallel"),
        name="ffn",
    )(x, g_pre, g_post, wg, wu, wd)


def _chunk(ref, r, c):
    return ref[r:r + 1, c * MIX_CHUNK:(c + 1) * MIX_CHUNK]


def _mix_front(zc, c, prev_a, prev_b, caw_ref, cbw_ref, cbb_ref, wax_ref):
    gb, gc, xa, xb, gg = (zc[:, s * MIX_CHUNK:(s + 1) * MIX_CHUNK] for s in range(N_MIX_PARTS))
    v = gc * xa
    ca = prev_a(v, 2) * _chunk(caw_ref, 0, c) + prev_a(v, 1) * _chunk(caw_ref, 1, c) + v * _chunk(caw_ref, 2, c)
    ya = gb * ca
    cb = (prev_b(xb, 3) * _chunk(cbw_ref, 0, c) + prev_b(xb, 2) * _chunk(cbw_ref, 1, c)
          + prev_b(xb, 1) * _chunk(cbw_ref, 2, c) + xb * _chunk(cbw_ref, 3, c))
    cb = cb + _chunk(cbb_ref, 0, c)
    gates = _dot(cb.astype(BF16), wax_ref[c])
    return v, xb, ya, cb, gates, gg


def _lru_coeffs(gates, c, ba_ref, bx_ref, softplus_neg_lam):
    r = _sigmoid(gates[:, :MIX_CHUNK] + _chunk(ba_ref, 0, c))
    i = _sigmoid(gates[:, MIX_CHUNK:] + _chunk(bx_ref, 0, c))
    log_a = (-LRU_C * r) * softplus_neg_lam[:, c * MIX_CHUNK:(c + 1) * MIX_CHUNK]
    a = jnp.exp(log_a)
    mult = jnp.sqrt(-_expm1(2.0 * log_a))
    return a, mult, i


def _seq_copies(hbm_ref, buf_ref, sem_ref, slot, step_block, to_vmem):
    copies = []
    for b in range(SUBLANES):
        hbm = hbm_ref.at[b, pl.ds(step_block * MIX_BLOCK_STEPS, MIX_BLOCK_STEPS), :]
        vmem = buf_ref.at[slot, :, b, :]
        src, dst = (hbm, vmem) if to_vmem else (vmem, hbm)
        copies.append(pltpu.make_async_copy(src, dst, sem_ref.at[slot, b]))
    return copies


def _mix_prompt_kernel(x_hbm, gpre_ref, gpost_ref, win_ref, caw_ref, cbw_ref, cbb_ref, wax_ref, ba_ref, bx_ref,
                       lam_ref, wout_ref, o_hbm, ta_ref, tb_ref, hl_ref, hista_ref, histb_ref, hc_ref, z_ref,
                       xbuf_ref, obuf_ref, xsem_ref, osem_ref):
    t = pl.program_id(0)
    n_t = pl.num_programs(0)
    slot = t % 2
    nb = SUBLANES
    rows = MIX_STEPS * nb
    hist_rows = (CONV_B_WIDTH - 1) * nb

    @pl.when(t == 0)
    def _():
        hista_ref[...] = jnp.zeros_like(hista_ref)
        histb_ref[...] = jnp.zeros_like(histb_ref)
        hc_ref[...] = jnp.zeros_like(hc_ref)
        for cp in _seq_copies(x_hbm, xbuf_ref, xsem_ref, 0, 0, True):
            cp.start()

    @pl.when(t + 1 < n_t)
    def _():
        for cp in _seq_copies(x_hbm, xbuf_ref, xsem_ref, 1 - slot, t + 1, True):
            cp.start()

    for cp in _seq_copies(x_hbm, xbuf_ref, xsem_ref, slot, t, True):
        cp.wait()

    @pl.when(t >= 2)
    def _():
        for cp in _seq_copies(o_hbm, obuf_ref, osem_ref, slot, t - 2, False):
            cp.wait()

    def prev(hist):
        return lambda cur, k: jnp.concatenate([hist[hist_rows - k * nb:, :], cur[:rows - k * nb, :]], axis=0)

    sp = _softplus(-lam_ref[...])
    row = lax.broadcasted_iota(jnp.int32, (rows, MIX_CHUNK), 0)
    first_rows = jnp.where(t == 0, nb, 0)
    hist_a = hista_ref[...]
    hist_b = histb_ref[...]
    hcur = hc_ref[...]
    n_sub = MIX_BLOCK_STEPS // MIX_STEPS
    items = [(u, c) for u in range(n_sub) for c in range(N_MIX_CHUNKS)]
    xs, normed = {}, {}

    def project(k):
        u, c = items[k]
        if u not in xs:
            xs[u] = xbuf_ref[slot, u * MIX_STEPS:(u + 1) * MIX_STEPS].reshape(rows, D_MODEL)
            normed[u] = _rms(xs[u], gpre_ref[...]).astype(BF16)
        wsl = slice(c * N_MIX_PARTS * MIX_CHUNK, (c + 1) * N_MIX_PARTS * MIX_CHUNK)
        z_ref[k % 2] = _dot(normed[u], win_ref[:, wsl])

    project(0)
    for u in range(n_sub):
        y = None
        tails_a, tails_b, h_last = [], [], []
        for c in range(N_MIX_CHUNKS):
            k = u * N_MIX_CHUNKS + c
            if k + 1 < len(items):
                project(k + 1)
            x = xs[u]
            csl = slice(c * MIX_CHUNK, (c + 1) * MIX_CHUNK)
            zc = z_ref[k % 2]
            v, xb, ya, cb, gates, gg = _mix_front(zc, c, prev(hist_a[:, csl]), prev(hist_b[:, csl]),
                                                  caw_ref, cbw_ref, cbb_ref, wax_ref)
            a, mult, i = _lru_coeffs(gates, c, ba_ref, bx_ref, sp)
            if u == 0:
                mult = jnp.where(row < first_rows, 1.0, mult)
            b = (mult * i) * cb
            hc = hcur[:, csl]
            hs = []
            for j in range(MIX_STEPS):
                sl = slice(j * nb, (j + 1) * nb)
                hc = a[sl, :] * hc + b[sl, :]
                hs.append(hc)
            yb = _gelu_tanh(gg) * jnp.concatenate(hs, axis=0)
            yc = _dot(jnp.concatenate([ya, yb], axis=-1).astype(BF16), wout_ref[c])
            y = yc if y is None else y + yc
            tails_a.append(v[rows - hist_rows:, :])
            tails_b.append(xb[rows - hist_rows:, :])
            h_last.append(hc)
        hist_a = jnp.concatenate(tails_a, axis=-1)
        hist_b = jnp.concatenate(tails_b, axis=-1)
        hcur = jnp.concatenate(h_last, axis=-1)
        out = x + _rms(y, gpost_ref[...])
        obuf_ref[slot, u * MIX_STEPS:(u + 1) * MIX_STEPS] = out.reshape(MIX_STEPS, nb, D_MODEL)

    for cp in _seq_copies(o_hbm, obuf_ref, osem_ref, slot, t, False):
        cp.start()

    @pl.when(t == n_t - 1)
    def _():
        for cp in _seq_copies(o_hbm, obuf_ref, osem_ref, slot, t, False):
            cp.wait()

    @pl.when((t == n_t - 1) & (t >= 1))
    def _():
        for cp in _seq_copies(o_hbm, obuf_ref, osem_ref, 1 - slot, t - 1, False):
            cp.wait()

    hista_ref[...] = hist_a
    histb_ref[...] = hist_b
    hc_ref[...] = hcur
    ta_ref[...] = hist_a
    tb_ref[...] = hist_b
    hl_ref[...] = hcur


def _mix_weight_specs():
    return [_const_spec((1, D_MODEL)), _const_spec((1, D_MODEL)),
            _const_spec((D_MODEL, N_MIX_PARTS * D_CONV)),
            _const_spec((CONV_A_WIDTH, D_CONV)), _const_spec((CONV_B_WIDTH, D_LRU)), _const_spec((1, D_LRU)),
            _const_spec((N_MIX_CHUNKS, MIX_CHUNK, 2 * MIX_CHUNK)), _const_spec((1, D_LRU)), _const_spec((1, D_LRU)),
            _const_spec((1, D_LRU)), _const_spec((N_MIX_CHUNKS, 2 * MIX_CHUNK, D_MODEL))]


def _mix_prompt(x, weights, batch, seq):
    assert batch == SUBLANES and seq % MIX_BLOCK_STEPS == 0
    hist_rows = (CONV_B_WIDTH - 1) * batch
    hbm_spec = pl.BlockSpec(memory_space=pl.ANY)
    hist_shape = jax.ShapeDtypeStruct((hist_rows, D_CONV), F32)
    h_shape = jax.ShapeDtypeStruct((batch, D_LRU), F32)
    io_buf = pltpu.VMEM((2, MIX_BLOCK_STEPS, batch, D_MODEL), F32)
    return pl.pallas_call(
        _mix_prompt_kernel,
        grid=(seq // MIX_BLOCK_STEPS,),
        in_specs=[hbm_spec] + _mix_weight_specs(),
        out_specs=[hbm_spec, _const_spec(hist_shape.shape), _const_spec(hist_shape.shape), _const_spec(h_shape.shape)],
        out_shape=[jax.ShapeDtypeStruct(x.shape, F32), hist_shape, hist_shape, h_shape],
        scratch_shapes=[pltpu.VMEM(hist_shape.shape, F32), pltpu.VMEM(hist_shape.shape, F32),
                        pltpu.VMEM(h_shape.shape, F32),
                        pltpu.VMEM((2, MIX_STEPS * batch, N_MIX_PARTS * MIX_CHUNK), F32),
                        io_buf, io_buf, pltpu.SemaphoreType.DMA((2, batch)), pltpu.SemaphoreType.DMA((2, batch))],
        compiler_params=_params("arbitrary"),
        name="mix_prompt",
    )(x, *weights)


def _mix_sample_kernel(x_ref, sa_ref, sb_ref, h0_ref, gpre_ref, gpost_ref, win_ref, caw_ref, cbw_ref, cbb_ref,
                       wax_ref, ba_ref, bx_ref, lam_ref, wout_ref, o_ref, na_ref, nb_ref, hl_ref):
    x = x_ref[...]
    h = _rms(x, gpre_ref[...]).astype(BF16)
    sp = _softplus(-lam_ref[...])
    y = None
    for c in range(N_MIX_CHUNKS):
        csl = slice(c * MIX_CHUNK, (c + 1) * MIX_CHUNK)
        wsl = slice(c * N_MIX_PARTS * MIX_CHUNK, (c + 1) * N_MIX_PARTS * MIX_CHUNK)
        zc = _dot(h, win_ref[:, wsl])

        def state_a(k):
            o = (CONV_A_WIDTH - 1 - k) * D_CONV + c * MIX_CHUNK
            return sa_ref[:, o:o + MIX_CHUNK]

        def state_b(k):
            o = (CONV_B_WIDTH - 1 - k) * D_LRU + c * MIX_CHUNK
            return sb_ref[:, o:o + MIX_CHUNK]

        v, xb, ya, cb, gates, gg = _mix_front(zc, c, lambda cur, k: state_a(k), lambda cur, k: state_b(k),
                                              caw_ref, cbw_ref, cbb_ref, wax_ref)
        a, mult, i = _lru_coeffs(gates, c, ba_ref, bx_ref, sp)
        hnew = a * h0_ref[:, csl] + (mult * i) * cb
        yb = _gelu_tanh(gg) * hnew
        yc = _dot(jnp.concatenate([ya, yb], axis=-1).astype(BF16), wout_ref[c])
        y = yc if y is None else y + yc

        na_ref[:, c * MIX_CHUNK:(c + 1) * MIX_CHUNK] = state_a(1)
        na_ref[:, D_CONV + c * MIX_CHUNK:D_CONV + (c + 1) * MIX_CHUNK] = v
        nb_ref[:, c * MIX_CHUNK:(c + 1) * MIX_CHUNK] = state_b(2)
        nb_ref[:, D_LRU + c * MIX_CHUNK:D_LRU + (c + 1) * MIX_CHUNK] = state_b(1)
        nb_ref[:, 2 * D_LRU + c * MIX_CHUNK:2 * D_LRU + (c + 1) * MIX_CHUNK] = xb
        hl_ref[:, csl] = hnew
    o_ref[...] = x + _rms(y, gpost_ref[...])


def _mix_sample(x, sa, sb, h0, weights):
    full = lambda a: _const_spec(a.shape)
    return pl.pallas_call(
        _mix_sample_kernel,
        grid=(1,),
        in_specs=[full(x), full(sa), full(sb), full(h0)] + _mix_weight_specs(),
        out_specs=[full(x), full(sa), full(sb), full(h0)],
        out_shape=[jax.ShapeDtypeStruct(x.shape, F32), jax.ShapeDtypeStruct(sa.shape, F32),
                   jax.ShapeDtypeStruct(sb.shape, F32), jax.ShapeDtypeStruct(h0.shape, F32)],
        compiler_params=_params("arbitrary"),
        name="mix_sample",
    )(x, sa, sb, h0, *weights)


def _kv_kernel(m_ref, g_ref, wk_ref, wv_ref, k_ref, v_ref):
    m = _rms(m_ref[...], g_ref[...]).astype(BF16)
    k_ref[...] = _dot(m, wk_ref[...])
    v_ref[...] = _dot(m, wv_ref[...])


def _memory_kv(mem, g_mem, wk, wv):
    rows = mem.shape[0]
    row_spec = pl.BlockSpec((KV_ROW_TILE, D_MODEL), lambda i: (i, 0))
    out = jax.ShapeDtypeStruct((rows, D_MODEL), F32)
    return pl.pallas_call(
        _kv_kernel,
        grid=(rows // KV_ROW_TILE,),
        in_specs=[row_spec, _const_spec((1, D_MODEL)), _const_spec((D_MODEL, D_MODEL)),
                  _const_spec((D_MODEL, D_MODEL))],
        out_specs=[row_spec, row_spec],
        out_shape=[out, out],
        compiler_params=_params("parallel"),
        name="memory_kv",
    )(mem, g_mem, wk, wv)


def _softmax_rows(s):
    e = jnp.exp(s - jnp.max(s, axis=-1, keepdims=True))
    return e * (1.0 / jnp.sum(e, axis=-1, keepdims=True))


def _xattn_prompt_kernel(x_ref, k_ref, v_ref, gpre_ref, gpost_ref, wq_ref, wo_ref, o_ref):
    x = x_ref[...]
    h = _rms(x, gpre_ref[...]).astype(BF16)
    q = _dot(h, wq_ref[...]).astype(BF16)
    heads = []
    for hd in range(N_XHEADS):
        sl = slice(hd * XHEAD_DIM, (hd + 1) * XHEAD_DIM)
        kh = k_ref[0, :, sl].astype(BF16)
        vh = v_ref[0, :, sl].astype(BF16)
        s = lax.dot_general(q[:, sl], kh, (((1,), (1,)), ((), ())), preferred_element_type=F32)
        p = _softmax_rows(s * (XHEAD_DIM ** -0.5)).astype(BF16)
        heads.append(_dot(p, vh).astype(BF16))
    y = _dot(jnp.concatenate(heads, axis=-1), wo_ref[...])
    o_ref[...] = x + _rms(y, gpost_ref[...])


def _xattn_prompt(x, mem_k, mem_v, g_pre, g_post, wq, wo, batch, seq):
    n_t = seq // XATTN_ROW_TILE
    row_spec = pl.BlockSpec((XATTN_ROW_TILE, D_MODEL), lambda b, t: (b * n_t + t, 0))
    kv_spec = pl.BlockSpec((1, N_MEM, D_MODEL), lambda b, t: (b, 0, 0))
    return pl.pallas_call(
        _xattn_prompt_kernel,
        grid=(batch, n_t),
        in_specs=[row_spec, kv_spec, kv_spec, _const_spec((1, D_MODEL)), _const_spec((1, D_MODEL)),
                  _const_spec((D_MODEL, D_MODEL)), _const_spec((D_MODEL, D_MODEL))],
        out_specs=row_spec,
        out_shape=jax.ShapeDtypeStruct(x.shape, F32),
        compiler_params=_params("parallel", "arbitrary"),
        name="xattn_prompt",
    )(x, mem_k, mem_v, g_pre, g_post, wq, wo)


def _q_sample_kernel(x_ref, gpre_ref, wq_ref, q_ref):
    q_ref[...] = _dot(_rms(x_ref[...], gpre_ref[...]).astype(BF16), wq_ref[...])


def _split_heads(a):
    n = a.shape[0]
    a = a.reshape(n, -1, N_XHEADS, 2, LANES)
    return jnp.swapaxes(a, 2, 3).reshape(n, -1, LANES)


def _attend_sample_kernel(q_ref, k_ref, v_ref, ones_ref, o_ref):
    rows = N_MEM * HEAD_ROWS
    for b in range(q_ref.shape[0]):
        qv = q_ref[b] * (XHEAD_DIM ** -0.5)
        prod = k_ref[b].reshape(N_MEM, HEAD_ROWS, LANES) * qv[None]
        part = _dot(prod.reshape(rows, LANES).astype(BF16), ones_ref[...]).reshape(N_MEM, HEAD_ROWS, LANES)
        s = part + pltpu.roll(part, N_XHEADS, 1)
        e = jnp.exp(s - jnp.max(s, axis=0, keepdims=True))
        den = jnp.sum(e, axis=0)
        num = jnp.sum(e * v_ref[b].reshape(N_MEM, HEAD_ROWS, LANES), axis=0)
        o_ref[b] = num * (1.0 / den)


def _out_sample_kernel(x_ref, o_ref, gpost_ref, wo_ref, y_ref):
    y = _dot(o_ref[...].astype(BF16), wo_ref[...])
    y_ref[...] = x_ref[...] + _rms(y, gpost_ref[...])


def _xattn_sample(x, cache_k, cache_v, g_pre, g_post, wq, wo):
    n = x.shape[0]
    full = lambda shape: _const_spec(shape)
    q = pl.pallas_call(
        _q_sample_kernel,
        grid=(1,),
        in_specs=[full(x.shape), full((1, D_MODEL)), full((D_MODEL, D_MODEL))],
        out_specs=full(x.shape),
        out_shape=jax.ShapeDtypeStruct(x.shape, F32),
        compiler_params=_params("arbitrary"),
        name="q_sample",
    )(x, g_pre, wq)
    nb = SAMPLE_XATTN_BLOCK
    rows = N_MEM * HEAD_ROWS
    q_spec = pl.BlockSpec((nb, HEAD_ROWS, LANES), lambda i: (i, 0, 0))
    kv_spec = pl.BlockSpec((nb, rows, LANES), lambda i: (i, 0, 0))
    o = pl.pallas_call(
        _attend_sample_kernel,
        grid=(n // nb,),
        in_specs=[q_spec, kv_spec, kv_spec, _const_spec((LANES, LANES))],
        out_specs=q_spec,
        out_shape=jax.ShapeDtypeStruct((n, HEAD_ROWS, LANES), F32),
        compiler_params=_params("parallel"),
        name="attend_sample",
    )(_split_heads(q), cache_k, cache_v, jnp.ones((LANES, LANES), BF16))
    o = o.reshape(n, 2, N_XHEADS, LANES).transpose(0, 2, 1, 3).reshape(n, D_MODEL)
    return pl.pallas_call(
        _out_sample_kernel,
        grid=(1,),
        in_specs=[full(x.shape), full(x.shape), full((1, D_MODEL)), full((D_MODEL, D_MODEL))],
        out_specs=full(x.shape),
        out_shape=jax.ShapeDtypeStruct(x.shape, F32),
        compiler_params=_params("arbitrary"),
        name="out_sample",
    )(x, o, g_post, wo)


def _block_diag(w):
    groups, heads, d, _ = w.shape
    eye = jnp.eye(heads, dtype=w.dtype)
    return (eye[None, :, None, :, None] * w[:, :, :, None, :]).reshape(groups, heads * d, heads * d)


def _mix_weights(g_pre, g_post, w_in, conv_a_w, conv_b_w, conv_b_b, lru_wa, lru_ba, lru_wx, lru_bx, lru_lam, w_out):
    row = lambda p: p.reshape(1, -1)
    win = w_in.reshape(D_MODEL, N_MIX_PARTS, N_MIX_CHUNKS, MIX_CHUNK).transpose(0, 2, 1, 3)
    win = win.reshape(D_MODEL, N_MIX_PARTS * D_CONV).astype(BF16)
    heads_per_chunk = MIX_CHUNK // LRU_HEAD_DIM
    grouped = lambda w: _block_diag(w.reshape(N_MIX_CHUNKS, heads_per_chunk, LRU_HEAD_DIM, LRU_HEAD_DIM))
    wax = jnp.concatenate([grouped(lru_wa), grouped(lru_wx)], axis=-1).astype(BF16)
    wout = w_out.reshape(2, N_MIX_CHUNKS, MIX_CHUNK, D_MODEL).transpose(1, 0, 2, 3)
    wout = wout.reshape(N_MIX_CHUNKS, 2 * MIX_CHUNK, D_MODEL).astype(BF16)
    return (row(g_pre), row(g_post), win, conv_a_w, conv_b_w, row(conv_b_b), wax, row(lru_ba), row(lru_bx),
            row(lru_lam), wout)


def kernel(x_prompt, x_sample, mem_prompt, cache_mem_k, cache_mem_v, state_conv_a, state_conv_b, state_lru, g_ffn1_pre, g_ffn1_post, ffn1_wg, ffn1_wu, ffn1_wd, g_mix_pre, g_mix_post, w_in, conv_a_w, conv_b_w, conv_b_b, lru_wa, lru_ba, lru_wx, lru_bx, lru_lam, w_out, g_xattn_pre, g_xattn_post, g_mem, xattn_wq, xattn_wk, xattn_wv, xattn_wo, g_ffn2_pre, g_ffn2_post, ffn2_wg, ffn2_wu, ffn2_wd):
    batch, seq, _ = x_prompt.shape
    n_s = x_sample.shape[0]
    depth = g_ffn1_pre.shape[0]
    assert depth == 1 and x_sample.shape[1] == 1
    l = 0
    bf = lambda w: w.astype(BF16)
    row = lambda p: p[l].reshape(1, -1)

    yp = x_prompt.reshape(batch * seq, D_MODEL)
    ys = x_sample.reshape(n_s, D_MODEL)

    ffn1 = (row(g_ffn1_pre), row(g_ffn1_post), bf(ffn1_wg[l]), bf(ffn1_wu[l]), bf(ffn1_wd[l]))
    ffn2 = (row(g_ffn2_pre), row(g_ffn2_post), bf(ffn2_wg[l]), bf(ffn2_wu[l]), bf(ffn2_wd[l]))
    mix_w = _mix_weights(g_mix_pre[l], g_mix_post[l], w_in[l], conv_a_w[l], conv_b_w[l], conv_b_b[l], lru_wa[l],
                         lru_ba[l], lru_wx[l], lru_bx[l], lru_lam[l], w_out[l])

    mk, mv = _memory_kv(mem_prompt.reshape(batch * N_MEM, D_MODEL), row(g_mem), bf(xattn_wk[l]), bf(xattn_wv[l]))

    yp = _ffn(yp, *ffn1, row_tile=FFN_ROW_TILE)
    yp, tail_a, tail_b, tail_h = _mix_prompt(yp.reshape(batch, seq, D_MODEL), mix_w, batch, seq)
    yp = yp.reshape(batch * seq, D_MODEL)
    tail_a = tail_a.reshape(CONV_B_WIDTH - 1, batch, D_CONV)[CONV_B_WIDTH - CONV_A_WIDTH:].transpose(1, 0, 2)
    tail_b = tail_b.reshape(CONV_B_WIDTH - 1, batch, D_LRU).transpose(1, 0, 2)
    yp = _xattn_prompt(yp, mk.reshape(batch, N_MEM, D_MODEL), mv.reshape(batch, N_MEM, D_MODEL),
                       row(g_xattn_pre), row(g_xattn_post), bf(xattn_wq[l]), bf(xattn_wo[l]), batch, seq)
    yp = _ffn(yp, *ffn2, row_tile=FFN_ROW_TILE)

    ys = _ffn(ys, *ffn1, row_tile=n_s)
    ys, new_a, new_b, new_h = _mix_sample(ys, state_conv_a[l].reshape(n_s, 2 * D_CONV),
                                          state_conv_b[l].reshape(n_s, 3 * D_LRU), state_lru[l], mix_w)
    ys = _xattn_sample(ys, _split_heads(cache_mem_k[l]), _split_heads(cache_mem_v[l]),
                       row(g_xattn_pre), row(g_xattn_post), bf(xattn_wq[l]), bf(xattn_wo[l]))
    ys = _ffn(ys, *ffn2, row_tile=n_s)

    kv_shape = (1, batch, N_MEM, N_XHEADS, XHEAD_DIM)
    return (yp.reshape(batch, seq, D_MODEL), ys.reshape(n_s, 1, D_MODEL),
            mk.reshape(kv_shape), mv.reshape(kv_shape),
            tail_a[None], tail_b[None], tail_h[None],
            new_a.reshape(1, n_s, 2, D_CONV), new_b.reshape(1, n_s, 3, D_LRU), new_h[None])
```

```python
import math

import jax
import jax.numpy as jnp
from jax import lax
from jax.experimental import pallas as pl
from jax.experimental.pallas import tpu as pltpu

D_MODEL = 1024
D_CONV = 512
D_LRU = 512
N_LRU_HEADS = 8
LRU_HEAD_DIM = D_LRU // N_LRU_HEADS
LRU_C = 8.0
CONV_A_WIDTH = 3
CONV_B_WIDTH = 4
N_MIX_PARTS = 5
FFN_DIM = 2816
N_MEM = 256
N_XHEADS = 4
XHEAD_DIM = D_MODEL // N_XHEADS
RMS_EPS = 1e-6

F32 = jnp.float32
BF16 = jnp.bfloat16

V7X_VMEM_LIMIT_BYTES = 56 * 1024 * 1024
SUBLANES = 8
LANES = 128
HEAD_ROWS = N_XHEADS * XHEAD_DIM // LANES

FFN_ROW_TILE = 512
FFN_COL_CHUNK = 256
MIX_STEPS = 32
MIX_BLOCK_STEPS = 128
MIX_CHUNK = 256
N_MIX_CHUNKS = D_CONV // MIX_CHUNK
XATTN_ROW_TILE = 512
KV_ROW_TILE = 512
SAMPLE_XATTN_BLOCK = 4


def _rms(x, g):
    y = x * lax.rsqrt(jnp.mean(x * x, axis=-1, keepdims=True) + RMS_EPS)
    return y * g


def _dot(a, b):
    return jnp.dot(a, b, preferred_element_type=F32)


def _sigmoid(x):
    return 1.0 / (1.0 + jnp.exp(-x))


def _gelu_tanh(x):
    c = math.sqrt(2.0 / math.pi)
    return x * (0.5 * (1.0 + jnp.tanh(c * (x + 0.044715 * (x * x * x)))))


def _expm1(y):
    u = jnp.exp(y)
    stable = (u - 1.0) * y / jnp.log(u)
    return jnp.where(u == 1.0, y, jnp.where(y < -1.0, u - 1.0, stable))


def _log1p(w):
    u = 1.0 + w
    return jnp.where(u == 1.0, w, jnp.log(u) * w / (u - 1.0))


def _softplus(x):
    return jnp.maximum(x, 0.0) + _log1p(jnp.exp(-jnp.abs(x)))


def _const_spec(shape):
    zeros = (0,) * len(shape)
    return pl.BlockSpec(shape, lambda *_: zeros, pipeline_mode=pl.Buffered(1))


def _params(*sem):
    return pltpu.CompilerParams(dimension_semantics=sem, vmem_limit_bytes=V7X_VMEM_LIMIT_BYTES)


def _ffn_kernel(x_ref, gpre_ref, gpost_ref, wg_ref, wu_ref, wd_ref, o_ref, act_ref):
    x = x_ref[...]
    h = _rms(x, gpre_ref[...]).astype(BF16)
    for c in range(FFN_DIM // FFN_COL_CHUNK):
        sl = slice(c * FFN_COL_CHUNK, (c + 1) * FFN_COL_CHUNK)
        g = _dot(h, wg_ref[:, sl].astype(BF16))
        u = _dot(h, wu_ref[:, sl].astype(BF16))
        act_ref[:, sl] = ((g * _sigmoid(g)) * u).astype(BF16)
    y = _dot(act_ref[...], wd_ref[...].astype(BF16))
    o_ref[...] = x + 0.5 * _rms(y, gpost_ref[...])


def _ffn(x, g_pre, g_post, wg, wu, wd, row_tile):
    rows = x.shape[0]
    row_spec = pl.BlockSpec((row_tile, D_MODEL), lambda i: (i, 0))
    return pl.pallas_call(
        _ffn_kernel,
        grid=(rows // row_tile,),
        in_specs=[row_spec, _const_spec((1, D_MODEL)), _const_spec((1, D_MODEL)),
                  _const_spec((D_MODEL, FFN_DIM)), _const_spec((D_MODEL, FFN_DIM)),
                  _const_spec((FFN_DIM, D_MODEL))],
        out_specs=row_spec,
        out_shape=jax.ShapeDtypeStruct((rows, D_MODEL), F32),
        scratch_shapes=[pltpu.VMEM((row_tile, FFN_DIM), BF16)],
        compiler_params=_params("parallel"),
        name="ffn",
    )(x, g_pre, g_post, wg, wu, wd)


def _chunk(ref, r, c):
    return ref[r:r + 1, c * MIX_CHUNK:(c + 1) * MIX_CHUNK]


def _mix_project(h, win_ref, c):
    parts = []
    for s in range(N_MIX_PARTS):
        o = s * D_CONV + c * MIX_CHUNK
        parts.append(_dot(h, win_ref[:, o:o + MIX_CHUNK].astype(BF16)))
    return jnp.concatenate(parts, axis=-1)


def _mix_out(ya, yb, wout_ref, c):
    ra = c * MIX_CHUNK
    rb = D_CONV + c * MIX_CHUNK
    return (_dot(ya.astype(BF16), wout_ref[ra:ra + MIX_CHUNK, :].astype(BF16))
            + _dot(yb.astype(BF16), wout_ref[rb:rb + MIX_CHUNK, :].astype(BF16)))


def _mix_front(zc, c, prev_a, prev_b, caw_ref, cbw_ref, cbb_ref, wax_ref):
    gb, gc, xa, xb, gg = (zc[:, s * MIX_CHUNK:(s + 1) * MIX_CHUNK] for s in range(N_MIX_PARTS))
    v = gc * xa
    ca = prev_a(v, 2) * _chunk(caw_ref, 0, c) + prev_a(v, 1) * _chunk(caw_ref, 1, c) + v * _chunk(caw_ref, 2, c)
    ya = gb * ca
    cb = (prev_b(xb, 3) * _chunk(cbw_ref, 0, c) + prev_b(xb, 2) * _chunk(cbw_ref, 1, c)
          + prev_b(xb, 1) * _chunk(cbw_ref, 2, c) + xb * _chunk(cbw_ref, 3, c))
    cb = cb + _chunk(cbb_ref, 0, c)
    gates = _dot(cb.astype(BF16), wax_ref[c].astype(BF16))
    return v, xb, ya, cb, gates, gg


def _lru_coeffs(gates, c, ba_ref, bx_ref, softplus_neg_lam):
    r = _sigmoid(gates[:, :MIX_CHUNK] + _chunk(ba_ref, 0, c))
    i = _sigmoid(gates[:, MIX_CHUNK:] + _chunk(bx_ref, 0, c))
    log_a = (-LRU_C * r) * softplus_neg_lam[:, c * MIX_CHUNK:(c + 1) * MIX_CHUNK]
    a = jnp.exp(log_a)
    mult = jnp.sqrt(-_expm1(2.0 * log_a))
    return a, mult, i


def _seq_copies(hbm_ref, buf_ref, sem_ref, slot, step_block, to_vmem):
    copies = []
    for b in range(SUBLANES):
        hbm = hbm_ref.at[b, pl.ds(step_block * MIX_BLOCK_STEPS, MIX_BLOCK_STEPS), :]
        vmem = buf_ref.at[slot, :, b, :]
        src, dst = (hbm, vmem) if to_vmem else (vmem, hbm)
        copies.append(pltpu.make_async_copy(src, dst, sem_ref.at[slot, b]))
    return copies


def _mix_prompt_kernel(x_hbm, gpre_ref, gpost_ref, win_ref, caw_ref, cbw_ref, cbb_ref, wax_ref, ba_ref, bx_ref,
                       lam_ref, wout_ref, o_hbm, ta_ref, tb_ref, hl_ref, hista_ref, histb_ref, hc_ref, z_ref,
                       xbuf_ref, obuf_ref, xsem_ref, osem_ref, winb_ref, waxb_ref, woutb_ref):
    t = pl.program_id(0)
    n_t = pl.num_programs(0)
    slot = t % 2
    nb = SUBLANES
    rows = MIX_STEPS * nb
    hist_rows = (CONV_B_WIDTH - 1) * nb

    @pl.when(t == 0)
    def _():
        hista_ref[...] = jnp.zeros_like(hista_ref)
        histb_ref[...] = jnp.zeros_like(histb_ref)
        hc_ref[...] = jnp.zeros_like(hc_ref)
        for cp in _seq_copies(x_hbm, xbuf_ref, xsem_ref, 0, 0, True):
            cp.start()
        winb_ref[...] = win_ref[...].astype(BF16)
        waxb_ref[...] = wax_ref[...].astype(BF16)
        woutb_ref[...] = wout_ref[...].astype(BF16)

    @pl.when(t + 1 < n_t)
    def _():
        for cp in _seq_copies(x_hbm, xbuf_ref, xsem_ref, 1 - slot, t + 1, True):
            cp.start()

    for cp in _seq_copies(x_hbm, xbuf_ref, xsem_ref, slot, t, True):
        cp.wait()

    @pl.when(t >= 2)
    def _():
        for cp in _seq_copies(o_hbm, obuf_ref, osem_ref, slot, t - 2, False):
            cp.wait()

    def prev(hist):
        return lambda cur, k: jnp.concatenate([hist[hist_rows - k * nb:, :], cur[:rows - k * nb, :]], axis=0)

    sp = _softplus(-lam_ref[...])
    row = lax.broadcasted_iota(jnp.int32, (rows, MIX_CHUNK), 0)
    first_rows = jnp.where(t == 0, nb, 0)
    hist_a = hista_ref[...]
    hist_b = histb_ref[...]
    hcur = hc_ref[...]
    n_sub = MIX_BLOCK_STEPS // MIX_STEPS
    items = [(u, c) for u in range(n_sub) for c in range(N_MIX_CHUNKS)]
    xs, normed = {}, {}

    def project(k):
        u, c = items[k]
        if u not in xs:
            xs[u] = xbuf_ref[slot, u * MIX_STEPS:(u + 1) * MIX_STEPS].reshape(rows, D_MODEL)
            normed[u] = _rms(xs[u], gpre_ref[...]).astype(BF16)
        z_ref[k % 2] = _mix_project(normed[u], winb_ref, c)

    project(0)
    for u in range(n_sub):
        y = None
        tails_a, tails_b, h_last = [], [], []
        for c in range(N_MIX_CHUNKS):
            k = u * N_MIX_CHUNKS + c
            if k + 1 < len(items):
                project(k + 1)
            x = xs[u]
            csl = slice(c * MIX_CHUNK, (c + 1) * MIX_CHUNK)
            zc = z_ref[k % 2]
            v, xb, ya, cb, gates, gg = _mix_front(zc, c, prev(hist_a[:, csl]), prev(hist_b[:, csl]),
                                                  caw_ref, cbw_ref, cbb_ref, waxb_ref)
            a, mult, i = _lru_coeffs(gates, c, ba_ref, bx_ref, sp)
            if u == 0:
                mult = jnp.where(row < first_rows, 1.0, mult)
            b = (mult * i) * cb
            hc = hcur[:, csl]
            hs = []
            for j in range(MIX_STEPS):
                sl = slice(j * nb, (j + 1) * nb)
                hc = a[sl, :] * hc + b[sl, :]
                hs.append(hc)
            yb = _gelu_tanh(gg) * jnp.concatenate(hs, axis=0)
            yc = _mix_out(ya, yb, woutb_ref, c)
            y = yc if y is None else y + yc
            tails_a.append(v[rows - hist_rows:, :])
            tails_b.append(xb[rows - hist_rows:, :])
            h_last.append(hc)
        hist_a = jnp.concatenate(tails_a, axis=-1)
        hist_b = jnp.concatenate(tails_b, axis=-1)
        hcur = jnp.concatenate(h_last, axis=-1)
        out = x + _rms(y, gpost_ref[...])
        obuf_ref[slot, u * MIX_STEPS:(u + 1) * MIX_STEPS] = out.reshape(MIX_STEPS, nb, D_MODEL)

    for cp in _seq_copies(o_hbm, obuf_ref, osem_ref, slot, t, False):
        cp.start()

    @pl.when(t == n_t - 1)
    def _():
        for cp in _seq_copies(o_hbm, obuf_ref, osem_ref, slot, t, False):
            cp.wait()

    @pl.when((t == n_t - 1) & (t >= 1))
    def _():
        for cp in _seq_copies(o_hbm, obuf_ref, osem_ref, 1 - slot, t - 1, False):
            cp.wait()

    hista_ref[...] = hist_a
    histb_ref[...] = hist_b
    hc_ref[...] = hcur
    ta_ref[...] = hist_a
    tb_ref[...] = hist_b
    hl_ref[...] = hcur


def _mix_weight_specs():
    return [_const_spec((1, D_MODEL)), _const_spec((1, D_MODEL)),
            _const_spec((D_MODEL, N_MIX_PARTS * D_CONV)),
            _const_spec((CONV_A_WIDTH, D_CONV)), _const_spec((CONV_B_WIDTH, D_LRU)), _const_spec((1, D_LRU)),
            _const_spec((N_MIX_CHUNKS, MIX_CHUNK, 2 * MIX_CHUNK)), _const_spec((1, D_LRU)), _const_spec((1, D_LRU)),
            _const_spec((1, D_LRU)), _const_spec((D_MODEL, D_MODEL))]


def _mix_prompt(x, weights, batch, seq):
    assert batch == SUBLANES and seq % MIX_BLOCK_STEPS == 0
    hist_rows = (CONV_B_WIDTH - 1) * batch
    hbm_spec = pl.BlockSpec(memory_space=pl.ANY)
    hist_shape = jax.ShapeDtypeStruct((hist_rows, D_CONV), F32)
    h_shape = jax.ShapeDtypeStruct((batch, D_LRU), F32)
    io_buf = pltpu.VMEM((2, MIX_BLOCK_STEPS, batch, D_MODEL), F32)
    return pl.pallas_call(
        _mix_prompt_kernel,
        grid=(seq // MIX_BLOCK_STEPS,),
        in_specs=[hbm_spec] + _mix_weight_specs(),
        out_specs=[hbm_spec, _const_spec(hist_shape.shape), _const_spec(hist_shape.shape), _const_spec(h_shape.shape)],
        out_shape=[jax.ShapeDtypeStruct(x.shape, F32), hist_shape, hist_shape, h_shape],
        scratch_shapes=[pltpu.VMEM(hist_shape.shape, F32), pltpu.VMEM(hist_shape.shape, F32),
                        pltpu.VMEM(h_shape.shape, F32),
                        pltpu.VMEM((2, MIX_STEPS * batch, N_MIX_PARTS * MIX_CHUNK), F32),
                        io_buf, io_buf, pltpu.SemaphoreType.DMA((2, batch)), pltpu.SemaphoreType.DMA((2, batch)),
                        pltpu.VMEM((D_MODEL, N_MIX_PARTS * D_CONV), BF16),
                        pltpu.VMEM((N_MIX_CHUNKS, MIX_CHUNK, 2 * MIX_CHUNK), BF16),
                        pltpu.VMEM((D_MODEL, D_MODEL), BF16)],
        compiler_params=_params("arbitrary"),
        name="mix_prompt",
    )(x, *weights)


def _mix_sample_kernel(x_ref, sa_ref, sb_ref, h0_ref, gpre_ref, gpost_ref, win_ref, caw_ref, cbw_ref, cbb_ref,
                       wax_ref, ba_ref, bx_ref, lam_ref, wout_ref, o_ref, na_ref, nb_ref, hl_ref):
    x = x_ref[...]
    h = _rms(x, gpre_ref[...]).astype(BF16)
    sp = _softplus(-lam_ref[...])
    y = None
    for c in range(N_MIX_CHUNKS):
        csl = slice(c * MIX_CHUNK, (c + 1) * MIX_CHUNK)
        zc = _mix_project(h, win_ref, c)

        def state_a(k):
            o = (CONV_A_WIDTH - 1 - k) * D_CONV + c * MIX_CHUNK
            return sa_ref[:, o:o + MIX_CHUNK]

        def state_b(k):
            o = (CONV_B_WIDTH - 1 - k) * D_LRU + c * MIX_CHUNK
            return sb_ref[:, o:o + MIX_CHUNK]

        v, xb, ya, cb, gates, gg = _mix_front(zc, c, lambda cur, k: state_a(k), lambda cur, k: state_b(k),
                                              caw_ref, cbw_ref, cbb_ref, wax_ref)
        a, mult, i = _lru_coeffs(gates, c, ba_ref, bx_ref, sp)
        hnew = a * h0_ref[:, csl] + (mult * i) * cb
        yb = _gelu_tanh(gg) * hnew
        yc = _mix_out(ya, yb, wout_ref, c)
        y = yc if y is None else y + yc

        na_ref[:, c * MIX_CHUNK:(c + 1) * MIX_CHUNK] = state_a(1)
        na_ref[:, D_CONV + c * MIX_CHUNK:D_CONV + (c + 1) * MIX_CHUNK] = v
        nb_ref[:, c * MIX_CHUNK:(c + 1) * MIX_CHUNK] = state_b(2)
        nb_ref[:, D_LRU + c * MIX_CHUNK:D_LRU + (c + 1) * MIX_CHUNK] = state_b(1)
        nb_ref[:, 2 * D_LRU + c * MIX_CHUNK:2 * D_LRU + (c + 1) * MIX_CHUNK] = xb
        hl_ref[:, csl] = hnew
    o_ref[...] = x + _rms(y, gpost_ref[...])


def _mix_sample(x, sa, sb, h0, weights):
    full = lambda a: _const_spec(a.shape)
    return pl.pallas_call(
        _mix_sample_kernel,
        grid=(1,),
        in_specs=[full(x), full(sa), full(sb), full(h0)] + _mix_weight_specs(),
        out_specs=[full(x), full(sa), full(sb), full(h0)],
        out_shape=[jax.ShapeDtypeStruct(x.shape, F32), jax.ShapeDtypeStruct(sa.shape, F32),
                   jax.ShapeDtypeStruct(sb.shape, F32), jax.ShapeDtypeStruct(h0.shape, F32)],
        compiler_params=_params("arbitrary"),
        name="mix_sample",
    )(x, sa, sb, h0, *weights)


def _kv_kernel(m_ref, g_ref, wk_ref, wv_ref, k_ref, v_ref):
    m = _rms(m_ref[...], g_ref[...]).astype(BF16)
    k_ref[...] = _dot(m, wk_ref[...].astype(BF16))
    v_ref[...] = _dot(m, wv_ref[...].astype(BF16))


def _memory_kv(mem, g_mem, wk, wv):
    rows = mem.shape[0]
    row_spec = pl.BlockSpec((KV_ROW_TILE, D_MODEL), lambda i: (i, 0))
    out = jax.ShapeDtypeStruct((rows, D_MODEL), F32)
    return pl.pallas_call(
        _kv_kernel,
        grid=(rows // KV_ROW_TILE,),
        in_specs=[row_spec, _const_spec((1, D_MODEL)), _const_spec((D_MODEL, D_MODEL)),
                  _const_spec((D_MODEL, D_MODEL))],
        out_specs=[row_spec, row_spec],
        out_shape=[out, out],
        compiler_params=_params("parallel"),
        name="memory_kv",
    )(mem, g_mem, wk, wv)


def _softmax_rows(s):
    e = jnp.exp(s - jnp.max(s, axis=-1, keepdims=True))
    return e * (1.0 / jnp.sum(e, axis=-1, keepdims=True))


def _xattn_prompt_kernel(x_ref, k_ref, v_ref, gpre_ref, gpost_ref, wq_ref, wo_ref, o_ref):
    x = x_ref[...]
    h = _rms(x, gpre_ref[...]).astype(BF16)
    q = _dot(h, wq_ref[...].astype(BF16)).astype(BF16)
    heads = []
    for hd in range(N_XHEADS):
        sl = slice(hd * XHEAD_DIM, (hd + 1) * XHEAD_DIM)
        kh = k_ref[0, :, sl].astype(BF16)
        vh = v_ref[0, :, sl].astype(BF16)
        s = lax.dot_general(q[:, sl], kh, (((1,), (1,)), ((), ())), preferred_element_type=F32)
        p = _softmax_rows(s * (XHEAD_DIM ** -0.5)).astype(BF16)
        heads.append(_dot(p, vh).astype(BF16))
    y = _dot(jnp.concatenate(heads, axis=-1), wo_ref[...].astype(BF16))
    o_ref[...] = x + _rms(y, gpost_ref[...])


def _xattn_prompt(x, mem_k, mem_v, g_pre, g_post, wq, wo, batch, seq):
    n_t = seq // XATTN_ROW_TILE
    row_spec = pl.BlockSpec((XATTN_ROW_TILE, D_MODEL), lambda b, t: (b * n_t + t, 0))
    kv_spec = pl.BlockSpec((1, N_MEM, D_MODEL), lambda b, t: (b, 0, 0))
    return pl.pallas_call(
        _xattn_prompt_kernel,
        grid=(batch, n_t),
        in_specs=[row_spec, kv_spec, kv_spec, _const_spec((1, D_MODEL)), _const_spec((1, D_MODEL)),
                  _const_spec((D_MODEL, D_MODEL)), _const_spec((D_MODEL, D_MODEL))],
        out_specs=row_spec,
        out_shape=jax.ShapeDtypeStruct(x.shape, F32),
        compiler_params=_params("parallel", "arbitrary"),
        name="xattn_prompt",
    )(x, mem_k, mem_v, g_pre, g_post, wq, wo)


def _q_sample_kernel(x_ref, gpre_ref, wq_ref, q_ref):
    q_ref[...] = _dot(_rms(x_ref[...], gpre_ref[...]).astype(BF16), wq_ref[...].astype(BF16))


def _split_heads(a):
    n = a.shape[0]
    a = a.reshape(n, -1, N_XHEADS, 2, LANES)
    return jnp.swapaxes(a, 2, 3).reshape(n, -1, LANES)


def _attend_sample_kernel(q_ref, k_ref, v_ref, ones_ref, o_ref):
    rows = N_MEM * HEAD_ROWS
    for b in range(q_ref.shape[0]):
        qv = q_ref[b] * (XHEAD_DIM ** -0.5)
        prod = k_ref[b].reshape(N_MEM, HEAD_ROWS, LANES) * qv[None]
        part = _dot(prod.reshape(rows, LANES).astype(BF16), ones_ref[...]).reshape(N_MEM, HEAD_ROWS, LANES)
        s = part + pltpu.roll(part, N_XHEADS, 1)
        e = jnp.exp(s - jnp.max(s, axis=0, keepdims=True))
        den = jnp.sum(e, axis=0)
        num = jnp.sum(e * v_ref[b].reshape(N_MEM, HEAD_ROWS, LANES), axis=0)
        o_ref[b] = num * (1.0 / den)


def _out_sample_kernel(x_ref, o_ref, gpost_ref, wo_ref, y_ref):
    y = _dot(o_ref[...].astype(BF16), wo_ref[...].astype(BF16))
    y_ref[...] = x_ref[...] + _rms(y, gpost_ref[...])


def _xattn_sample(x, cache_k, cache_v, g_pre, g_post, wq, wo):
    n = x.shape[0]
    full = lambda shape: _const_spec(shape)
    q = pl.pallas_call(
        _q_sample_kernel,
        grid=(1,),
        in_specs=[full(x.shape), full((1, D_MODEL)), full((D_MODEL, D_MODEL))],
        out_specs=full(x.shape),
        out_shape=jax.ShapeDtypeStruct(x.shape, F32),
        compiler_params=_params("arbitrary"),
        name="q_sample",
    )(x, g_pre, wq)
    nb = SAMPLE_XATTN_BLOCK
    rows = N_MEM * HEAD_ROWS
    q_spec = pl.BlockSpec((nb, HEAD_ROWS, LANES), lambda i: (i, 0, 0))
    kv_spec = pl.BlockSpec((nb, rows, LANES), lambda i: (i, 0, 0))
    o = pl.pallas_call(
        _attend_sample_kernel,
        grid=(n // nb,),
        in_specs=[q_spec, kv_spec, kv_spec, _const_spec((LANES, LANES))],
        out_specs=q_spec,
        out_shape=jax.ShapeDtypeStruct((n, HEAD_ROWS, LANES), F32),
        compiler_params=_params("parallel"),
        name="attend_sample",
    )(_split_heads(q), cache_k, cache_v, jnp.ones((LANES, LANES), BF16))
    o = o.reshape(n, 2, N_XHEADS, LANES).transpose(0, 2, 1, 3).reshape(n, D_MODEL)
    return pl.pallas_call(
        _out_sample_kernel,
        grid=(1,),
        in_specs=[full(x.shape), full(x.shape), full((1, D_MODEL)), full((D_MODEL, D_MODEL))],
        out_specs=full(x.shape),
        out_shape=jax.ShapeDtypeStruct(x.shape, F32),
        compiler_params=_params("arbitrary"),
        name="out_sample",
    )(x, o, g_post, wo)


def _block_diag(w):
    groups, heads, d, _ = w.shape
    eye = jnp.eye(heads, dtype=w.dtype)
    return (eye[None, :, None, :, None] * w[:, :, :, None, :]).reshape(groups, heads * d, heads * d)


def _mix_weights(g_pre, g_post, w_in, conv_a_w, conv_b_w, conv_b_b, lru_wa, lru_ba, lru_wx, lru_bx, lru_lam, w_out):
    row = lambda p: p.reshape(1, -1)
    heads_per_chunk = MIX_CHUNK // LRU_HEAD_DIM
    grouped = lambda w: _block_diag(w.reshape(N_MIX_CHUNKS, heads_per_chunk, LRU_HEAD_DIM, LRU_HEAD_DIM))
    wax = jnp.concatenate([grouped(lru_wa), grouped(lru_wx)], axis=-1)
    return (row(g_pre), row(g_post), w_in, conv_a_w, conv_b_w, row(conv_b_b), wax, row(lru_ba), row(lru_bx),
            row(lru_lam), w_out)


def kernel(x_prompt, x_sample, mem_prompt, cache_mem_k, cache_mem_v, state_conv_a, state_conv_b, state_lru, g_ffn1_pre, g_ffn1_post, ffn1_wg, ffn1_wu, ffn1_wd, g_mix_pre, g_mix_post, w_in, conv_a_w, conv_b_w, conv_b_b, lru_wa, lru_ba, lru_wx, lru_bx, lru_lam, w_out, g_xattn_pre, g_xattn_post, g_mem, xattn_wq, xattn_wk, xattn_wv, xattn_wo, g_ffn2_pre, g_ffn2_post, ffn2_wg, ffn2_wu, ffn2_wd):
    batch, seq, _ = x_prompt.shape
    n_s = x_sample.shape[0]
    depth = g_ffn1_pre.shape[0]
    assert depth == 1 and x_sample.shape[1] == 1
    l = 0
    row = lambda p: p[l].reshape(1, -1)

    yp = x_prompt.reshape(batch * seq, D_MODEL)
    ys = x_sample.reshape(n_s, D_MODEL)

    ffn1 = (row(g_ffn1_pre), row(g_ffn1_post), ffn1_wg[l], ffn1_wu[l], ffn1_wd[l])
    ffn2 = (row(g_ffn2_pre), row(g_ffn2_post), ffn2_wg[l], ffn2_wu[l], ffn2_wd[l])
    mix_w = _mix_weights(g_mix_pre[l], g_mix_post[l], w_in[l], conv_a_w[l], conv_b_w[l], conv_b_b[l], lru_wa[l],
                         lru_ba[l], lru_wx[l], lru_bx[l], lru_lam[l], w_out[l])

    mk, mv = _memory_kv(mem_prompt.reshape(batch * N_MEM, D_MODEL), row(g_mem), xattn_wk[l], xattn_wv[l])

    yp = _ffn(yp, *ffn1, row_tile=FFN_ROW_TILE)
    yp, tail_a, tail_b, tail_h = _mix_prompt(yp.reshape(batch, seq, D_MODEL), mix_w, batch, seq)
    yp = yp.reshape(batch * seq, D_MODEL)
    tail_a = tail_a.reshape(CONV_B_WIDTH - 1, batch, D_CONV)[CONV_B_WIDTH - CONV_A_WIDTH:].transpose(1, 0, 2)
    tail_b = tail_b.reshape(CONV_B_WIDTH - 1, batch, D_LRU).transpose(1, 0, 2)
    yp = _xattn_prompt(yp, mk.reshape(batch, N_MEM, D_MODEL), mv.reshape(batch, N_MEM, D_MODEL),
                       row(g_xattn_pre), row(g_xattn_post), xattn_wq[l], xattn_wo[l], batch, seq)
    yp = _ffn(yp, *ffn2, row_tile=FFN_ROW_TILE)

    ys = _ffn(ys, *ffn1, row_tile=n_s)
    ys, new_a, new_b, new_h = _mix_sample(ys, state_conv_a[l].reshape(n_s, 2 * D_CONV),
                                          state_conv_b[l].reshape(n_s, 3 * D_LRU), state_lru[l], mix_w)
    ys = _xattn_sample(ys, _split_heads(cache_mem_k[l]), _split_heads(cache_mem_v[l]),
                       row(g_xattn_pre), row(g_xattn_post), xattn_wq[l], xattn_wo[l])
    ys = _ffn(ys, *ffn2, row_tile=n_s)

    kv_shape = (1, batch, N_MEM, N_XHEADS, XHEAD_DIM)
    return (yp.reshape(batch, seq, D_MODEL), ys.reshape(n_s, 1, D_MODEL),
            mk.reshape(kv_shape), mv.reshape(kv_shape),
            tail_a[None], tail_b[None], tail_h[None],
            new_a.reshape(1, n_s, 2, D_CONV), new_b.reshape(1, n_s, 3, D_LRU), new_h[None])
```

```python
import math

import jax
import jax.numpy as jnp
from jax import lax
from jax.experimental import pallas as pl
from jax.experimental.pallas import tpu as pltpu

D_MODEL = 1024
D_CONV = 512
D_LRU = 512
N_LRU_HEADS = 8
LRU_HEAD_DIM = D_LRU // N_LRU_HEADS
LRU_C = 8.0
CONV_A_WIDTH = 3
CONV_B_WIDTH = 4
N_MIX_PARTS = 5
FFN_DIM = 2816
N_MEM = 256
N_XHEADS = 4
XHEAD_DIM = D_MODEL // N_XHEADS
RMS_EPS = 1e-6

F32 = jnp.float32
BF16 = jnp.bfloat16

V7X_VMEM_LIMIT_BYTES = 56 * 1024 * 1024
SUBLANES = 8
LANES = 128
HEAD_ROWS = N_XHEADS * XHEAD_DIM // LANES

FFN_ROW_TILE = 512
FFN_COL_CHUNK = 256
MIX_STEPS = 32
MIX_BLOCK_STEPS = 128
MIX_CHUNK = 256
N_MIX_CHUNKS = D_CONV // MIX_CHUNK
XATTN_ROW_TILE = 1024
XATTN_SUB_ROWS = 512
KV_ROW_TILE = 512
SAMPLE_XATTN_BLOCK = 8


def _rms(x, g):
    y = x * lax.rsqrt(jnp.mean(x * x, axis=-1, keepdims=True) + RMS_EPS)
    return y * g


def _dot(a, b):
    return jnp.dot(a, b, preferred_element_type=F32)


def _sigmoid(x):
    return 1.0 / (1.0 + jnp.exp(-x))


def _gelu_tanh(x):
    c = math.sqrt(2.0 / math.pi)
    return x * (0.5 * (1.0 + jnp.tanh(c * (x + 0.044715 * (x * x * x)))))


def _expm1(y):
    u = jnp.exp(y)
    stable = (u - 1.0) * y / jnp.log(u)
    return jnp.where(u == 1.0, y, jnp.where(y < -1.0, u - 1.0, stable))


def _log1p(w):
    u = 1.0 + w
    return jnp.where(u == 1.0, w, jnp.log(u) * w / (u - 1.0))


def _softplus(x):
    return jnp.maximum(x, 0.0) + _log1p(jnp.exp(-jnp.abs(x)))


def _const_spec(shape):
    zeros = (0,) * len(shape)
    return pl.BlockSpec(shape, lambda *_: zeros, pipeline_mode=pl.Buffered(1))


def _params(*sem):
    return pltpu.CompilerParams(dimension_semantics=sem, vmem_limit_bytes=V7X_VMEM_LIMIT_BYTES)


def _ffn_kernel(x_ref, gpre_ref, gpost_ref, wg_ref, wu_ref, wd_ref, o_ref, act_ref):
    x = x_ref[...]
    h = _rms(x, gpre_ref[...]).astype(BF16)
    for c in range(FFN_DIM // FFN_COL_CHUNK):
        sl = slice(c * FFN_COL_CHUNK, (c + 1) * FFN_COL_CHUNK)
        g = _dot(h, wg_ref[:, sl].astype(BF16))
        u = _dot(h, wu_ref[:, sl].astype(BF16))
        act_ref[:, sl] = ((g * _sigmoid(g)) * u).astype(BF16)
    y = _dot(act_ref[...], wd_ref[...].astype(BF16))
    o_ref[...] = x + 0.5 * _rms(y, gpost_ref[...])


def _ffn(x, g_pre, g_post, wg, wu, wd, row_tile):
    rows = x.shape[0]
    row_spec = pl.BlockSpec((row_tile, D_MODEL), lambda i: (i, 0))
    return pl.pallas_call(
        _ffn_kernel,
        grid=(rows // row_tile,),
        in_specs=[row_spec, _const_spec((1, D_MODEL)), _const_spec((1, D_MODEL)),
                  _const_spec((D_MODEL, FFN_DIM)), _const_spec((D_MODEL, FFN_DIM)),
                  _const_spec((FFN_DIM, D_MODEL))],
        out_specs=row_spec,
        out_shape=jax.ShapeDtypeStruct((rows, D_MODEL), F32),
        scratch_shapes=[pltpu.VMEM((row_tile, FFN_DIM), BF16)],
        compiler_params=_params("parallel"),
        name="ffn",
    )(x, g_pre, g_post, wg, wu, wd)


def _chunk(ref, r, c):
    return ref[r:r + 1, c * MIX_CHUNK:(c + 1) * MIX_CHUNK]


def _mix_project(h, win_ref, c):
    parts = []
    for s in range(N_MIX_PARTS):
        o = s * D_CONV + c * MIX_CHUNK
        parts.append(_dot(h, win_ref[:, o:o + MIX_CHUNK].astype(BF16)))
    return jnp.concatenate(parts, axis=-1)


def _mix_out(ya, yb, wout_ref, c):
    ra = c * MIX_CHUNK
    rb = D_CONV + c * MIX_CHUNK
    return (_dot(ya.astype(BF16), wout_ref[ra:ra + MIX_CHUNK, :].astype(BF16))
            + _dot(yb.astype(BF16), wout_ref[rb:rb + MIX_CHUNK, :].astype(BF16)))


def _mix_front(zc, c, prev_a, prev_b, caw_ref, cbw_ref, cbb_ref, wax_ref):
    gb, gc, xa, xb, gg = (zc[:, s * MIX_CHUNK:(s + 1) * MIX_CHUNK] for s in range(N_MIX_PARTS))
    v = gc * xa
    ca = prev_a(v, 2) * _chunk(caw_ref, 0, c) + prev_a(v, 1) * _chunk(caw_ref, 1, c) + v * _chunk(caw_ref, 2, c)
    ya = gb * ca
    cb = (prev_b(xb, 3) * _chunk(cbw_ref, 0, c) + prev_b(xb, 2) * _chunk(cbw_ref, 1, c)
          + prev_b(xb, 1) * _chunk(cbw_ref, 2, c) + xb * _chunk(cbw_ref, 3, c))
    cb = cb + _chunk(cbb_ref, 0, c)
    gates = _dot(cb.astype(BF16), wax_ref[c].astype(BF16))
    return v, xb, ya, cb, gates, gg


def _lru_coeffs(gates, c, ba_ref, bx_ref, softplus_neg_lam):
    r = _sigmoid(gates[:, :MIX_CHUNK] + _chunk(ba_ref, 0, c))
    i = _sigmoid(gates[:, MIX_CHUNK:] + _chunk(bx_ref, 0, c))
    log_a = (-LRU_C * r) * softplus_neg_lam[:, c * MIX_CHUNK:(c + 1) * MIX_CHUNK]
    a = jnp.exp(log_a)
    mult = jnp.sqrt(-_expm1(2.0 * log_a))
    return a, mult, i


def _seq_copies(hbm_ref, buf_ref, sem_ref, slot, step_block, to_vmem):
    copies = []
    for b in range(SUBLANES):
        hbm = hbm_ref.at[b, pl.ds(step_block * MIX_BLOCK_STEPS, MIX_BLOCK_STEPS), :]
        vmem = buf_ref.at[slot, :, b, :]
        src, dst = (hbm, vmem) if to_vmem else (vmem, hbm)
        copies.append(pltpu.make_async_copy(src, dst, sem_ref.at[slot, b]))
    return copies


def _mix_prompt_kernel(x_hbm, gpre_ref, gpost_ref, win_ref, caw_ref, cbw_ref, cbb_ref, wax_ref, ba_ref, bx_ref,
                       lam_ref, wout_ref, o_hbm, ta_ref, tb_ref, hl_ref, hista_ref, histb_ref, hc_ref, z_ref,
                       xbuf_ref, obuf_ref, xsem_ref, osem_ref, winb_ref, waxb_ref, woutb_ref):
    t = pl.program_id(0)
    n_t = pl.num_programs(0)
    slot = t % 2
    nb = SUBLANES
    rows = MIX_STEPS * nb
    hist_rows = (CONV_B_WIDTH - 1) * nb

    @pl.when(t == 0)
    def _():
        hista_ref[...] = jnp.zeros_like(hista_ref)
        histb_ref[...] = jnp.zeros_like(histb_ref)
        hc_ref[...] = jnp.zeros_like(hc_ref)
        for cp in _seq_copies(x_hbm, xbuf_ref, xsem_ref, 0, 0, True):
            cp.start()
        winb_ref[...] = win_ref[...].astype(BF16)
        waxb_ref[...] = wax_ref[...].astype(BF16)
        woutb_ref[...] = wout_ref[...].astype(BF16)

    @pl.when(t + 1 < n_t)
    def _():
        for cp in _seq_copies(x_hbm, xbuf_ref, xsem_ref, 1 - slot, t + 1, True):
            cp.start()

    for cp in _seq_copies(x_hbm, xbuf_ref, xsem_ref, slot, t, True):
        cp.wait()

    @pl.when(t >= 2)
    def _():
        for cp in _seq_copies(o_hbm, obuf_ref, osem_ref, slot, t - 2, False):
            cp.wait()

    def prev(hist):
        return lambda cur, k: jnp.concatenate([hist[hist_rows - k * nb:, :], cur[:rows - k * nb, :]], axis=0)

    sp = _softplus(-lam_ref[...])
    row = lax.broadcasted_iota(jnp.int32, (rows, MIX_CHUNK), 0)
    first_rows = jnp.where(t == 0, nb, 0)
    hist_a = hista_ref[...]
    hist_b = histb_ref[...]
    hcur = hc_ref[...]
    n_sub = MIX_BLOCK_STEPS // MIX_STEPS
    items = [(u, c) for u in range(n_sub) for c in range(N_MIX_CHUNKS)]
    xs, normed = {}, {}

    def project(k):
        u, c = items[k]
        if u not in xs:
            xs[u] = xbuf_ref[slot, u * MIX_STEPS:(u + 1) * MIX_STEPS].reshape(rows, D_MODEL)
            normed[u] = _rms(xs[u], gpre_ref[...]).astype(BF16)
        z_ref[k % 2] = _mix_project(normed[u], winb_ref, c)

    project(0)
    for u in range(n_sub):
        y = None
        tails_a, tails_b, h_last = [], [], []
        for c in range(N_MIX_CHUNKS):
            k = u * N_MIX_CHUNKS + c
            if k + 1 < len(items):
                project(k + 1)
            x = xs[u]
            csl = slice(c * MIX_CHUNK, (c + 1) * MIX_CHUNK)
            zc = z_ref[k % 2]
            v, xb, ya, cb, gates, gg = _mix_front(zc, c, prev(hist_a[:, csl]), prev(hist_b[:, csl]),
                                                  caw_ref, cbw_ref, cbb_ref, waxb_ref)
            a, mult, i = _lru_coeffs(gates, c, ba_ref, bx_ref, sp)
            if u == 0:
                mult = jnp.where(row < first_rows, 1.0, mult)
            b = (mult * i) * cb
            hc = hcur[:, csl]
            hs = []
            for j in range(MIX_STEPS):
                sl = slice(j * nb, (j + 1) * nb)
                hc = a[sl, :] * hc + b[sl, :]
                hs.append(hc)
            yb = _gelu_tanh(gg) * jnp.concatenate(hs, axis=0)
            yc = _mix_out(ya, yb, woutb_ref, c)
            y = yc if y is None else y + yc
            tails_a.append(v[rows - hist_rows:, :])
            tails_b.append(xb[rows - hist_rows:, :])
            h_last.append(hc)
        hist_a = jnp.concatenate(tails_a, axis=-1)
        hist_b = jnp.concatenate(tails_b, axis=-1)
        hcur = jnp.concatenate(h_last, axis=-1)
        out = x + _rms(y, gpost_ref[...])
        obuf_ref[slot, u * MIX_STEPS:(u + 1) * MIX_STEPS] = out.reshape(MIX_STEPS, nb, D_MODEL)

    for cp in _seq_copies(o_hbm, obuf_ref, osem_ref, slot, t, False):
        cp.start()

    @pl.when(t == n_t - 1)
    def _():
        for cp in _seq_copies(o_hbm, obuf_ref, osem_ref, slot, t, False):
            cp.wait()

    @pl.when((t == n_t - 1) & (t >= 1))
    def _():
        for cp in _seq_copies(o_hbm, obuf_ref, osem_ref, 1 - slot, t - 1, False):
            cp.wait()

    hista_ref[...] = hist_a
    histb_ref[...] = hist_b
    hc_ref[...] = hcur
    ta_ref[...] = hist_a
    tb_ref[...] = hist_b
    hl_ref[...] = hcur


def _mix_weight_specs():
    return [_const_spec((1, D_MODEL)), _const_spec((1, D_MODEL)),
            _const_spec((D_MODEL, N_MIX_PARTS * D_CONV)),
            _const_spec((CONV_A_WIDTH, D_CONV)), _const_spec((CONV_B_WIDTH, D_LRU)), _const_spec((1, D_LRU)),
            _const_spec((N_MIX_CHUNKS, MIX_CHUNK, 2 * MIX_CHUNK)), _const_spec((1, D_LRU)), _const_spec((1, D_LRU)),
            _const_spec((1, D_LRU)), _const_spec((D_MODEL, D_MODEL))]


def _mix_prompt(x, weights, batch, seq):
    assert batch == SUBLANES and seq % MIX_BLOCK_STEPS == 0
    hist_rows = (CONV_B_WIDTH - 1) * batch
    hbm_spec = pl.BlockSpec(memory_space=pl.ANY)
    hist_shape = jax.ShapeDtypeStruct((hist_rows, D_CONV), F32)
    h_shape = jax.ShapeDtypeStruct((batch, D_LRU), F32)
    io_buf = pltpu.VMEM((2, MIX_BLOCK_STEPS, batch, D_MODEL), F32)
    return pl.pallas_call(
        _mix_prompt_kernel,
        grid=(seq // MIX_BLOCK_STEPS,),
        in_specs=[hbm_spec] + _mix_weight_specs(),
        out_specs=[hbm_spec, _const_spec(hist_shape.shape), _const_spec(hist_shape.shape), _const_spec(h_shape.shape)],
        out_shape=[jax.ShapeDtypeStruct(x.shape, F32), hist_shape, hist_shape, h_shape],
        scratch_shapes=[pltpu.VMEM(hist_shape.shape, F32), pltpu.VMEM(hist_shape.shape, F32),
                        pltpu.VMEM(h_shape.shape, F32),
                        pltpu.VMEM((2, MIX_STEPS * batch, N_MIX_PARTS * MIX_CHUNK), F32),
                        io_buf, io_buf, pltpu.SemaphoreType.DMA((2, batch)), pltpu.SemaphoreType.DMA((2, batch)),
                        pltpu.VMEM((D_MODEL, N_MIX_PARTS * D_CONV), BF16),
                        pltpu.VMEM((N_MIX_CHUNKS, MIX_CHUNK, 2 * MIX_CHUNK), BF16),
                        pltpu.VMEM((D_MODEL, D_MODEL), BF16)],
        compiler_params=_params("arbitrary"),
        name="mix_prompt",
    )(x, *weights)


def _mix_sample_kernel(x_ref, sa_ref, sb_ref, h0_ref, gpre_ref, gpost_ref, win_ref, caw_ref, cbw_ref, cbb_ref,
                       wax_ref, ba_ref, bx_ref, lam_ref, wout_ref, o_ref, na_ref, nb_ref, hl_ref):
    x = x_ref[...]
    h = _rms(x, gpre_ref[...]).astype(BF16)
    sp = _softplus(-lam_ref[...])
    y = None
    for c in range(N_MIX_CHUNKS):
        csl = slice(c * MIX_CHUNK, (c + 1) * MIX_CHUNK)
        zc = _mix_project(h, win_ref, c)

        def state_a(k):
            o = (CONV_A_WIDTH - 1 - k) * D_CONV + c * MIX_CHUNK
            return sa_ref[:, o:o + MIX_CHUNK]

        def state_b(k):
            o = (CONV_B_WIDTH - 1 - k) * D_LRU + c * MIX_CHUNK
            return sb_ref[:, o:o + MIX_CHUNK]

        v, xb, ya, cb, gates, gg = _mix_front(zc, c, lambda cur, k: state_a(k), lambda cur, k: state_b(k),
                                              caw_ref, cbw_ref, cbb_ref, wax_ref)
        a, mult, i = _lru_coeffs(gates, c, ba_ref, bx_ref, sp)
        hnew = a * h0_ref[:, csl] + (mult * i) * cb
        yb = _gelu_tanh(gg) * hnew
        yc = _mix_out(ya, yb, wout_ref, c)
        y = yc if y is None else y + yc

        na_ref[:, c * MIX_CHUNK:(c + 1) * MIX_CHUNK] = state_a(1)
        na_ref[:, D_CONV + c * MIX_CHUNK:D_CONV + (c + 1) * MIX_CHUNK] = v
        nb_ref[:, c * MIX_CHUNK:(c + 1) * MIX_CHUNK] = state_b(2)
        nb_ref[:, D_LRU + c * MIX_CHUNK:D_LRU + (c + 1) * MIX_CHUNK] = state_b(1)
        nb_ref[:, 2 * D_LRU + c * MIX_CHUNK:2 * D_LRU + (c + 1) * MIX_CHUNK] = xb
        hl_ref[:, csl] = hnew
    o_ref[...] = x + _rms(y, gpost_ref[...])


def _mix_sample(x, sa, sb, h0, weights):
    full = lambda a: _const_spec(a.shape)
    return pl.pallas_call(
        _mix_sample_kernel,
        grid=(1,),
        in_specs=[full(x), full(sa), full(sb), full(h0)] + _mix_weight_specs(),
        out_specs=[full(x), full(sa), full(sb), full(h0)],
        out_shape=[jax.ShapeDtypeStruct(x.shape, F32), jax.ShapeDtypeStruct(sa.shape, F32),
                   jax.ShapeDtypeStruct(sb.shape, F32), jax.ShapeDtypeStruct(h0.shape, F32)],
        compiler_params=_params("arbitrary"),
        name="mix_sample",
    )(x, sa, sb, h0, *weights)


def _kv_kernel(m_ref, g_ref, wk_ref, wv_ref, k_ref, v_ref):
    m = _rms(m_ref[...], g_ref[...]).astype(BF16)
    k_ref[...] = _dot(m, wk_ref[...].astype(BF16))
    v_ref[...] = _dot(m, wv_ref[...].astype(BF16))


def _memory_kv(mem, g_mem, wk, wv):
    rows = mem.shape[0]
    row_spec = pl.BlockSpec((KV_ROW_TILE, D_MODEL), lambda i: (i, 0))
    out = jax.ShapeDtypeStruct((rows, D_MODEL), F32)
    return pl.pallas_call(
        _kv_kernel,
        grid=(rows // KV_ROW_TILE,),
        in_specs=[row_spec, _const_spec((1, D_MODEL)), _const_spec((D_MODEL, D_MODEL)),
                  _const_spec((D_MODEL, D_MODEL))],
        out_specs=[row_spec, row_spec],
        out_shape=[out, out],
        compiler_params=_params("parallel"),
        name="memory_kv",
    )(mem, g_mem, wk, wv)


def _softmax_rows(s):
    e = jnp.exp(s - jnp.max(s, axis=-1, keepdims=True))
    return e * (1.0 / jnp.sum(e, axis=-1, keepdims=True))


def _xattn_prompt_kernel(x_ref, k_ref, v_ref, gpre_ref, gpost_ref, wq_ref, wo_ref, o_ref):
    wq = wq_ref[...].astype(BF16)
    wo = wo_ref[...].astype(BF16)
    n_sub = x_ref.shape[0] // XATTN_SUB_ROWS
    subs = [slice(u * XATTN_SUB_ROWS, (u + 1) * XATTN_SUB_ROWS) for u in range(n_sub)]
    xs = [x_ref[r, :] for r in subs]
    qs = [_dot(_rms(x, gpre_ref[...]).astype(BF16), wq).astype(BF16) for x in xs]
    attn = []
    for q in qs:
        heads = []
        for hd in range(N_XHEADS):
            sl = slice(hd * XHEAD_DIM, (hd + 1) * XHEAD_DIM)
            kh = k_ref[0, :, sl].astype(BF16)
            vh = v_ref[0, :, sl].astype(BF16)
            s = lax.dot_general(q[:, sl], kh, (((1,), (1,)), ((), ())), preferred_element_type=F32)
            p = _softmax_rows(s * (XHEAD_DIM ** -0.5)).astype(BF16)
            heads.append(_dot(p, vh).astype(BF16))
        attn.append(jnp.concatenate(heads, axis=-1))
    for r, x, a in zip(subs, xs, attn):
        o_ref[r, :] = x + _rms(_dot(a, wo), gpost_ref[...])


def _xattn_prompt(x, mem_k, mem_v, g_pre, g_post, wq, wo, batch, seq):
    n_t = seq // XATTN_ROW_TILE
    row_spec = pl.BlockSpec((XATTN_ROW_TILE, D_MODEL), lambda b, t: (b * n_t + t, 0))
    kv_spec = pl.BlockSpec((1, N_MEM, D_MODEL), lambda b, t: (b, 0, 0))
    return pl.pallas_call(
        _xattn_prompt_kernel,
        grid=(batch, n_t),
        in_specs=[row_spec, kv_spec, kv_spec, _const_spec((1, D_MODEL)), _const_spec((1, D_MODEL)),
                  _const_spec((D_MODEL, D_MODEL)), _const_spec((D_MODEL, D_MODEL))],
        out_specs=row_spec,
        out_shape=jax.ShapeDtypeStruct(x.shape, F32),
        compiler_params=_params("parallel", "arbitrary"),
        name="xattn_prompt",
    )(x, mem_k, mem_v, g_pre, g_post, wq, wo)


def _q_sample_kernel(x_ref, gpre_ref, wq_ref, q_ref):
    q_ref[...] = _dot(_rms(x_ref[...], gpre_ref[...]).astype(BF16), wq_ref[...].astype(BF16))


def _split_heads(a):
    n = a.shape[0]
    a = a.reshape(n, -1, N_XHEADS, 2, LANES)
    return jnp.swapaxes(a, 2, 3).reshape(n, -1, LANES)


def _attend_sample_kernel(q_ref, k_ref, v_ref, ones_ref, o_ref):
    rows = N_MEM * HEAD_ROWS
    for b in range(q_ref.shape[0]):
        qv = q_ref[b] * (XHEAD_DIM ** -0.5)
        prod = k_ref[b].reshape(N_MEM, HEAD_ROWS, LANES) * qv[None]
        part = _dot(prod.reshape(rows, LANES).astype(BF16), ones_ref[...]).reshape(N_MEM, HEAD_ROWS, LANES)
        s = part + pltpu.roll(part, N_XHEADS, 1)
        e = jnp.exp(s - jnp.max(s, axis=0, keepdims=True))
        den = jnp.sum(e, axis=0)
        num = jnp.sum(e * v_ref[b].reshape(N_MEM, HEAD_ROWS, LANES), axis=0)
        o_ref[b] = num * (1.0 / den)


def _out_sample_kernel(x_ref, o_ref, gpost_ref, wo_ref, y_ref):
    y = _dot(o_ref[...].astype(BF16), wo_ref[...].astype(BF16))
    y_ref[...] = x_ref[...] + _rms(y, gpost_ref[...])


def _xattn_sample(x, cache_k, cache_v, g_pre, g_post, wq, wo):
    n = x.shape[0]
    full = lambda shape: _const_spec(shape)
    q = pl.pallas_call(
        _q_sample_kernel,
        grid=(1,),
        in_specs=[full(x.shape), full((1, D_MODEL)), full((D_MODEL, D_MODEL))],
        out_specs=full(x.shape),
        out_shape=jax.ShapeDtypeStruct(x.shape, F32),
        compiler_params=_params("arbitrary"),
        name="q_sample",
    )(x, g_pre, wq)
    nb = SAMPLE_XATTN_BLOCK
    rows = N_MEM * HEAD_ROWS
    q_spec = pl.BlockSpec((nb, HEAD_ROWS, LANES), lambda i: (i, 0, 0))
    kv_spec = pl.BlockSpec((nb, rows, LANES), lambda i: (i, 0, 0))
    o = pl.pallas_call(
        _attend_sample_kernel,
        grid=(n // nb,),
        in_specs=[q_spec, kv_spec, kv_spec, _const_spec((LANES, LANES))],
        out_specs=q_spec,
        out_shape=jax.ShapeDtypeStruct((n, HEAD_ROWS, LANES), F32),
        compiler_params=_params("parallel"),
        name="attend_sample",
    )(_split_heads(q), cache_k, cache_v, jnp.ones((LANES, LANES), BF16))
    o = o.reshape(n, 2, N_XHEADS, LANES).transpose(0, 2, 1, 3).reshape(n, D_MODEL)
    return pl.pallas_call(
        _out_sample_kernel,
        grid=(1,),
        in_specs=[full(x.shape), full(x.shape), full((1, D_MODEL)), full((D_MODEL, D_MODEL))],
        out_specs=full(x.shape),
        out_shape=jax.ShapeDtypeStruct(x.shape, F32),
        compiler_params=_params("arbitrary"),
        name="out_sample",
    )(x, o, g_post, wo)


def _block_diag(w):
    groups, heads, d, _ = w.shape
    eye = jnp.eye(heads, dtype=w.dtype)
    return (eye[None, :, None, :, None] * w[:, :, :, None, :]).reshape(groups, heads * d, heads * d)


def _mix_weights(g_pre, g_post, w_in, conv_a_w, conv_b_w, conv_b_b, lru_wa, lru_ba, lru_wx, lru_bx, lru_lam, w_out):
    row = lambda p: p.reshape(1, -1)
    heads_per_chunk = MIX_CHUNK // LRU_HEAD_DIM
    grouped = lambda w: _block_diag(w.reshape(N_MIX_CHUNKS, heads_per_chunk, LRU_HEAD_DIM, LRU_HEAD_DIM))
    wax = jnp.concatenate([grouped(lru_wa), grouped(lru_wx)], axis=-1)
    return (row(g_pre), row(g_post), w_in, conv_a_w, conv_b_w, row(conv_b_b), wax, row(lru_ba), row(lru_bx),
            row(lru_lam), w_out)


def kernel(x_prompt, x_sample, mem_prompt, cache_mem_k, cache_mem_v, state_conv_a, state_conv_b, state_lru, g_ffn1_pre, g_ffn1_post, ffn1_wg, ffn1_wu, ffn1_wd, g_mix_pre, g_mix_post, w_in, conv_a_w, conv_b_w, conv_b_b, lru_wa, lru_ba, lru_wx, lru_bx, lru_lam, w_out, g_xattn_pre, g_xattn_post, g_mem, xattn_wq, xattn_wk, xattn_wv, xattn_wo, g_ffn2_pre, g_ffn2_post, ffn2_wg, ffn2_wu, ffn2_wd):
    batch, seq, _ = x_prompt.shape
    n_s = x_sample.shape[0]
    depth = g_ffn1_pre.shape[0]
    assert depth == 1 and x_sample.shape[1] == 1
    l = 0
    row = lambda p: p[l].reshape(1, -1)

    yp = x_prompt.reshape(batch * seq, D_MODEL)
    ys = x_sample.reshape(n_s, D_MODEL)

    ffn1 = (row(g_ffn1_pre), row(g_ffn1_post), ffn1_wg[l], ffn1_wu[l], ffn1_wd[l])
    ffn2 = (row(g_ffn2_pre), row(g_ffn2_post), ffn2_wg[l], ffn2_wu[l], ffn2_wd[l])
    mix_w = _mix_weights(g_mix_pre[l], g_mix_post[l], w_in[l], conv_a_w[l], conv_b_w[l], conv_b_b[l], lru_wa[l],
                         lru_ba[l], lru_wx[l], lru_bx[l], lru_lam[l], w_out[l])

    mk, mv = _memory_kv(mem_prompt.reshape(batch * N_MEM, D_MODEL), row(g_mem), xattn_wk[l], xattn_wv[l])

    yp = _ffn(yp, *ffn1, row_tile=FFN_ROW_TILE)
    yp, tail_a, tail_b, tail_h = _mix_prompt(yp.reshape(batch, seq, D_MODEL), mix_w, batch, seq)
    yp = yp.reshape(batch * seq, D_MODEL)
    tail_a = tail_a.reshape(CONV_B_WIDTH - 1, batch, D_CONV)[CONV_B_WIDTH - CONV_A_WIDTH:].transpose(1, 0, 2)
    tail_b = tail_b.reshape(CONV_B_WIDTH - 1, batch, D_LRU).transpose(1, 0, 2)
    yp = _xattn_prompt(yp, mk.reshape(batch, N_MEM, D_MODEL), mv.reshape(batch, N_MEM, D_MODEL),
                       row(g_xattn_pre), row(g_xattn_post), xattn_wq[l], xattn_wo[l], batch, seq)
    yp = _ffn(yp, *ffn2, row_tile=FFN_ROW_TILE)

    ys = _ffn(ys, *ffn1, row_tile=n_s)
    ys, new_a, new_b, new_h = _mix_sample(ys, state_conv_a[l].reshape(n_s, 2 * D_CONV),
                                          state_conv_b[l].reshape(n_s, 3 * D_LRU), state_lru[l], mix_w)
    ys = _xattn_sample(ys, _split_heads(cache_mem_k[l]), _split_heads(cache_mem_v[l]),
                       row(g_xattn_pre), row(g_xattn_post), xattn_wq[l], xattn_wo[l])
    ys = _ffn(ys, *ffn2, row_tile=n_s)

    kv_shape = (1, batch, N_MEM, N_XHEADS, XHEAD_DIM)
    return (yp.reshape(batch, seq, D_MODEL), ys.reshape(n_s, 1, D_MODEL),
            mk.reshape(kv_shape), mv.reshape(kv_shape),
            tail_a[None], tail_b[None], tail_h[None],
            new_a.reshape(1, n_s, 2, D_CONV), new_b.reshape(1, n_s, 3, D_LRU), new_h[None])
```

```python
import functools
import math

import jax
import jax.numpy as jnp
from jax import lax
from jax.experimental import pallas as pl
from jax.experimental.pallas import tpu as pltpu

D_MODEL = 1024
D_CONV = 512
D_LRU = 512
N_LRU_HEADS = 8
LRU_HEAD_DIM = D_LRU // N_LRU_HEADS
LRU_C = 8.0
CONV_A_WIDTH = 3
CONV_B_WIDTH = 4
N_MIX_PARTS = 5
FFN_DIM = 2816
N_MEM = 256
N_XHEADS = 4
XHEAD_DIM = D_MODEL // N_XHEADS
RMS_EPS = 1e-6

F32 = jnp.float32
BF16 = jnp.bfloat16

V7X_VMEM_LIMIT_BYTES = 56 * 1024 * 1024
SUBLANES = 8
LANES = 128
HEAD_ROWS = N_XHEADS * XHEAD_DIM // LANES

FFN_ROW_TILE = 512
FFN_COL_CHUNK = 256
MIX_STEPS = 32
MIX_BLOCK_STEPS = 64
MIX_CHUNK = 256
N_MIX_CHUNKS = D_CONV // MIX_CHUNK
XATTN_ROW_TILE = 1024
XATTN_SUB_ROWS = 512
KV_ROW_TILE = 512
SAMPLE_XATTN_BLOCK = 8


def _rms(x, g):
    y = x * lax.rsqrt(jnp.mean(x * x, axis=-1, keepdims=True) + RMS_EPS)
    return y * g


def _dot(a, b):
    return jnp.dot(a, b, preferred_element_type=F32)


def _sigmoid(x):
    return 1.0 / (1.0 + jnp.exp(-x))


def _gelu_tanh(x):
    c = math.sqrt(2.0 / math.pi)
    neg_two_z = x * ((-2.0 * c * 0.044715) * (x * x) - 2.0 * c)
    return x / (1.0 + jnp.exp(neg_two_z))


def _one_minus_exp2x(x):
    t = jnp.tanh(x)
    return (-2.0 * t) / (1.0 - t)


def _log1p(w):
    u = 1.0 + w
    return jnp.where(u == 1.0, w, jnp.log(u) * w / (u - 1.0))


def _softplus(x):
    return jnp.maximum(x, 0.0) + _log1p(jnp.exp(-jnp.abs(x)))


def _const_spec(shape):
    zeros = (0,) * len(shape)
    return pl.BlockSpec(shape, lambda *_: zeros, pipeline_mode=pl.Buffered(1))


def _params(*sem):
    return pltpu.CompilerParams(dimension_semantics=sem, vmem_limit_bytes=V7X_VMEM_LIMIT_BYTES)


def _ffn_kernel(x_ref, gpre_ref, gpost_ref, wg_ref, wu_ref, wd_ref, o_ref, act_ref):
    x = x_ref[...]
    h = _rms(x, gpre_ref[...]).astype(BF16)
    for c in range(FFN_DIM // FFN_COL_CHUNK):
        sl = slice(c * FFN_COL_CHUNK, (c + 1) * FFN_COL_CHUNK)
        g = _dot(h, wg_ref[:, sl].astype(BF16))
        u = _dot(h, wu_ref[:, sl].astype(BF16))
        act_ref[:, sl] = ((g * _sigmoid(g)) * u).astype(BF16)
    y = _dot(act_ref[...], wd_ref[...].astype(BF16))
    o_ref[...] = x + 0.5 * _rms(y, gpost_ref[...])


def _ffn(x, g_pre, g_post, wg, wu, wd, row_tile):
    rows = x.shape[0]
    row_spec = pl.BlockSpec((row_tile, D_MODEL), lambda i: (i, 0))
    return pl.pallas_call(
        _ffn_kernel,
        grid=(rows // row_tile,),
        in_specs=[row_spec, _const_spec((1, D_MODEL)), _const_spec((1, D_MODEL)),
                  _const_spec((D_MODEL, FFN_DIM)), _const_spec((D_MODEL, FFN_DIM)),
                  _const_spec((FFN_DIM, D_MODEL))],
        out_specs=row_spec,
        out_shape=jax.ShapeDtypeStruct((rows, D_MODEL), F32),
        scratch_shapes=[pltpu.VMEM((row_tile, FFN_DIM), BF16)],
        compiler_params=_params("parallel"),
        name="ffn",
    )(x, g_pre, g_post, wg, wu, wd)


def _chunk(ref, r, c):
    return ref[r:r + 1, c * MIX_CHUNK:(c + 1) * MIX_CHUNK]


def _mix_project(h, win_ref, c):
    parts = []
    for s in range(N_MIX_PARTS):
        o = s * D_CONV + c * MIX_CHUNK
        parts.append(_dot(h, win_ref[:, o:o + MIX_CHUNK].astype(BF16)))
    return jnp.concatenate(parts, axis=-1)


def _mix_out(ya, yb, wout_ref, c):
    ra = c * MIX_CHUNK
    rb = D_CONV + c * MIX_CHUNK
    return (_dot(ya.astype(BF16), wout_ref[ra:ra + MIX_CHUNK, :].astype(BF16))
            + _dot(yb.astype(BF16), wout_ref[rb:rb + MIX_CHUNK, :].astype(BF16)))


def _mix_front(zc, c, prev_a, prev_b, caw_ref, cbw_ref, cbb_ref, wax_ref):
    gb, gc, xa, xb, gg = (zc[:, s * MIX_CHUNK:(s + 1) * MIX_CHUNK] for s in range(N_MIX_PARTS))
    v = gc * xa
    ca = prev_a(v, 2) * _chunk(caw_ref, 0, c) + prev_a(v, 1) * _chunk(caw_ref, 1, c) + v * _chunk(caw_ref, 2, c)
    ya = gb * ca
    cb = (prev_b(xb, 3) * _chunk(cbw_ref, 0, c) + prev_b(xb, 2) * _chunk(cbw_ref, 1, c)
          + prev_b(xb, 1) * _chunk(cbw_ref, 2, c) + xb * _chunk(cbw_ref, 3, c))
    cb = cb + _chunk(cbb_ref, 0, c)
    gates = _dot(cb.astype(BF16), wax_ref[c].astype(BF16))
    return v, xb, ya, cb, gates, gg


def _lru_coeffs(gates, c, ba_ref, bx_ref, softplus_neg_lam):
    r = _sigmoid(gates[:, :MIX_CHUNK] + _chunk(ba_ref, 0, c))
    i = _sigmoid(gates[:, MIX_CHUNK:] + _chunk(bx_ref, 0, c))
    log_a = (-LRU_C * r) * softplus_neg_lam[:, c * MIX_CHUNK:(c + 1) * MIX_CHUNK]
    a = jnp.exp(log_a)
    mult = jnp.sqrt(_one_minus_exp2x(log_a))
    return a, mult, i


def _seq_copies(hbm_ref, buf_ref, sem_ref, slot, step_block, to_vmem):
    copies = []
    for b in range(SUBLANES):
        hbm = hbm_ref.at[b, pl.ds(step_block * MIX_BLOCK_STEPS, MIX_BLOCK_STEPS), :]
        vmem = buf_ref.at[slot, :, b, :]
        src, dst = (hbm, vmem) if to_vmem else (vmem, hbm)
        copies.append(pltpu.make_async_copy(src, dst, sem_ref.at[slot, b]))
    return copies


def _mix_prompt_kernel(x_hbm, gpre_ref, gpost_ref, win_ref, caw_ref, cbw_ref, cbb_ref, wax_ref, ba_ref, bx_ref,
                       lam_ref, wout_ref, o_hbm, ta_ref, tb_ref, hl_ref, hista_ref, histb_ref, hc_ref, z_ref,
                       xbuf_ref, obuf_ref, xsem_ref, osem_ref, winb_ref, waxb_ref, woutb_ref, *, n_tiles):
    i = pl.program_id(0)
    nb = SUBLANES
    rows = MIX_STEPS * nb
    hist_rows = (CONV_B_WIDTH - 1) * nb
    items = [(u, c) for u in range(MIX_BLOCK_STEPS // MIX_STEPS) for c in range(N_MIX_CHUNKS)]

    @pl.when(i == 0)
    def _():
        hista_ref[...] = jnp.zeros_like(hista_ref)
        histb_ref[...] = jnp.zeros_like(histb_ref)
        hc_ref[...] = jnp.zeros_like(hc_ref)
        for cp in _seq_copies(x_hbm, xbuf_ref, xsem_ref, 0, 0, True):
            cp.start()
        winb_ref[...] = win_ref[...].astype(BF16)
        waxb_ref[...] = wax_ref[...].astype(BF16)
        woutb_ref[...] = wout_ref[...].astype(BF16)

    @pl.when(i + 1 < n_tiles)
    def _():
        for cp in _seq_copies(x_hbm, xbuf_ref, xsem_ref, (i + 1) % 3, i + 1, True):
            cp.start()

    @pl.when(i < n_tiles)
    def _():
        for cp in _seq_copies(x_hbm, xbuf_ref, xsem_ref, i % 3, i, True):
            cp.wait()

    @pl.when(i >= 3)
    def _():
        for cp in _seq_copies(o_hbm, obuf_ref, osem_ref, (i - 3) % 2, i - 3, False):
            cp.wait()

    def prev(hist):
        return lambda cur, k: jnp.concatenate([hist[hist_rows - k * nb:, :], cur[:rows - k * nb, :]], axis=0)

    def x_rows(tile, u):
        return xbuf_ref[tile % 3, u * MIX_STEPS:(u + 1) * MIX_STEPS].reshape(rows, D_MODEL)

    def run(proj_tile, mix_tile):
        proj_tile, proj_slot = proj_tile if proj_tile is not None else (None, None)
        mix_tile, mix_slot = mix_tile if mix_tile is not None else (None, None)
        normed = {}
        if mix_tile is not None:
            sp = _softplus(-lam_ref[...])
            row = lax.broadcasted_iota(jnp.int32, (rows, MIX_CHUNK), 0)
            first_rows = jnp.where(mix_tile == 0, nb, 0)
            hist_a = hista_ref[...]
            hist_b = histb_ref[...]
            hcur = hc_ref[...]

        def project(u, c):
            if u not in normed:
                normed[u] = _rms(x_rows(proj_tile, u), gpre_ref[...]).astype(BF16)
            z_ref[proj_slot, u, c] = _mix_project(normed[u], winb_ref, c)

        for u, c in items:
            if mix_tile is None:
                project(u, c)
                continue
            if c == 0:
                y = None
                tails_a, tails_b, h_last = [], [], []
            csl = slice(c * MIX_CHUNK, (c + 1) * MIX_CHUNK)
            zc = z_ref[mix_slot, u, c]
            v, xb, ya, cb, gates, gg = _mix_front(zc, c, prev(hist_a[:, csl]), prev(hist_b[:, csl]),
                                                  caw_ref, cbw_ref, cbb_ref, waxb_ref)
            if proj_tile is not None:
                project(u, c)
            a, mult, gate_in = _lru_coeffs(gates, c, ba_ref, bx_ref, sp)
            if u == 0:
                mult = jnp.where(row < first_rows, 1.0, mult)
            b = (mult * gate_in) * cb
            hc = hcur[:, csl]
            hs = []
            for j in range(MIX_STEPS):
                sl = slice(j * nb, (j + 1) * nb)
                hc = a[sl, :] * hc + b[sl, :]
                hs.append(hc)
            yb = _gelu_tanh(gg) * jnp.concatenate(hs, axis=0)
            yc = _mix_out(ya, yb, woutb_ref, c)
            y = yc if y is None else y + yc
            tails_a.append(v[rows - hist_rows:, :])
            tails_b.append(xb[rows - hist_rows:, :])
            h_last.append(hc)
            if c == N_MIX_CHUNKS - 1:
                hist_a = jnp.concatenate(tails_a, axis=-1)
                hist_b = jnp.concatenate(tails_b, axis=-1)
                hcur = jnp.concatenate(h_last, axis=-1)
                out = x_rows(mix_tile, u) + _rms(y, gpost_ref[...])
                obuf_ref[mix_tile % 2, u * MIX_STEPS:(u + 1) * MIX_STEPS] = out.reshape(MIX_STEPS, nb, D_MODEL)
        if mix_tile is not None:
            hista_ref[...] = hist_a
            histb_ref[...] = hist_b
            hc_ref[...] = hcur

    @pl.when(i == 0)
    def _():
        run((i, 0), None)

    for parity in range(2):
        @pl.when((i > 0) & (i < n_tiles) & (i % 2 == parity))
        def _():
            run((i, parity), (i - 1, 1 - parity))

    @pl.when(i == n_tiles)
    def _():
        run(None, (i - 1, (n_tiles - 1) % 2))
        ta_ref[...] = hista_ref[...]
        tb_ref[...] = histb_ref[...]
        hl_ref[...] = hc_ref[...]

    @pl.when(i >= 1)
    def _():
        for cp in _seq_copies(o_hbm, obuf_ref, osem_ref, (i - 1) % 2, i - 1, False):
            cp.start()

    @pl.when(i == n_tiles)
    def _():
        for tile in range(max(n_tiles - 2, 0), n_tiles):
            for cp in _seq_copies(o_hbm, obuf_ref, osem_ref, tile % 2, tile, False):
                cp.wait()


def _mix_weight_specs():
    return [_const_spec((1, D_MODEL)), _const_spec((1, D_MODEL)),
            _const_spec((D_MODEL, N_MIX_PARTS * D_CONV)),
            _const_spec((CONV_A_WIDTH, D_CONV)), _const_spec((CONV_B_WIDTH, D_LRU)), _const_spec((1, D_LRU)),
            _const_spec((N_MIX_CHUNKS, MIX_CHUNK, 2 * MIX_CHUNK)), _const_spec((1, D_LRU)), _const_spec((1, D_LRU)),
            _const_spec((1, D_LRU)), _const_spec((D_MODEL, D_MODEL))]


def _mix_prompt(x, weights, batch, seq):
    assert batch == SUBLANES and seq % MIX_BLOCK_STEPS == 0
    n_tiles = seq // MIX_BLOCK_STEPS
    n_sub = MIX_BLOCK_STEPS // MIX_STEPS
    hist_rows = (CONV_B_WIDTH - 1) * batch
    hbm_spec = pl.BlockSpec(memory_space=pl.ANY)
    hist_shape = jax.ShapeDtypeStruct((hist_rows, D_CONV), F32)
    h_shape = jax.ShapeDtypeStruct((batch, D_LRU), F32)
    return pl.pallas_call(
        functools.partial(_mix_prompt_kernel, n_tiles=n_tiles),
        grid=(n_tiles + 1,),
        in_specs=[hbm_spec] + _mix_weight_specs(),
        out_specs=[hbm_spec, _const_spec(hist_shape.shape), _const_spec(hist_shape.shape), _const_spec(h_shape.shape)],
        out_shape=[jax.ShapeDtypeStruct(x.shape, F32), hist_shape, hist_shape, h_shape],
        scratch_shapes=[pltpu.VMEM(hist_shape.shape, F32), pltpu.VMEM(hist_shape.shape, F32),
                        pltpu.VMEM(h_shape.shape, F32),
                        pltpu.VMEM((2, n_sub, N_MIX_CHUNKS, MIX_STEPS * batch, N_MIX_PARTS * MIX_CHUNK), F32),
                        pltpu.VMEM((3, MIX_BLOCK_STEPS, batch, D_MODEL), F32),
                        pltpu.VMEM((2, MIX_BLOCK_STEPS, batch, D_MODEL), F32),
                        pltpu.SemaphoreType.DMA((3, batch)), pltpu.SemaphoreType.DMA((2, batch)),
                        pltpu.VMEM((D_MODEL, N_MIX_PARTS * D_CONV), BF16),
                        pltpu.VMEM((N_MIX_CHUNKS, MIX_CHUNK, 2 * MIX_CHUNK), BF16),
                        pltpu.VMEM((D_MODEL, D_MODEL), BF16)],
        compiler_params=_params("arbitrary"),
        name="mix_prompt",
    )(x, *weights)


def _mix_sample_kernel(x_ref, sa_ref, sb_ref, h0_ref, gpre_ref, gpost_ref, win_ref, caw_ref, cbw_ref, cbb_ref,
                       wax_ref, ba_ref, bx_ref, lam_ref, wout_ref, o_ref, na_ref, nb_ref, hl_ref):
    x = x_ref[...]
    h = _rms(x, gpre_ref[...]).astype(BF16)
    sp = _softplus(-lam_ref[...])
    y = None
    for c in range(N_MIX_CHUNKS):
        csl = slice(c * MIX_CHUNK, (c + 1) * MIX_CHUNK)
        zc = _mix_project(h, win_ref, c)

        def state_a(k):
            o = (CONV_A_WIDTH - 1 - k) * D_CONV + c * MIX_CHUNK
            return sa_ref[:, o:o + MIX_CHUNK]

        def state_b(k):
            o = (CONV_B_WIDTH - 1 - k) * D_LRU + c * MIX_CHUNK
            return sb_ref[:, o:o + MIX_CHUNK]

        v, xb, ya, cb, gates, gg = _mix_front(zc, c, lambda cur, k: state_a(k), lambda cur, k: state_b(k),
                                              caw_ref, cbw_ref, cbb_ref, wax_ref)
        a, mult, i = _lru_coeffs(gates, c, ba_ref, bx_ref, sp)
        hnew = a * h0_ref[:, csl] + (mult * i) * cb
        yb = _gelu_tanh(gg) * hnew
        yc = _mix_out(ya, yb, wout_ref, c)
        y = yc if y is None else y + yc

        na_ref[:, c * MIX_CHUNK:(c + 1) * MIX_CHUNK] = state_a(1)
        na_ref[:, D_CONV + c * MIX_CHUNK:D_CONV + (c + 1) * MIX_CHUNK] = v
        nb_ref[:, c * MIX_CHUNK:(c + 1) * MIX_CHUNK] = state_b(2)
        nb_ref[:, D_LRU + c * MIX_CHUNK:D_LRU + (c + 1) * MIX_CHUNK] = state_b(1)
        nb_ref[:, 2 * D_LRU + c * MIX_CHUNK:2 * D_LRU + (c + 1) * MIX_CHUNK] = xb
        hl_ref[:, csl] = hnew
    o_ref[...] = x + _rms(y, gpost_ref[...])


def _mix_sample(x, sa, sb, h0, weights):
    full = lambda a: _const_spec(a.shape)
    return pl.pallas_call(
        _mix_sample_kernel,
        grid=(1,),
        in_specs=[full(x), full(sa), full(sb), full(h0)] + _mix_weight_specs(),
        out_specs=[full(x), full(sa), full(sb), full(h0)],
        out_shape=[jax.ShapeDtypeStruct(x.shape, F32), jax.ShapeDtypeStruct(sa.shape, F32),
                   jax.ShapeDtypeStruct(sb.shape, F32), jax.ShapeDtypeStruct(h0.shape, F32)],
        compiler_params=_params("arbitrary"),
        name="mix_sample",
    )(x, sa, sb, h0, *weights)


def _kv_kernel(m_ref, g_ref, wk_ref, wv_ref, k_ref, v_ref):
    m = _rms(m_ref[...], g_ref[...]).astype(BF16)
    k_ref[...] = _dot(m, wk_ref[...].astype(BF16))
    v_ref[...] = _dot(m, wv_ref[...].astype(BF16))


def _memory_kv(mem, g_mem, wk, wv):
    rows = mem.shape[0]
    row_spec = pl.BlockSpec((KV_ROW_TILE, D_MODEL), lambda i: (i, 0))
    out = jax.ShapeDtypeStruct((rows, D_MODEL), F32)
    return pl.pallas_call(
        _kv_kernel,
        grid=(rows // KV_ROW_TILE,),
        in_specs=[row_spec, _const_spec((1, D_MODEL)), _const_spec((D_MODEL, D_MODEL)),
                  _const_spec((D_MODEL, D_MODEL))],
        out_specs=[row_spec, row_spec],
        out_shape=[out, out],
        compiler_params=_params("parallel"),
        name="memory_kv",
    )(mem, g_mem, wk, wv)


def _softmax_rows(s):
    e = jnp.exp(s - jnp.max(s, axis=-1, keepdims=True))
    return e * (1.0 / jnp.sum(e, axis=-1, keepdims=True))


def _xattn_prompt_kernel(x_ref, k_ref, v_ref, gpre_ref, gpost_ref, wq_ref, wo_ref, o_ref):
    wq = wq_ref[...].astype(BF16)
    wo = wo_ref[...].astype(BF16)
    n_sub = x_ref.shape[0] // XATTN_SUB_ROWS
    subs = [slice(u * XATTN_SUB_ROWS, (u + 1) * XATTN_SUB_ROWS) for u in range(n_sub)]
    xs = [x_ref[r, :] for r in subs]
    qs = [_dot(_rms(x, gpre_ref[...]).astype(BF16), wq).astype(BF16) for x in xs]
    attn = []
    for q in qs:
        heads = []
        for hd in range(N_XHEADS):
            sl = slice(hd * XHEAD_DIM, (hd + 1) * XHEAD_DIM)
            kh = k_ref[0, :, sl].astype(BF16)
            vh = v_ref[0, :, sl].astype(BF16)
            s = lax.dot_general(q[:, sl], kh, (((1,), (1,)), ((), ())), preferred_element_type=F32)
            p = _softmax_rows(s * (XHEAD_DIM ** -0.5)).astype(BF16)
            heads.append(_dot(p, vh).astype(BF16))
        attn.append(jnp.concatenate(heads, axis=-1))
    for r, x, a in zip(subs, xs, attn):
        o_ref[r, :] = x + _rms(_dot(a, wo), gpost_ref[...])


def _xattn_prompt(x, mem_k, mem_v, g_pre, g_post, wq, wo, batch, seq):
    n_t = seq // XATTN_ROW_TILE
    row_spec = pl.BlockSpec((XATTN_ROW_TILE, D_MODEL), lambda b, t: (b * n_t + t, 0))
    kv_spec = pl.BlockSpec((1, N_MEM, D_MODEL), lambda b, t: (b, 0, 0))
    return pl.pallas_call(
        _xattn_prompt_kernel,
        grid=(batch, n_t),
        in_specs=[row_spec, kv_spec, kv_spec, _const_spec((1, D_MODEL)), _const_spec((1, D_MODEL)),
                  _const_spec((D_MODEL, D_MODEL)), _const_spec((D_MODEL, D_MODEL))],
        out_specs=row_spec,
        out_shape=jax.ShapeDtypeStruct(x.shape, F32),
        compiler_params=_params("parallel", "arbitrary"),
        name="xattn_prompt",
    )(x, mem_k, mem_v, g_pre, g_post, wq, wo)


def _q_sample_kernel(x_ref, gpre_ref, wq_ref, q_ref):
    q_ref[...] = _dot(_rms(x_ref[...], gpre_ref[...]).astype(BF16), wq_ref[...].astype(BF16))


def _split_heads(a):
    n = a.shape[0]
    a = a.reshape(n, -1, N_XHEADS, 2, LANES)
    return jnp.swapaxes(a, 2, 3).reshape(n, -1, LANES)


def _attend_sample_kernel(q_ref, k_ref, v_ref, ones_ref, o_ref):
    rows = N_MEM * HEAD_ROWS
    for b in range(q_ref.shape[0]):
        qv = q_ref[b] * (XHEAD_DIM ** -0.5)
        prod = k_ref[b].reshape(N_MEM, HEAD_ROWS, LANES) * qv[None]
        part = _dot(prod.reshape(rows, LANES).astype(BF16), ones_ref[...]).reshape(N_MEM, HEAD_ROWS, LANES)
        s = part + pltpu.roll(part, N_XHEADS, 1)
        e = jnp.exp(s - jnp.max(s, axis=0, keepdims=True))
        den = jnp.sum(e, axis=0)
        num = jnp.sum(e * v_ref[b].reshape(N_MEM, HEAD_ROWS, LANES), axis=0)
        o_ref[b] = num * (1.0 / den)


def _out_sample_kernel(x_ref, o_ref, gpost_ref, wo_ref, y_ref):
    y = _dot(o_ref[...].astype(BF16), wo_ref[...].astype(BF16))
    y_ref[...] = x_ref[...] + _rms(y, gpost_ref[...])


def _xattn_sample(x, cache_k, cache_v, g_pre, g_post, wq, wo):
    n = x.shape[0]
    full = lambda shape: _const_spec(shape)
    q = pl.pallas_call(
        _q_sample_kernel,
        grid=(1,),
        in_specs=[full(x.shape), full((1, D_MODEL)), full((D_MODEL, D_MODEL))],
        out_specs=full(x.shape),
        out_shape=jax.ShapeDtypeStruct(x.shape, F32),
        compiler_params=_params("arbitrary"),
        name="q_sample",
    )(x, g_pre, wq)
    nb = SAMPLE_XATTN_BLOCK
    rows = N_MEM * HEAD_ROWS
    q_spec = pl.BlockSpec((nb, HEAD_ROWS, LANES), lambda i: (i, 0, 0))
    kv_spec = pl.BlockSpec((nb, rows, LANES), lambda i: (i, 0, 0))
    o = pl.pallas_call(
        _attend_sample_kernel,
        grid=(n // nb,),
        in_specs=[q_spec, kv_spec, kv_spec, _const_spec((LANES, LANES))],
        out_specs=q_spec,
        out_shape=jax.ShapeDtypeStruct((n, HEAD_ROWS, LANES), F32),
        compiler_params=_params("parallel"),
        name="attend_sample",
    )(_split_heads(q), cache_k, cache_v, jnp.ones((LANES, LANES), BF16))
    o = o.reshape(n, 2, N_XHEADS, LANES).transpose(0, 2, 1, 3).reshape(n, D_MODEL)
    return pl.pallas_call(
        _out_sample_kernel,
        grid=(1,),
        in_specs=[full(x.shape), full(x.shape), full((1, D_MODEL)), full((D_MODEL, D_MODEL))],
        out_specs=full(x.shape),
        out_shape=jax.ShapeDtypeStruct(x.shape, F32),
        compiler_params=_params("arbitrary"),
        name="out_sample",
    )(x, o, g_post, wo)


def _block_diag(w):
    groups, heads, d, _ = w.shape
    eye = jnp.eye(heads, dtype=w.dtype)
    return (eye[None, :, None, :, None] * w[:, :, :, None, :]).reshape(groups, heads * d, heads * d)


def _mix_weights(g_pre, g_post, w_in, conv_a_w, conv_b_w, conv_b_b, lru_wa, lru_ba, lru_wx, lru_bx, lru_lam, w_out):
    row = lambda p: p.reshape(1, -1)
    heads_per_chunk = MIX_CHUNK // LRU_HEAD_DIM
    grouped = lambda w: _block_diag(w.reshape(N_MIX_CHUNKS, heads_per_chunk, LRU_HEAD_DIM, LRU_HEAD_DIM))
    wax = jnp.concatenate([grouped(lru_wa), grouped(lru_wx)], axis=-1)
    return (row(g_pre), row(g_post), w_in, conv_a_w, conv_b_w, row(conv_b_b), wax, row(lru_ba), row(lru_bx),
            row(lru_lam), w_out)


def kernel(x_prompt, x_sample, mem_prompt, cache_mem_k, cache_mem_v, state_conv_a, state_conv_b, state_lru, g_ffn1_pre, g_ffn1_post, ffn1_wg, ffn1_wu, ffn1_wd, g_mix_pre, g_mix_post, w_in, conv_a_w, conv_b_w, conv_b_b, lru_wa, lru_ba, lru_wx, lru_bx, lru_lam, w_out, g_xattn_pre, g_xattn_post, g_mem, xattn_wq, xattn_wk, xattn_wv, xattn_wo, g_ffn2_pre, g_ffn2_post, ffn2_wg, ffn2_wu, ffn2_wd):
    batch, seq, _ = x_prompt.shape
    n_s = x_sample.shape[0]
    depth = g_ffn1_pre.shape[0]
    assert depth == 1 and x_sample.shape[1] == 1
    l = 0
    row = lambda p: p[l].reshape(1, -1)

    yp = x_prompt.reshape(batch * seq, D_MODEL)
    ys = x_sample.reshape(n_s, D_MODEL)

    ffn1 = (row(g_ffn1_pre), row(g_ffn1_post), ffn1_wg[l], ffn1_wu[l], ffn1_wd[l])
    ffn2 = (row(g_ffn2_pre), row(g_ffn2_post), ffn2_wg[l], ffn2_wu[l], ffn2_wd[l])
    mix_w = _mix_weights(g_mix_pre[l], g_mix_post[l], w_in[l], conv_a_w[l], conv_b_w[l], conv_b_b[l], lru_wa[l],
                         lru_ba[l], lru_wx[l], lru_bx[l], lru_lam[l], w_out[l])

    mk, mv = _memory_kv(mem_prompt.reshape(batch * N_MEM, D_MODEL), row(g_mem), xattn_wk[l], xattn_wv[l])

    yp = _ffn(yp, *ffn1, row_tile=FFN_ROW_TILE)
    yp, tail_a, tail_b, tail_h = _mix_prompt(yp.reshape(batch, seq, D_MODEL), mix_w, batch, seq)
    yp = yp.reshape(batch * seq, D_MODEL)
    tail_a = tail_a.reshape(CONV_B_WIDTH - 1, batch, D_CONV)[CONV_B_WIDTH - CONV_A_WIDTH:].transpose(1, 0, 2)
    tail_b = tail_b.reshape(CONV_B_WIDTH - 1, batch, D_LRU).transpose(1, 0, 2)
    yp = _xattn_prompt(yp, mk.reshape(batch, N_MEM, D_MODEL), mv.reshape(batch, N_MEM, D_MODEL),
                       row(g_xattn_pre), row(g_xattn_post), xattn_wq[l], xattn_wo[l], batch, seq)
    yp = _ffn(yp, *ffn2, row_tile=FFN_ROW_TILE)

    ys = _ffn(ys, *ffn1, row_tile=n_s)
    ys, new_a, new_b, new_h = _mix_sample(ys, state_conv_a[l].reshape(n_s, 2 * D_CONV),
                                          state_conv_b[l].reshape(n_s, 3 * D_LRU), state_lru[l], mix_w)
    ys = _xattn_sample(ys, _split_heads(cache_mem_k[l]), _split_heads(cache_mem_v[l]),
                       row(g_xattn_pre), row(g_xattn_post), xattn_wq[l], xattn_wo[l])
    ys = _ffn(ys, *ffn2, row_tile=n_s)

    kv_shape = (1, batch, N_MEM, N_XHEADS, XHEAD_DIM)
    return (yp.reshape(batch, seq, D_MODEL), ys.reshape(n_s, 1, D_MODEL),
            mk.reshape(kv_shape), mv.reshape(kv_shape),
            tail_a[None], tail_b[None], tail_h[None],
            new_a.reshape(1, n_s, 2, D_CONV), new_b.reshape(1, n_s, 3, D_LRU), new_h[None])
```

```python
import functools
import math

import jax
import jax.numpy as jnp
from jax import lax
from jax.experimental import pallas as pl
from jax.experimental.pallas import tpu as pltpu

D_MODEL = 1024
D_CONV = 512
D_LRU = 512
N_LRU_HEADS = 8
LRU_HEAD_DIM = D_LRU // N_LRU_HEADS
LRU_C = 8.0
CONV_A_WIDTH = 3
CONV_B_WIDTH = 4
N_MIX_PARTS = 5
FFN_DIM = 2816
N_MEM = 256
N_XHEADS = 4
XHEAD_DIM = D_MODEL // N_XHEADS
RMS_EPS = 1e-6

F32 = jnp.float32
BF16 = jnp.bfloat16

V7X_VMEM_LIMIT_BYTES = 56 * 1024 * 1024
SUBLANES = 8
LANES = 128
HEAD_ROWS = N_XHEADS * XHEAD_DIM // LANES

FFN_ROW_TILE = 512
FFN_COL_CHUNK = 256
MIX_STEPS = 32
MIX_BLOCK_STEPS = 64
MIX_CHUNK = 256
N_MIX_CHUNKS = D_CONV // MIX_CHUNK
XATTN_ROW_TILE = 1024
XATTN_SUB_ROWS = 512
KV_ROW_TILE = 512
SAMPLE_XATTN_BLOCK = 8


def _rms(x, g):
    y = x * lax.rsqrt(jnp.mean(x * x, axis=-1, keepdims=True) + RMS_EPS)
    return y * g


def _dot(a, b):
    return jnp.dot(a, b, preferred_element_type=F32)


def _sigmoid(x):
    return 1.0 / (1.0 + jnp.exp(-x))


def _gelu_tanh(x):
    c = math.sqrt(2.0 / math.pi)
    neg_two_z = x * ((-2.0 * c * 0.044715) * (x * x) - 2.0 * c)
    return x / (1.0 + jnp.exp(neg_two_z))


def _one_minus_exp2x(x):
    t = jnp.tanh(x)
    return (-2.0 * t) / (1.0 - t)


def _log1p(w):
    u = 1.0 + w
    return jnp.where(u == 1.0, w, jnp.log(u) * w / (u - 1.0))


def _softplus(x):
    return jnp.maximum(x, 0.0) + _log1p(jnp.exp(-jnp.abs(x)))


def _const_spec(shape):
    zeros = (0,) * len(shape)
    return pl.BlockSpec(shape, lambda *_: zeros, pipeline_mode=pl.Buffered(1))


def _params(*sem):
    return pltpu.CompilerParams(dimension_semantics=sem, vmem_limit_bytes=V7X_VMEM_LIMIT_BYTES)


def _ffn_rows(x_ref, o_ref, gpre_ref, gpost_ref, wg_ref, wu_ref, wd_ref, act_ref):
    rows = x_ref.shape[0]
    x = x_ref[...]
    h = _rms(x, gpre_ref[...]).astype(BF16)
    for c in range(FFN_DIM // FFN_COL_CHUNK):
        sl = slice(c * FFN_COL_CHUNK, (c + 1) * FFN_COL_CHUNK)
        g = _dot(h, wg_ref[:, sl].astype(BF16))
        u = _dot(h, wu_ref[:, sl].astype(BF16))
        act_ref[0:rows, sl] = ((g * _sigmoid(g)) * u).astype(BF16)
    y = _dot(act_ref[0:rows, :], wd_ref[...].astype(BF16))
    o_ref[...] = x + 0.5 * _rms(y, gpost_ref[...])


def _ffn_kernel(xp_ref, xs_ref, gpre_ref, gpost_ref, wg_ref, wu_ref, wd_ref, op_ref, os_ref, act_ref, *, n_prompt):
    i = pl.program_id(0)

    @pl.when(i < n_prompt)
    def _():
        _ffn_rows(xp_ref, op_ref, gpre_ref, gpost_ref, wg_ref, wu_ref, wd_ref, act_ref)

    @pl.when(i == n_prompt)
    def _():
        _ffn_rows(xs_ref, os_ref, gpre_ref, gpost_ref, wg_ref, wu_ref, wd_ref, act_ref)


def _ffn(xp, xs, g_pre, g_post, wg, wu, wd):
    n_prompt = xp.shape[0] // FFN_ROW_TILE
    prompt_spec = pl.BlockSpec((FFN_ROW_TILE, D_MODEL), lambda i: (jnp.minimum(i, n_prompt - 1), 0))
    return pl.pallas_call(
        functools.partial(_ffn_kernel, n_prompt=n_prompt),
        grid=(n_prompt + 1,),
        in_specs=[prompt_spec, _const_spec(xs.shape), _const_spec((1, D_MODEL)), _const_spec((1, D_MODEL)),
                  _const_spec((D_MODEL, FFN_DIM)), _const_spec((D_MODEL, FFN_DIM)),
                  _const_spec((FFN_DIM, D_MODEL))],
        out_specs=[prompt_spec, _const_spec(xs.shape)],
        out_shape=[jax.ShapeDtypeStruct(xp.shape, F32), jax.ShapeDtypeStruct(xs.shape, F32)],
        scratch_shapes=[pltpu.VMEM((FFN_ROW_TILE, FFN_DIM), BF16)],
        compiler_params=_params("arbitrary"),
        name="ffn",
    )(xp, xs, g_pre, g_post, wg, wu, wd)


def _chunk(ref, r, c):
    return ref[r:r + 1, c * MIX_CHUNK:(c + 1) * MIX_CHUNK]


def _mix_project(h, win_ref, c):
    parts = []
    for s in range(N_MIX_PARTS):
        o = s * D_CONV + c * MIX_CHUNK
        parts.append(_dot(h, win_ref[:, o:o + MIX_CHUNK].astype(BF16)))
    return jnp.concatenate(parts, axis=-1)


def _mix_out(ya, yb, wout_ref, c):
    ra = c * MIX_CHUNK
    rb = D_CONV + c * MIX_CHUNK
    return (_dot(ya.astype(BF16), wout_ref[ra:ra + MIX_CHUNK, :].astype(BF16))
            + _dot(yb.astype(BF16), wout_ref[rb:rb + MIX_CHUNK, :].astype(BF16)))


def _mix_front(zc, c, prev_a, prev_b, caw_ref, cbw_ref, cbb_ref, wax_ref):
    gb, gc, xa, xb, gg = (zc[:, s * MIX_CHUNK:(s + 1) * MIX_CHUNK] for s in range(N_MIX_PARTS))
    v = gc * xa
    ca = prev_a(v, 2) * _chunk(caw_ref, 0, c) + prev_a(v, 1) * _chunk(caw_ref, 1, c) + v * _chunk(caw_ref, 2, c)
    ya = gb * ca
    cb = (prev_b(xb, 3) * _chunk(cbw_ref, 0, c) + prev_b(xb, 2) * _chunk(cbw_ref, 1, c)
          + prev_b(xb, 1) * _chunk(cbw_ref, 2, c) + xb * _chunk(cbw_ref, 3, c))
    cb = cb + _chunk(cbb_ref, 0, c)
    gates = _dot(cb.astype(BF16), wax_ref[c].astype(BF16))
    return v, xb, ya, cb, gates, gg


def _lru_coeffs(gates, c, ba_ref, bx_ref, softplus_neg_lam):
    r = _sigmoid(gates[:, :MIX_CHUNK] + _chunk(ba_ref, 0, c))
    i = _sigmoid(gates[:, MIX_CHUNK:] + _chunk(bx_ref, 0, c))
    log_a = (-LRU_C * r) * softplus_neg_lam[:, c * MIX_CHUNK:(c + 1) * MIX_CHUNK]
    a = jnp.exp(log_a)
    mult = jnp.sqrt(_one_minus_exp2x(log_a))
    return a, mult, i


def _seq_copies(hbm_ref, buf_ref, sem_ref, slot, step_block, to_vmem):
    copies = []
    for b in range(SUBLANES):
        hbm = hbm_ref.at[b, pl.ds(step_block * MIX_BLOCK_STEPS, MIX_BLOCK_STEPS), :]
        vmem = buf_ref.at[slot, :, b, :]
        src, dst = (hbm, vmem) if to_vmem else (vmem, hbm)
        copies.append(pltpu.make_async_copy(src, dst, sem_ref.at[slot, b]))
    return copies


def _mix_prompt_kernel(x_hbm, gpre_ref, gpost_ref, win_ref, caw_ref, cbw_ref, cbb_ref, wax_ref, ba_ref, bx_ref,
                       lam_ref, wout_ref, o_hbm, ta_ref, tb_ref, hl_ref, hista_ref, histb_ref, hc_ref, z_ref,
                       xbuf_ref, obuf_ref, xsem_ref, osem_ref, winb_ref, waxb_ref, woutb_ref, *, n_tiles):
    i = pl.program_id(0)
    nb = SUBLANES
    rows = MIX_STEPS * nb
    hist_rows = (CONV_B_WIDTH - 1) * nb
    items = [(u, c) for u in range(MIX_BLOCK_STEPS // MIX_STEPS) for c in range(N_MIX_CHUNKS)]

    @pl.when(i == 0)
    def _():
        hista_ref[...] = jnp.zeros_like(hista_ref)
        histb_ref[...] = jnp.zeros_like(histb_ref)
        hc_ref[...] = jnp.zeros_like(hc_ref)
        for cp in _seq_copies(x_hbm, xbuf_ref, xsem_ref, 0, 0, True):
            cp.start()
        winb_ref[...] = win_ref[...].astype(BF16)
        waxb_ref[...] = wax_ref[...].astype(BF16)
        woutb_ref[...] = wout_ref[...].astype(BF16)

    @pl.when(i + 1 < n_tiles)
    def _():
        for cp in _seq_copies(x_hbm, xbuf_ref, xsem_ref, (i + 1) % 3, i + 1, True):
            cp.start()

    @pl.when(i < n_tiles)
    def _():
        for cp in _seq_copies(x_hbm, xbuf_ref, xsem_ref, i % 3, i, True):
            cp.wait()

    @pl.when(i >= 3)
    def _():
        for cp in _seq_copies(o_hbm, obuf_ref, osem_ref, (i - 3) % 2, i - 3, False):
            cp.wait()

    def prev(hist):
        return lambda cur, k: jnp.concatenate([hist[hist_rows - k * nb:, :], cur[:rows - k * nb, :]], axis=0)

    def x_rows(tile, u):
        return xbuf_ref[tile % 3, u * MIX_STEPS:(u + 1) * MIX_STEPS].reshape(rows, D_MODEL)

    def run(proj_tile, mix_tile):
        proj_tile, proj_slot = proj_tile if proj_tile is not None else (None, None)
        mix_tile, mix_slot = mix_tile if mix_tile is not None else (None, None)
        normed = {}
        if mix_tile is not None:
            sp = _softplus(-lam_ref[...])
            row = lax.broadcasted_iota(jnp.int32, (rows, MIX_CHUNK), 0)
            first_rows = jnp.where(mix_tile == 0, nb, 0)
            hist_a = hista_ref[...]
            hist_b = histb_ref[...]
            hcur = hc_ref[...]

        def project(u, c):
            if u not in normed:
                normed[u] = _rms(x_rows(proj_tile, u), gpre_ref[...]).astype(BF16)
            z_ref[proj_slot, u, c] = _mix_project(normed[u], winb_ref, c)

        for u, c in items:
            if mix_tile is None:
                project(u, c)
                continue
            if c == 0:
                y = None
                tails_a, tails_b, h_last = [], [], []
            csl = slice(c * MIX_CHUNK, (c + 1) * MIX_CHUNK)
            zc = z_ref[mix_slot, u, c]
            v, xb, ya, cb, gates, gg = _mix_front(zc, c, prev(hist_a[:, csl]), prev(hist_b[:, csl]),
                                                  caw_ref, cbw_ref, cbb_ref, waxb_ref)
            if proj_tile is not None:
                project(u, c)
            a, mult, gate_in = _lru_coeffs(gates, c, ba_ref, bx_ref, sp)
            if u == 0:
                mult = jnp.where(row < first_rows, 1.0, mult)
            b = (mult * gate_in) * cb
            hc = hcur[:, csl]
            hs = []
            for j in range(MIX_STEPS):
                sl = slice(j * nb, (j + 1) * nb)
                hc = a[sl, :] * hc + b[sl, :]
                hs.append(hc)
            yb = _gelu_tanh(gg) * jnp.concatenate(hs, axis=0)
            yc = _mix_out(ya, yb, woutb_ref, c)
            y = yc if y is None else y + yc
            tails_a.append(v[rows - hist_rows:, :])
            tails_b.append(xb[rows - hist_rows:, :])
            h_last.append(hc)
            if c == N_MIX_CHUNKS - 1:
                hist_a = jnp.concatenate(tails_a, axis=-1)
                hist_b = jnp.concatenate(tails_b, axis=-1)
                hcur = jnp.concatenate(h_last, axis=-1)
                out = x_rows(mix_tile, u) + _rms(y, gpost_ref[...])
                obuf_ref[mix_tile % 2, u * MIX_STEPS:(u + 1) * MIX_STEPS] = out.reshape(MIX_STEPS, nb, D_MODEL)
        if mix_tile is not None:
            hista_ref[...] = hist_a
            histb_ref[...] = hist_b
            hc_ref[...] = hcur

    @pl.when(i == 0)
    def _():
        run((i, 0), None)

    for parity in range(2):
        @pl.when((i > 0) & (i < n_tiles) & (i % 2 == parity))
        def _():
            run((i, parity), (i - 1, 1 - parity))

    @pl.when(i == n_tiles)
    def _():
        run(None, (i - 1, (n_tiles - 1) % 2))
        ta_ref[...] = hista_ref[...]
        tb_ref[...] = histb_ref[...]
        hl_ref[...] = hc_ref[...]

    @pl.when(i >= 1)
    def _():
        for cp in _seq_copies(o_hbm, obuf_ref, osem_ref, (i - 1) % 2, i - 1, False):
            cp.start()

    @pl.when(i == n_tiles)
    def _():
        for tile in range(max(n_tiles - 2, 0), n_tiles):
            for cp in _seq_copies(o_hbm, obuf_ref, osem_ref, tile % 2, tile, False):
                cp.wait()


def _mix_weight_specs():
    return [_const_spec((1, D_MODEL)), _const_spec((1, D_MODEL)),
            _const_spec((D_MODEL, N_MIX_PARTS * D_CONV)),
            _const_spec((CONV_A_WIDTH, D_CONV)), _const_spec((CONV_B_WIDTH, D_LRU)), _const_spec((1, D_LRU)),
            _const_spec((N_MIX_CHUNKS, MIX_CHUNK, 2 * MIX_CHUNK)), _const_spec((1, D_LRU)), _const_spec((1, D_LRU)),
            _const_spec((1, D_LRU)), _const_spec((D_MODEL, D_MODEL))]


def _mix_prompt(x, weights, batch, seq):
    assert batch == SUBLANES and seq % MIX_BLOCK_STEPS == 0
    n_tiles = seq // MIX_BLOCK_STEPS
    n_sub = MIX_BLOCK_STEPS // MIX_STEPS
    hist_rows = (CONV_B_WIDTH - 1) * batch
    hbm_spec = pl.BlockSpec(memory_space=pl.ANY)
    hist_shape = jax.ShapeDtypeStruct((hist_rows, D_CONV), F32)
    h_shape = jax.ShapeDtypeStruct((batch, D_LRU), F32)
    return pl.pallas_call(
        functools.partial(_mix_prompt_kernel, n_tiles=n_tiles),
        grid=(n_tiles + 1,),
        in_specs=[hbm_spec] + _mix_weight_specs(),
        out_specs=[hbm_spec, _const_spec(hist_shape.shape), _const_spec(hist_shape.shape), _const_spec(h_shape.shape)],
        out_shape=[jax.ShapeDtypeStruct(x.shape, F32), hist_shape, hist_shape, h_shape],
        scratch_shapes=[pltpu.VMEM(hist_shape.shape, F32), pltpu.VMEM(hist_shape.shape, F32),
                        pltpu.VMEM(h_shape.shape, F32),
                        pltpu.VMEM((2, n_sub, N_MIX_CHUNKS, MIX_STEPS * batch, N_MIX_PARTS * MIX_CHUNK), F32),
                        pltpu.VMEM((3, MIX_BLOCK_STEPS, batch, D_MODEL), F32),
                        pltpu.VMEM((2, MIX_BLOCK_STEPS, batch, D_MODEL), F32),
                        pltpu.SemaphoreType.DMA((3, batch)), pltpu.SemaphoreType.DMA((2, batch)),
                        pltpu.VMEM((D_MODEL, N_MIX_PARTS * D_CONV), BF16),
                        pltpu.VMEM((N_MIX_CHUNKS, MIX_CHUNK, 2 * MIX_CHUNK), BF16),
                        pltpu.VMEM((D_MODEL, D_MODEL), BF16)],
        compiler_params=_params("arbitrary"),
        name="mix_prompt",
    )(x, *weights)


def _mix_sample_kernel(x_ref, sa_ref, sb_ref, h0_ref, gpre_ref, gpost_ref, win_ref, caw_ref, cbw_ref, cbb_ref,
                       wax_ref, ba_ref, bx_ref, lam_ref, wout_ref, o_ref, na_ref, nb_ref, hl_ref):
    x = x_ref[...]
    h = _rms(x, gpre_ref[...]).astype(BF16)
    sp = _softplus(-lam_ref[...])
    y = None
    for c in range(N_MIX_CHUNKS):
        csl = slice(c * MIX_CHUNK, (c + 1) * MIX_CHUNK)
        zc = _mix_project(h, win_ref, c)

        def state_a(k):
            o = (CONV_A_WIDTH - 1 - k) * D_CONV + c * MIX_CHUNK
            return sa_ref[:, o:o + MIX_CHUNK]

        def state_b(k):
            o = (CONV_B_WIDTH - 1 - k) * D_LRU + c * MIX_CHUNK
            return sb_ref[:, o:o + MIX_CHUNK]

        v, xb, ya, cb, gates, gg = _mix_front(zc, c, lambda cur, k: state_a(k), lambda cur, k: state_b(k),
                                              caw_ref, cbw_ref, cbb_ref, wax_ref)
        a, mult, i = _lru_coeffs(gates, c, ba_ref, bx_ref, sp)
        hnew = a * h0_ref[:, csl] + (mult * i) * cb
        yb = _gelu_tanh(gg) * hnew
        yc = _mix_out(ya, yb, wout_ref, c)
        y = yc if y is None else y + yc

        na_ref[:, c * MIX_CHUNK:(c + 1) * MIX_CHUNK] = state_a(1)
        na_ref[:, D_CONV + c * MIX_CHUNK:D_CONV + (c + 1) * MIX_CHUNK] = v
        nb_ref[:, c * MIX_CHUNK:(c + 1) * MIX_CHUNK] = state_b(2)
        nb_ref[:, D_LRU + c * MIX_CHUNK:D_LRU + (c + 1) * MIX_CHUNK] = state_b(1)
        nb_ref[:, 2 * D_LRU + c * MIX_CHUNK:2 * D_LRU + (c + 1) * MIX_CHUNK] = xb
        hl_ref[:, csl] = hnew
    o_ref[...] = x + _rms(y, gpost_ref[...])


def _mix_sample(x, sa, sb, h0, weights):
    full = lambda a: _const_spec(a.shape)
    return pl.pallas_call(
        _mix_sample_kernel,
        grid=(1,),
        in_specs=[full(x), full(sa), full(sb), full(h0)] + _mix_weight_specs(),
        out_specs=[full(x), full(sa), full(sb), full(h0)],
        out_shape=[jax.ShapeDtypeStruct(x.shape, F32), jax.ShapeDtypeStruct(sa.shape, F32),
                   jax.ShapeDtypeStruct(sb.shape, F32), jax.ShapeDtypeStruct(h0.shape, F32)],
        compiler_params=_params("arbitrary"),
        name="mix_sample",
    )(x, sa, sb, h0, *weights)


def _head_row_copies(buf_ref, hbm_ref, sem_ref, which, slot, step):
    copies = []
    for hd in range(N_XHEADS):
        for half in range(XHEAD_DIM // LANES):
            col = (hd * (XHEAD_DIM // LANES) + half) * LANES
            sub = half * N_XHEADS + hd
            src = buf_ref.at[slot, :, pl.ds(col, LANES)]
            dst = hbm_ref.at[pl.ds(step * KV_ROW_TILE, KV_ROW_TILE), sub, :]
            copies.append(pltpu.make_async_copy(src, dst, sem_ref.at[which, slot, sub]))
    return copies


def _kv_kernel(m_ref, g_ref, wk_ref, wv_ref, kb_ref, vb_ref, kt_hbm, vt_hbm, kbuf_ref, vbuf_ref, sem_ref, *, n_steps):
    i = pl.program_id(0)
    slot = i % 2
    m = _rms(m_ref[...], g_ref[...]).astype(BF16)
    k = _dot(m, wk_ref[...].astype(BF16))
    v = _dot(m, wv_ref[...].astype(BF16))
    kb_ref[...] = k.astype(BF16)
    vb_ref[...] = v.astype(BF16)
    kbuf_ref[slot] = k
    vbuf_ref[slot] = v

    def copies(s, step):
        return (_head_row_copies(kbuf_ref, kt_hbm, sem_ref, 0, s, step)
                + _head_row_copies(vbuf_ref, vt_hbm, sem_ref, 1, s, step))

    for cp in copies(slot, i):
        cp.start()

    @pl.when(i >= 1)
    def _():
        for cp in copies(1 - slot, i - 1):
            cp.wait()

    @pl.when(i == n_steps - 1)
    def _():
        for cp in copies(slot, i):
            cp.wait()


def _memory_kv(mem, g_mem, wk, wv):
    rows = mem.shape[0]
    n_steps = rows // KV_ROW_TILE
    row_spec = pl.BlockSpec((KV_ROW_TILE, D_MODEL), lambda i: (i, 0))
    hbm_spec = pl.BlockSpec(memory_space=pl.ANY)
    natural = jax.ShapeDtypeStruct((rows, D_MODEL), BF16)
    tiled = jax.ShapeDtypeStruct((rows, HEAD_ROWS, LANES), F32)
    buf = pltpu.VMEM((2, KV_ROW_TILE, D_MODEL), F32)
    return pl.pallas_call(
        functools.partial(_kv_kernel, n_steps=n_steps),
        grid=(n_steps,),
        in_specs=[row_spec, _const_spec((1, D_MODEL)), _const_spec((D_MODEL, D_MODEL)),
                  _const_spec((D_MODEL, D_MODEL))],
        out_specs=[row_spec, row_spec, hbm_spec, hbm_spec],
        out_shape=[natural, natural, tiled, tiled],
        scratch_shapes=[buf, buf, pltpu.SemaphoreType.DMA((2, 2, HEAD_ROWS))],
        compiler_params=_params("arbitrary"),
        name="memory_kv",
    )(mem, g_mem, wk, wv)


def _merge_heads(a, lead_shape):
    a = a.reshape(lead_shape + (XHEAD_DIM // LANES, N_XHEADS, LANES))
    return jnp.swapaxes(a, -3, -2).reshape(lead_shape + (N_XHEADS, XHEAD_DIM))


def _softmax_rows(s):
    e = jnp.exp(s - jnp.max(s, axis=-1, keepdims=True))
    return e * (1.0 / jnp.sum(e, axis=-1, keepdims=True))


def _xattn_prompt_kernel(x_ref, k_ref, v_ref, xs_ref, as_ref, gpre_ref, gpost_ref, wq_ref, wo_ref, o_ref, os_ref, *,
                         n_prompt):
    i = pl.program_id(0)

    @pl.when(i < n_prompt)
    def _():
        wq = wq_ref[...].astype(BF16)
        wo = wo_ref[...].astype(BF16)
        n_sub = x_ref.shape[0] // XATTN_SUB_ROWS
        subs = [slice(u * XATTN_SUB_ROWS, (u + 1) * XATTN_SUB_ROWS) for u in range(n_sub)]
        xs = [x_ref[r, :] for r in subs]
        qs = [_dot(_rms(x, gpre_ref[...]).astype(BF16), wq).astype(BF16) for x in xs]
        attn = []
        for q in qs:
            heads = []
            for hd in range(N_XHEADS):
                sl = slice(hd * XHEAD_DIM, (hd + 1) * XHEAD_DIM)
                s = lax.dot_general(q[:, sl], k_ref[0, :, sl], (((1,), (1,)), ((), ())), preferred_element_type=F32)
                p = _softmax_rows(s * (XHEAD_DIM ** -0.5)).astype(BF16)
                heads.append(_dot(p, v_ref[0, :, sl]).astype(BF16))
            attn.append(jnp.concatenate(heads, axis=-1))
        for r, x, a in zip(subs, xs, attn):
            o_ref[r, :] = x + _rms(_dot(a, wo), gpost_ref[...])

    @pl.when(i == n_prompt)
    def _():
        y = _dot(as_ref[...].astype(BF16), wo_ref[...].astype(BF16))
        os_ref[...] = xs_ref[...] + _rms(y, gpost_ref[...])


def _xattn_prompt(x, mem_k, mem_v, xs, attn_s, g_pre, g_post, wq, wo, batch, seq):
    n_t = seq // XATTN_ROW_TILE
    n_prompt = batch * n_t
    tile = lambda i: jnp.minimum(i, n_prompt - 1)
    row_spec = pl.BlockSpec((XATTN_ROW_TILE, D_MODEL), lambda i: (tile(i), 0))
    kv_spec = pl.BlockSpec((1, N_MEM, D_MODEL), lambda i: (tile(i) // n_t, 0, 0))
    return pl.pallas_call(
        functools.partial(_xattn_prompt_kernel, n_prompt=n_prompt),
        grid=(n_prompt + 1,),
        in_specs=[row_spec, kv_spec, kv_spec, _const_spec(xs.shape), _const_spec(xs.shape),
                  _const_spec((1, D_MODEL)), _const_spec((1, D_MODEL)),
                  _const_spec((D_MODEL, D_MODEL)), _const_spec((D_MODEL, D_MODEL))],
        out_specs=[row_spec, _const_spec(xs.shape)],
        out_shape=[jax.ShapeDtypeStruct(x.shape, F32), jax.ShapeDtypeStruct(xs.shape, F32)],
        compiler_params=_params("arbitrary"),
        name="xattn_prompt",
    )(x, mem_k, mem_v, xs, attn_s, g_pre, g_post, wq, wo)


def _q_sample_kernel(x_ref, gpre_ref, wq_ref, q_ref):
    q_ref[...] = _dot(_rms(x_ref[...], gpre_ref[...]).astype(BF16), wq_ref[...].astype(BF16))


def _split_heads(a):
    n = a.shape[0]
    a = a.reshape(n, -1, N_XHEADS, 2, LANES)
    return jnp.swapaxes(a, 2, 3).reshape(n, -1, LANES)


def _attend_sample_kernel(q_ref, k_ref, v_ref, ones_ref, o_ref):
    rows = N_MEM * HEAD_ROWS
    for b in range(q_ref.shape[0]):
        qv = q_ref[b] * (XHEAD_DIM ** -0.5)
        prod = k_ref[b].reshape(N_MEM, HEAD_ROWS, LANES) * qv[None]
        part = _dot(prod.reshape(rows, LANES).astype(BF16), ones_ref[...]).reshape(N_MEM, HEAD_ROWS, LANES)
        s = part + pltpu.roll(part, N_XHEADS, 1)
        e = jnp.exp(s - jnp.max(s, axis=0, keepdims=True))
        den = jnp.sum(e, axis=0)
        num = jnp.sum(e * v_ref[b].reshape(N_MEM, HEAD_ROWS, LANES), axis=0)
        o_ref[b] = num * (1.0 / den)


def _attend_sample(x, cache_k, cache_v, g_pre, wq):
    n = x.shape[0]
    full = lambda shape: _const_spec(shape)
    q = pl.pallas_call(
        _q_sample_kernel,
        grid=(1,),
        in_specs=[full(x.shape), full((1, D_MODEL)), full((D_MODEL, D_MODEL))],
        out_specs=full(x.shape),
        out_shape=jax.ShapeDtypeStruct(x.shape, F32),
        compiler_params=_params("arbitrary"),
        name="q_sample",
    )(x, g_pre, wq)
    nb = SAMPLE_XATTN_BLOCK
    rows = N_MEM * HEAD_ROWS
    q_spec = pl.BlockSpec((nb, HEAD_ROWS, LANES), lambda i: (i, 0, 0))
    kv_spec = pl.BlockSpec((nb, rows, LANES), lambda i: (i, 0, 0))
    o = pl.pallas_call(
        _attend_sample_kernel,
        grid=(n // nb,),
        in_specs=[q_spec, kv_spec, kv_spec, _const_spec((LANES, LANES))],
        out_specs=q_spec,
        out_shape=jax.ShapeDtypeStruct((n, HEAD_ROWS, LANES), F32),
        compiler_params=_params("parallel"),
        name="attend_sample",
    )(_split_heads(q), cache_k, cache_v, jnp.ones((LANES, LANES), BF16))
    return _merge_heads(o, (n,)).reshape(n, D_MODEL)


def _block_diag(w):
    groups, heads, d, _ = w.shape
    eye = jnp.eye(heads, dtype=w.dtype)
    return (eye[None, :, None, :, None] * w[:, :, :, None, :]).reshape(groups, heads * d, heads * d)


def _mix_weights(g_pre, g_post, w_in, conv_a_w, conv_b_w, conv_b_b, lru_wa, lru_ba, lru_wx, lru_bx, lru_lam, w_out):
    row = lambda p: p.reshape(1, -1)
    heads_per_chunk = MIX_CHUNK // LRU_HEAD_DIM
    grouped = lambda w: _block_diag(w.reshape(N_MIX_CHUNKS, heads_per_chunk, LRU_HEAD_DIM, LRU_HEAD_DIM))
    wax = jnp.concatenate([grouped(lru_wa), grouped(lru_wx)], axis=-1)
    return (row(g_pre), row(g_post), w_in, conv_a_w, conv_b_w, row(conv_b_b), wax, row(lru_ba), row(lru_bx),
            row(lru_lam), w_out)


def kernel(x_prompt, x_sample, mem_prompt, cache_mem_k, cache_mem_v, state_conv_a, state_conv_b, state_lru, g_ffn1_pre, g_ffn1_post, ffn1_wg, ffn1_wu, ffn1_wd, g_mix_pre, g_mix_post, w_in, conv_a_w, conv_b_w, conv_b_b, lru_wa, lru_ba, lru_wx, lru_bx, lru_lam, w_out, g_xattn_pre, g_xattn_post, g_mem, xattn_wq, xattn_wk, xattn_wv, xattn_wo, g_ffn2_pre, g_ffn2_post, ffn2_wg, ffn2_wu, ffn2_wd):
    batch, seq, _ = x_prompt.shape
    n_s = x_sample.shape[0]
    depth = g_ffn1_pre.shape[0]
    assert depth == 1 and x_sample.shape[1] == 1
    l = 0
    row = lambda p: p[l].reshape(1, -1)

    yp = x_prompt.reshape(batch * seq, D_MODEL)
    ys = x_sample.reshape(n_s, D_MODEL)

    ffn1 = (row(g_ffn1_pre), row(g_ffn1_post), ffn1_wg[l], ffn1_wu[l], ffn1_wd[l])
    ffn2 = (row(g_ffn2_pre), row(g_ffn2_post), ffn2_wg[l], ffn2_wu[l], ffn2_wd[l])
    mix_w = _mix_weights(g_mix_pre[l], g_mix_post[l], w_in[l], conv_a_w[l], conv_b_w[l], conv_b_b[l], lru_wa[l],
                         lru_ba[l], lru_wx[l], lru_bx[l], lru_lam[l], w_out[l])

    mk_b, mv_b, mk, mv = _memory_kv(mem_prompt.reshape(batch * N_MEM, D_MODEL), row(g_mem), xattn_wk[l], xattn_wv[l])

    yp, ys = _ffn(yp, ys, *ffn1)

    yp, tail_a, tail_b, tail_h = _mix_prompt(yp.reshape(batch, seq, D_MODEL), mix_w, batch, seq)
    yp = yp.reshape(batch * seq, D_MODEL)
    tail_a = tail_a.reshape(CONV_B_WIDTH - 1, batch, D_CONV)[CONV_B_WIDTH - CONV_A_WIDTH:].transpose(1, 0, 2)
    tail_b = tail_b.reshape(CONV_B_WIDTH - 1, batch, D_LRU).transpose(1, 0, 2)
    ys, new_a, new_b, new_h = _mix_sample(ys, state_conv_a[l].reshape(n_s, 2 * D_CONV),
                                          state_conv_b[l].reshape(n_s, 3 * D_LRU), state_lru[l], mix_w)

    attn_s = _attend_sample(ys, _split_heads(cache_mem_k[l]), _split_heads(cache_mem_v[l]), row(g_xattn_pre),
                            xattn_wq[l])
    yp, ys = _xattn_prompt(yp, mk_b.reshape(batch, N_MEM, D_MODEL), mv_b.reshape(batch, N_MEM, D_MODEL), ys, attn_s,
                           row(g_xattn_pre), row(g_xattn_post), xattn_wq[l], xattn_wo[l], batch, seq)

    yp, ys = _ffn(yp, ys, *ffn2)

    kv_lead = (1, batch, N_MEM)
    return (yp.reshape(batch, seq, D_MODEL), ys.reshape(n_s, 1, D_MODEL),
            _merge_heads(mk, kv_lead), _merge_heads(mv, kv_lead),
            tail_a[None], tail_b[None], tail_h[None],
            new_a.reshape(1, n_s, 2, D_CONV), new_b.reshape(1, n_s, 3, D_LRU), new_h[None])
```

```python
import functools
import math

import jax
import jax.numpy as jnp
from jax import lax
from jax.experimental import pallas as pl
from jax.experimental.pallas import tpu as pltpu

D_MODEL = 1024
D_CONV = 512
D_LRU = 512
N_LRU_HEADS = 8
LRU_HEAD_DIM = D_LRU // N_LRU_HEADS
LRU_C = 8.0
CONV_A_WIDTH = 3
CONV_B_WIDTH = 4
N_MIX_PARTS = 5
FFN_DIM = 2816
N_MEM = 256
N_XHEADS = 4
XHEAD_DIM = D_MODEL // N_XHEADS
RMS_EPS = 1e-6

F32 = jnp.float32
BF16 = jnp.bfloat16

V7X_VMEM_LIMIT_BYTES = 56 * 1024 * 1024
SUBLANES = 8
LANES = 128
HEAD_ROWS = N_XHEADS * XHEAD_DIM // LANES

FFN_ROW_TILE = 512
FFN_COL_CHUNK = 256
MIX_STEPS = 32
MIX_BLOCK_STEPS = 64
MIX_CHUNK = 256
N_MIX_CHUNKS = D_CONV // MIX_CHUNK
XATTN_ROW_TILE = 1024
XATTN_SUB_ROWS = 512
KV_ROW_TILE = 512
SAMPLE_XATTN_BLOCK = 8


def _rms(x, g):
    y = x * lax.rsqrt(jnp.mean(x * x, axis=-1, keepdims=True) + RMS_EPS)
    return y * g


def _dot(a, b):
    return jnp.dot(a, b, preferred_element_type=F32)


def _sigmoid(x):
    return 1.0 / (1.0 + jnp.exp(-x))


def _gelu_tanh(x):
    c = math.sqrt(2.0 / math.pi)
    neg_two_z = x * ((-2.0 * c * 0.044715) * (x * x) - 2.0 * c)
    return x / (1.0 + jnp.exp(neg_two_z))


def _one_minus_exp2x(x):
    t = jnp.tanh(x)
    return (-2.0 * t) / (1.0 - t)


def _log1p(w):
    u = 1.0 + w
    return jnp.where(u == 1.0, w, jnp.log(u) * w / (u - 1.0))


def _softplus(x):
    return jnp.maximum(x, 0.0) + _log1p(jnp.exp(-jnp.abs(x)))


def _const_spec(shape):
    zeros = (0,) * len(shape)
    return pl.BlockSpec(shape, lambda *_: zeros, pipeline_mode=pl.Buffered(1))


def _params(*sem):
    return pltpu.CompilerParams(dimension_semantics=sem, vmem_limit_bytes=V7X_VMEM_LIMIT_BYTES)


def _lane_tiles_to_rows(ref):
    return jnp.concatenate([ref[:, j, :] for j in range(ref.shape[1])], axis=-1)


def _rows_to_lane_tiles(ref, val):
    for j in range(ref.shape[1]):
        ref[:, j, :] = val[:, j * LANES:(j + 1) * LANES]


def _ffn_rows(x, gpre_ref, gpost_ref, wg_ref, wu_ref, wd_ref, act_ref):
    rows = x.shape[0]
    h = _rms(x, gpre_ref[...]).astype(BF16)
    for c in range(FFN_DIM // FFN_COL_CHUNK):
        sl = slice(c * FFN_COL_CHUNK, (c + 1) * FFN_COL_CHUNK)
        g = _dot(h, wg_ref[:, sl].astype(BF16))
        u = _dot(h, wu_ref[:, sl].astype(BF16))
        act_ref[0:rows, sl] = ((g * _sigmoid(g)) * u).astype(BF16)
    y = _dot(act_ref[0:rows, :], wd_ref[...].astype(BF16))
    return x + 0.5 * _rms(y, gpost_ref[...])


def _ffn_kernel(xp_ref, xs_ref, gpre_ref, gpost_ref, wg_ref, wu_ref, wd_ref, op_ref, os_ref, act_ref, *, n_prompt):
    i = pl.program_id(0)
    weights = (gpre_ref, gpost_ref, wg_ref, wu_ref, wd_ref, act_ref)

    @pl.when(i < n_prompt)
    def _():
        op_ref[...] = _ffn_rows(xp_ref[...], *weights)

    @pl.when(i == n_prompt)
    def _():
        xs = _lane_tiles_to_rows(xs_ref) if len(xs_ref.shape) == 3 else xs_ref[...]
        out = _ffn_rows(xs, *weights)
        if len(os_ref.shape) == 3:
            _rows_to_lane_tiles(os_ref, out)
        else:
            os_ref[...] = out


def _ffn(xp, xs, g_pre, g_post, wg, wu, wd, sample_out_tiles):
    n_prompt = xp.shape[0] // FFN_ROW_TILE
    n_s = xs.shape[0]
    out_s = (n_s, D_MODEL // LANES, LANES) if sample_out_tiles else (n_s, D_MODEL)
    prompt_spec = pl.BlockSpec((FFN_ROW_TILE, D_MODEL), lambda i: (jnp.minimum(i, n_prompt - 1), 0))
    return pl.pallas_call(
        functools.partial(_ffn_kernel, n_prompt=n_prompt),
        grid=(n_prompt + 1,),
        in_specs=[prompt_spec, _const_spec(xs.shape), _const_spec((1, D_MODEL)), _const_spec((1, D_MODEL)),
                  _const_spec((D_MODEL, FFN_DIM)), _const_spec((D_MODEL, FFN_DIM)),
                  _const_spec((FFN_DIM, D_MODEL))],
        out_specs=[prompt_spec, _const_spec(out_s)],
        out_shape=[jax.ShapeDtypeStruct(xp.shape, F32), jax.ShapeDtypeStruct(out_s, F32)],
        scratch_shapes=[pltpu.VMEM((FFN_ROW_TILE, FFN_DIM), BF16)],
        compiler_params=_params("arbitrary"),
        name="ffn",
    )(xp, xs, g_pre, g_post, wg, wu, wd)


def _chunk(ref, r, c):
    return ref[r:r + 1, c * MIX_CHUNK:(c + 1) * MIX_CHUNK]


def _mix_project(h, win_ref, c):
    parts = []
    for s in range(N_MIX_PARTS):
        o = s * D_CONV + c * MIX_CHUNK
        parts.append(_dot(h, win_ref[:, o:o + MIX_CHUNK].astype(BF16)))
    return jnp.concatenate(parts, axis=-1)


def _mix_out(ya, yb, wout_ref, c):
    ra = c * MIX_CHUNK
    rb = D_CONV + c * MIX_CHUNK
    return (_dot(ya.astype(BF16), wout_ref[ra:ra + MIX_CHUNK, :].astype(BF16))
            + _dot(yb.astype(BF16), wout_ref[rb:rb + MIX_CHUNK, :].astype(BF16)))


def _mix_front(zc, c, prev_a, prev_b, caw_ref, cbw_ref, cbb_ref, wax_ref):
    gb, gc, xa, xb, gg = (zc[:, s * MIX_CHUNK:(s + 1) * MIX_CHUNK] for s in range(N_MIX_PARTS))
    v = gc * xa
    ca = prev_a(v, 2) * _chunk(caw_ref, 0, c) + prev_a(v, 1) * _chunk(caw_ref, 1, c) + v * _chunk(caw_ref, 2, c)
    ya = gb * ca
    cb = (prev_b(xb, 3) * _chunk(cbw_ref, 0, c) + prev_b(xb, 2) * _chunk(cbw_ref, 1, c)
          + prev_b(xb, 1) * _chunk(cbw_ref, 2, c) + xb * _chunk(cbw_ref, 3, c))
    cb = cb + _chunk(cbb_ref, 0, c)
    gates = _dot(cb.astype(BF16), wax_ref[c].astype(BF16))
    return v, xb, ya, cb, gates, gg


def _lru_coeffs(gates, c, ba_ref, bx_ref, softplus_neg_lam):
    r = _sigmoid(gates[:, :MIX_CHUNK] + _chunk(ba_ref, 0, c))
    i = _sigmoid(gates[:, MIX_CHUNK:] + _chunk(bx_ref, 0, c))
    log_a = (-LRU_C * r) * softplus_neg_lam[:, c * MIX_CHUNK:(c + 1) * MIX_CHUNK]
    a = jnp.exp(log_a)
    mult = jnp.sqrt(_one_minus_exp2x(log_a))
    return a, mult, i


def _seq_copies(hbm_ref, buf_ref, sem_ref, slot, step_block, to_vmem):
    copies = []
    for b in range(SUBLANES):
        hbm = hbm_ref.at[b, pl.ds(step_block * MIX_BLOCK_STEPS, MIX_BLOCK_STEPS), :]
        vmem = buf_ref.at[slot, :, b, :]
        src, dst = (hbm, vmem) if to_vmem else (vmem, hbm)
        copies.append(pltpu.make_async_copy(src, dst, sem_ref.at[slot, b]))
    return copies


def _mix_kernel(x_hbm, xs_ref, sa_ref, sb_ref, h0_ref, gpre_ref, gpost_ref, win_ref, caw_ref, cbw_ref, cbb_ref,
                wa_ref, wx_ref, ba_ref, bx_ref, lam_ref, wout_ref, gq_ref, wq_ref,
                o_hbm, ta_ref, tb_ref, hl_ref, os_ref, na_ref, nb_ref, hs_ref, qs_ref,
                hista_ref, histb_ref, hc_ref, z_ref, xbuf_ref, obuf_ref, xsem_ref, osem_ref, winb_ref, waxb_ref,
                woutb_ref, *, n_tiles):
    i = pl.program_id(0)
    nb = SUBLANES
    rows = MIX_STEPS * nb
    hist_rows = (CONV_B_WIDTH - 1) * nb
    items = [(u, c) for u in range(MIX_BLOCK_STEPS // MIX_STEPS) for c in range(N_MIX_CHUNKS)]

    @pl.when(i == 0)
    def _():
        hista_ref[...] = jnp.zeros_like(hista_ref)
        histb_ref[...] = jnp.zeros_like(histb_ref)
        hc_ref[...] = jnp.zeros_like(hc_ref)
        for cp in _seq_copies(x_hbm, xbuf_ref, xsem_ref, 0, 0, True):
            cp.start()
        winb_ref[...] = win_ref[...].astype(BF16)
        woutb_ref[...] = wout_ref[...].astype(BF16)
        waxb_ref[...] = jnp.zeros_like(waxb_ref)
        heads_per_chunk = MIX_CHUNK // LRU_HEAD_DIM
        for hd in range(N_LRU_HEADS):
            c, j = divmod(hd, heads_per_chunk)
            rsl = slice(j * LRU_HEAD_DIM, (j + 1) * LRU_HEAD_DIM)
            waxb_ref[c, rsl, rsl] = wa_ref[hd].astype(BF16)
            waxb_ref[c, rsl, MIX_CHUNK + j * LRU_HEAD_DIM:MIX_CHUNK + (j + 1) * LRU_HEAD_DIM] = wx_ref[hd].astype(BF16)

    @pl.when(i + 1 < n_tiles)
    def _():
        for cp in _seq_copies(x_hbm, xbuf_ref, xsem_ref, (i + 1) % 3, i + 1, True):
            cp.start()

    @pl.when(i < n_tiles)
    def _():
        for cp in _seq_copies(x_hbm, xbuf_ref, xsem_ref, i % 3, i, True):
            cp.wait()

    @pl.when((i >= 3) & (i <= n_tiles))
    def _():
        for cp in _seq_copies(o_hbm, obuf_ref, osem_ref, (i - 3) % 2, i - 3, False):
            cp.wait()

    def prev(hist):
        return lambda cur, k: jnp.concatenate([hist[hist_rows - k * nb:, :], cur[:rows - k * nb, :]], axis=0)

    def x_rows(tile, u):
        return xbuf_ref[tile % 3, u * MIX_STEPS:(u + 1) * MIX_STEPS].reshape(rows, D_MODEL)

    def run(proj_tile, mix_tile):
        proj_tile, proj_slot = proj_tile if proj_tile is not None else (None, None)
        mix_tile, mix_slot = mix_tile if mix_tile is not None else (None, None)
        normed = {}
        if mix_tile is not None:
            sp = _softplus(-lam_ref[...])
            row = lax.broadcasted_iota(jnp.int32, (rows, MIX_CHUNK), 0)
            first_rows = jnp.where(mix_tile == 0, nb, 0)
            hist_a = hista_ref[...]
            hist_b = histb_ref[...]
            hcur = hc_ref[...]

        def project(u, c):
            if u not in normed:
                normed[u] = _rms(x_rows(proj_tile, u), gpre_ref[...]).astype(BF16)
            z_ref[proj_slot, u, c] = _mix_project(normed[u], winb_ref, c)

        for u, c in items:
            if mix_tile is None:
                project(u, c)
                continue
            if c == 0:
                y = None
                tails_a, tails_b, h_last = [], [], []
            csl = slice(c * MIX_CHUNK, (c + 1) * MIX_CHUNK)
            zc = z_ref[mix_slot, u, c]
            v, xb, ya, cb, gates, gg = _mix_front(zc, c, prev(hist_a[:, csl]), prev(hist_b[:, csl]),
                                                  caw_ref, cbw_ref, cbb_ref, waxb_ref)
            if proj_tile is not None:
                project(u, c)
            a, mult, gate_in = _lru_coeffs(gates, c, ba_ref, bx_ref, sp)
            if u == 0:
                mult = jnp.where(row < first_rows, 1.0, mult)
            b = (mult * gate_in) * cb
            hc = hcur[:, csl]
            hs = []
            for j in range(MIX_STEPS):
                sl = slice(j * nb, (j + 1) * nb)
                hc = a[sl, :] * hc + b[sl, :]
                hs.append(hc)
            yb = _gelu_tanh(gg) * jnp.concatenate(hs, axis=0)
            yc = _mix_out(ya, yb, woutb_ref, c)
            y = yc if y is None else y + yc
            tails_a.append(v[rows - hist_rows:, :])
            tails_b.append(xb[rows - hist_rows:, :])
            h_last.append(hc)
            if c == N_MIX_CHUNKS - 1:
                hist_a = jnp.concatenate(tails_a, axis=-1)
                hist_b = jnp.concatenate(tails_b, axis=-1)
                hcur = jnp.concatenate(h_last, axis=-1)
                out = x_rows(mix_tile, u) + _rms(y, gpost_ref[...])
                obuf_ref[mix_tile % 2, u * MIX_STEPS:(u + 1) * MIX_STEPS] = out.reshape(MIX_STEPS, nb, D_MODEL)
        if mix_tile is not None:
            hista_ref[...] = hist_a
            histb_ref[...] = hist_b
            hc_ref[...] = hcur

    @pl.when(i == 0)
    def _():
        run((i, 0), None)

    for parity in range(2):
        @pl.when((i > 0) & (i < n_tiles) & (i % 2 == parity))
        def _():
            run((i, parity), (i - 1, 1 - parity))

    @pl.when(i == n_tiles)
    def _():
        run(None, (i - 1, (n_tiles - 1) % 2))
        ta_ref[...] = hista_ref[...]
        tb_ref[...] = histb_ref[...]
        hl_ref[...] = hc_ref[...]

    @pl.when(i == n_tiles + 1)
    def _():
        _mix_sample_rows(xs_ref, sa_ref, sb_ref, h0_ref, gpre_ref, gpost_ref, winb_ref, caw_ref, cbw_ref, cbb_ref,
                         waxb_ref, ba_ref, bx_ref, lam_ref, woutb_ref, gq_ref, wq_ref,
                         os_ref, na_ref, nb_ref, hs_ref, qs_ref)

    @pl.when((i >= 1) & (i <= n_tiles))
    def _():
        for cp in _seq_copies(o_hbm, obuf_ref, osem_ref, (i - 1) % 2, i - 1, False):
            cp.start()

    @pl.when(i == n_tiles)
    def _():
        for tile in range(max(n_tiles - 2, 0), n_tiles):
            for cp in _seq_copies(o_hbm, obuf_ref, osem_ref, tile % 2, tile, False):
                cp.wait()


def _mix_sample_rows(x_ref, sa_ref, sb_ref, h0_ref, gpre_ref, gpost_ref, win_ref, caw_ref, cbw_ref, cbb_ref,
                     wax_ref, ba_ref, bx_ref, lam_ref, wout_ref, gq_ref, wq_ref,
                     o_ref, na_ref, nb_ref, hl_ref, q_ref):
    tiles_per_chunk = MIX_CHUNK // LANES
    n_a = CONV_A_WIDTH - 1
    x = x_ref[...]
    h = _rms(x, gpre_ref[...]).astype(BF16)
    sp = _softplus(-lam_ref[...])
    y = None
    for c in range(N_MIX_CHUNKS):
        csl = slice(c * MIX_CHUNK, (c + 1) * MIX_CHUNK)
        lane_tiles = range(c * tiles_per_chunk, (c + 1) * tiles_per_chunk)
        zc = _mix_project(h, win_ref, c)

        def state_a(k):
            return jnp.concatenate([sa_ref[:, j * n_a + (n_a - k), :] for j in lane_tiles], axis=-1)

        def state_b(k):
            return sb_ref[CONV_B_WIDTH - 1 - k, :, csl]

        v, xb, ya, cb, gates, gg = _mix_front(zc, c, lambda cur, k: state_a(k), lambda cur, k: state_b(k),
                                              caw_ref, cbw_ref, cbb_ref, wax_ref)
        a, mult, gate_in = _lru_coeffs(gates, c, ba_ref, bx_ref, sp)
        hnew = a * h0_ref[:, csl] + (mult * gate_in) * cb
        yb = _gelu_tanh(gg) * hnew
        yc = _mix_out(ya, yb, wout_ref, c)
        y = yc if y is None else y + yc

        for t, j in enumerate(lane_tiles):
            for age in range(n_a - 1):
                na_ref[:, j * n_a + age, :] = sa_ref[:, j * n_a + age + 1, :]
            na_ref[:, j * n_a + n_a - 1, :] = v[:, t * LANES:(t + 1) * LANES]
        for age in range(CONV_B_WIDTH - 2):
            nb_ref[age, :, csl] = sb_ref[age + 1, :, csl]
        nb_ref[CONV_B_WIDTH - 2, :, csl] = xb
        hl_ref[:, csl] = hnew
    out = x + _rms(y, gpost_ref[...])
    o_ref[...] = out
    q = _dot(_rms(out, gq_ref[...]).astype(BF16), wq_ref[...].astype(BF16))
    for hd in range(N_XHEADS):
        for half in range(XHEAD_DIM // LANES):
            col = hd * XHEAD_DIM + half * LANES
            q_ref[:, half * N_XHEADS + hd, :] = q[:, col:col + LANES]


def _mix(x, xs, sa, sb, h0, weights, g_q, wq, batch, seq):
    assert batch == SUBLANES and seq % MIX_BLOCK_STEPS == 0
    n_tiles = seq // MIX_BLOCK_STEPS
    n_sub = MIX_BLOCK_STEPS // MIX_STEPS
    n_s = xs.shape[0]
    hist_rows = (CONV_B_WIDTH - 1) * batch
    hbm_spec = pl.BlockSpec(memory_space=pl.ANY)
    hist_shape = jax.ShapeDtypeStruct((hist_rows, D_CONV), F32)
    h_shape = jax.ShapeDtypeStruct((batch, D_LRU), F32)
    q_shape = jax.ShapeDtypeStruct((n_s, HEAD_ROWS, LANES), F32)
    like = lambda a: jax.ShapeDtypeStruct(a.shape, F32)
    full = lambda a: _const_spec(a.shape)
    weight_specs = [_const_spec((1, D_MODEL)), _const_spec((1, D_MODEL)), _const_spec((D_MODEL, N_MIX_PARTS * D_CONV)),
                    _const_spec((CONV_A_WIDTH, D_CONV)), _const_spec((CONV_B_WIDTH, D_LRU)), _const_spec((1, D_LRU)),
                    _const_spec((N_LRU_HEADS, LRU_HEAD_DIM, LRU_HEAD_DIM)),
                    _const_spec((N_LRU_HEADS, LRU_HEAD_DIM, LRU_HEAD_DIM)),
                    _const_spec((1, D_LRU)), _const_spec((1, D_LRU)), _const_spec((1, D_LRU)),
                    _const_spec((D_MODEL, D_MODEL)), _const_spec((1, D_MODEL)), _const_spec((D_MODEL, D_MODEL))]
    return pl.pallas_call(
        functools.partial(_mix_kernel, n_tiles=n_tiles),
        grid=(n_tiles + 2,),
        in_specs=[hbm_spec, full(xs), full(sa), full(sb), full(h0)] + weight_specs,
        out_specs=[hbm_spec, full(hist_shape), full(hist_shape), full(h_shape),
                   full(xs), full(sa), full(sb), full(h0), full(q_shape)],
        out_shape=[like(x), hist_shape, hist_shape, h_shape, like(xs), like(sa), like(sb), like(h0), q_shape],
        scratch_shapes=[pltpu.VMEM(hist_shape.shape, F32), pltpu.VMEM(hist_shape.shape, F32),
                        pltpu.VMEM(h_shape.shape, F32),
                        pltpu.VMEM((2, n_sub, N_MIX_CHUNKS, MIX_STEPS * batch, N_MIX_PARTS * MIX_CHUNK), F32),
                        pltpu.VMEM((3, MIX_BLOCK_STEPS, batch, D_MODEL), F32),
                        pltpu.VMEM((2, MIX_BLOCK_STEPS, batch, D_MODEL), F32),
                        pltpu.SemaphoreType.DMA((3, batch)), pltpu.SemaphoreType.DMA((2, batch)),
                        pltpu.VMEM((D_MODEL, N_MIX_PARTS * D_CONV), BF16),
                        pltpu.VMEM((N_MIX_CHUNKS, MIX_CHUNK, 2 * MIX_CHUNK), BF16),
                        pltpu.VMEM((D_MODEL, D_MODEL), BF16)],
        compiler_params=_params("arbitrary"),
        name="mix",
    )(x, xs, sa, sb, h0, *weights, g_q, wq)


def _head_row_copies(buf_ref, hbm_ref, sem_ref, which, slot, step):
    copies = []
    for hd in range(N_XHEADS):
        for half in range(XHEAD_DIM // LANES):
            col = (hd * (XHEAD_DIM // LANES) + half) * LANES
            sub = half * N_XHEADS + hd
            src = buf_ref.at[slot, :, pl.ds(col, LANES)]
            dst = hbm_ref.at[pl.ds(step * KV_ROW_TILE, KV_ROW_TILE), sub, :]
            copies.append(pltpu.make_async_copy(src, dst, sem_ref.at[which, slot, sub]))
    return copies


def _kv_kernel(m_ref, g_ref, wk_ref, wv_ref, kb_ref, vb_ref, kt_hbm, vt_hbm, kbuf_ref, vbuf_ref, sem_ref, *, n_steps):
    i = pl.program_id(0)
    slot = i % 2
    m = _rms(m_ref[...], g_ref[...]).astype(BF16)
    k = _dot(m, wk_ref[...].astype(BF16))
    v = _dot(m, wv_ref[...].astype(BF16))
    kb_ref[...] = k.astype(BF16)
    vb_ref[...] = v.astype(BF16)
    kbuf_ref[slot] = k
    vbuf_ref[slot] = v

    def copies(s, step):
        return (_head_row_copies(kbuf_ref, kt_hbm, sem_ref, 0, s, step)
                + _head_row_copies(vbuf_ref, vt_hbm, sem_ref, 1, s, step))

    for cp in copies(slot, i):
        cp.start()

    @pl.when(i >= 1)
    def _():
        for cp in copies(1 - slot, i - 1):
            cp.wait()

    @pl.when(i == n_steps - 1)
    def _():
        for cp in copies(slot, i):
            cp.wait()


def _memory_kv(mem, g_mem, wk, wv):
    rows = mem.shape[0]
    n_steps = rows // KV_ROW_TILE
    row_spec = pl.BlockSpec((KV_ROW_TILE, D_MODEL), lambda i: (i, 0))
    hbm_spec = pl.BlockSpec(memory_space=pl.ANY)
    natural = jax.ShapeDtypeStruct((rows, D_MODEL), BF16)
    tiled = jax.ShapeDtypeStruct((rows, HEAD_ROWS, LANES), F32)
    buf = pltpu.VMEM((2, KV_ROW_TILE, D_MODEL), F32)
    return pl.pallas_call(
        functools.partial(_kv_kernel, n_steps=n_steps),
        grid=(n_steps,),
        in_specs=[row_spec, _const_spec((1, D_MODEL)), _const_spec((D_MODEL, D_MODEL)),
                  _const_spec((D_MODEL, D_MODEL))],
        out_specs=[row_spec, row_spec, hbm_spec, hbm_spec],
        out_shape=[natural, natural, tiled, tiled],
        scratch_shapes=[buf, buf, pltpu.SemaphoreType.DMA((2, 2, HEAD_ROWS))],
        compiler_params=_params("arbitrary"),
        name="memory_kv",
    )(mem, g_mem, wk, wv)


def _merge_heads(a, lead_shape):
    a = a.reshape(lead_shape + (XHEAD_DIM // LANES, N_XHEADS, LANES))
    return jnp.swapaxes(a, -3, -2).reshape(lead_shape + (N_XHEADS, XHEAD_DIM))


def _softmax_rows(s):
    e = jnp.exp(s - jnp.max(s, axis=-1, keepdims=True))
    return e * (1.0 / jnp.sum(e, axis=-1, keepdims=True))


def _xattn_prompt_kernel(x_ref, k_ref, v_ref, xs_ref, as_ref, gpre_ref, gpost_ref, wq_ref, wo_ref, o_ref, os_ref, *,
                         n_prompt):
    i = pl.program_id(0)

    @pl.when(i < n_prompt)
    def _():
        wq = wq_ref[...].astype(BF16)
        wo = wo_ref[...].astype(BF16)
        n_sub = x_ref.shape[0] // XATTN_SUB_ROWS
        subs = [slice(u * XATTN_SUB_ROWS, (u + 1) * XATTN_SUB_ROWS) for u in range(n_sub)]
        xs = [x_ref[r, :] for r in subs]
        qs = [_dot(_rms(x, gpre_ref[...]).astype(BF16), wq).astype(BF16) for x in xs]
        attn = []
        for q in qs:
            heads = []
            for hd in range(N_XHEADS):
                sl = slice(hd * XHEAD_DIM, (hd + 1) * XHEAD_DIM)
                s = lax.dot_general(q[:, sl], k_ref[0, :, sl], (((1,), (1,)), ((), ())), preferred_element_type=F32)
                p = _softmax_rows(s * (XHEAD_DIM ** -0.5)).astype(BF16)
                heads.append(_dot(p, v_ref[0, :, sl]).astype(BF16))
            attn.append(jnp.concatenate(heads, axis=-1))
        for r, x, a in zip(subs, xs, attn):
            o_ref[r, :] = x + _rms(_dot(a, wo), gpost_ref[...])

    @pl.when(i == n_prompt)
    def _():
        attn = jnp.concatenate([as_ref[:, half * N_XHEADS + hd, :] for hd in range(N_XHEADS)
                                for half in range(XHEAD_DIM // LANES)], axis=-1)
        y = _dot(attn.astype(BF16), wo_ref[...].astype(BF16))
        os_ref[...] = xs_ref[...] + _rms(y, gpost_ref[...])


def _xattn_prompt(x, mem_k, mem_v, xs, attn_s, g_pre, g_post, wq, wo, batch, seq):
    n_t = seq // XATTN_ROW_TILE
    n_prompt = batch * n_t
    tile = lambda i: jnp.minimum(i, n_prompt - 1)
    row_spec = pl.BlockSpec((XATTN_ROW_TILE, D_MODEL), lambda i: (tile(i), 0))
    kv_spec = pl.BlockSpec((1, N_MEM, D_MODEL), lambda i: (tile(i) // n_t, 0, 0))
    return pl.pallas_call(
        functools.partial(_xattn_prompt_kernel, n_prompt=n_prompt),
        grid=(n_prompt + 1,),
        in_specs=[row_spec, kv_spec, kv_spec, _const_spec(xs.shape), _const_spec(attn_s.shape),
                  _const_spec((1, D_MODEL)), _const_spec((1, D_MODEL)),
                  _const_spec((D_MODEL, D_MODEL)), _const_spec((D_MODEL, D_MODEL))],
        out_specs=[row_spec, _const_spec(xs.shape)],
        out_shape=[jax.ShapeDtypeStruct(x.shape, F32), jax.ShapeDtypeStruct(xs.shape, F32)],
        compiler_params=_params("arbitrary"),
        name="xattn_prompt",
    )(x, mem_k, mem_v, xs, attn_s, g_pre, g_post, wq, wo)


def _split_heads(a):
    n = a.shape[0]
    a = a.reshape(n, -1, N_XHEADS, 2, LANES)
    return jnp.swapaxes(a, 2, 3).reshape(n, -1, LANES)


def _attend_sample_kernel(q_ref, k_ref, v_ref, ones_ref, o_ref):
    rows = N_MEM * HEAD_ROWS
    for b in range(q_ref.shape[0]):
        qv = q_ref[b] * (XHEAD_DIM ** -0.5)
        prod = k_ref[b].reshape(N_MEM, HEAD_ROWS, LANES) * qv[None]
        part = _dot(prod.reshape(rows, LANES).astype(BF16), ones_ref[...]).reshape(N_MEM, HEAD_ROWS, LANES)
        s = part + pltpu.roll(part, N_XHEADS, 1)
        e = jnp.exp(s - jnp.max(s, axis=0, keepdims=True))
        den = jnp.sum(e, axis=0)
        num = jnp.sum(e * v_ref[b].reshape(N_MEM, HEAD_ROWS, LANES), axis=0)
        o_ref[b] = num * (1.0 / den)


def _attend_sample(q, cache_k, cache_v):
    n = q.shape[0]
    nb = SAMPLE_XATTN_BLOCK
    rows = N_MEM * HEAD_ROWS
    q_spec = pl.BlockSpec((nb, HEAD_ROWS, LANES), lambda i: (i, 0, 0))
    kv_spec = pl.BlockSpec((nb, rows, LANES), lambda i: (i, 0, 0))
    return pl.pallas_call(
        _attend_sample_kernel,
        grid=(n // nb,),
        in_specs=[q_spec, kv_spec, kv_spec, _const_spec((LANES, LANES))],
        out_specs=q_spec,
        out_shape=jax.ShapeDtypeStruct((n, HEAD_ROWS, LANES), F32),
        compiler_params=_params("parallel"),
        name="attend_sample",
    )(q, cache_k, cache_v, jnp.ones((LANES, LANES), BF16))


def kernel(x_prompt, x_sample, mem_prompt, cache_mem_k, cache_mem_v, state_conv_a, state_conv_b, state_lru, g_ffn1_pre, g_ffn1_post, ffn1_wg, ffn1_wu, ffn1_wd, g_mix_pre, g_mix_post, w_in, conv_a_w, conv_b_w, conv_b_b, lru_wa, lru_ba, lru_wx, lru_bx, lru_lam, w_out, g_xattn_pre, g_xattn_post, g_mem, xattn_wq, xattn_wk, xattn_wv, xattn_wo, g_ffn2_pre, g_ffn2_post, ffn2_wg, ffn2_wu, ffn2_wd):
    batch, seq, _ = x_prompt.shape
    n_s = x_sample.shape[0]
    depth = g_ffn1_pre.shape[0]
    assert depth == 1 and x_sample.shape[1] == 1
    l = 0
    row = lambda p: p[l].reshape(1, -1)

    yp = x_prompt.reshape(batch * seq, D_MODEL)
    ys = x_sample.reshape(n_s, D_MODEL // LANES, LANES)
    sa = state_conv_a[l].reshape(n_s, CONV_A_WIDTH - 1, D_CONV // LANES, LANES)
    sa = jnp.swapaxes(sa, 1, 2).reshape(n_s, (D_CONV // LANES) * (CONV_A_WIDTH - 1), LANES)
    sb = jnp.swapaxes(state_conv_b[l], 0, 1)

    ffn1 = (row(g_ffn1_pre), row(g_ffn1_post), ffn1_wg[l], ffn1_wu[l], ffn1_wd[l])
    ffn2 = (row(g_ffn2_pre), row(g_ffn2_post), ffn2_wg[l], ffn2_wu[l], ffn2_wd[l])
    mix_w = (row(g_mix_pre), row(g_mix_post), w_in[l], conv_a_w[l], conv_b_w[l], row(conv_b_b), lru_wa[l], lru_wx[l],
             row(lru_ba), row(lru_bx), row(lru_lam), w_out[l])

    mk_b, mv_b, mk, mv = _memory_kv(mem_prompt.reshape(batch * N_MEM, D_MODEL), row(g_mem), xattn_wk[l], xattn_wv[l])

    yp, ys = _ffn(yp, ys, *ffn1, sample_out_tiles=False)

    yp, tail_a, tail_b, tail_h, ys, new_a, new_b, new_h, q_s = _mix(
        yp.reshape(batch, seq, D_MODEL), ys, sa, sb, state_lru[l], mix_w, row(g_xattn_pre), xattn_wq[l], batch, seq)
    yp = yp.reshape(batch * seq, D_MODEL)
    tail_a = tail_a.reshape(CONV_B_WIDTH - 1, batch, D_CONV)[CONV_B_WIDTH - CONV_A_WIDTH:].transpose(1, 0, 2)
    tail_b = tail_b.reshape(CONV_B_WIDTH - 1, batch, D_LRU).transpose(1, 0, 2)
    new_a = jnp.swapaxes(new_a.reshape(n_s, D_CONV // LANES, CONV_A_WIDTH - 1, LANES), 1, 2)
    new_a = new_a.reshape(1, n_s, CONV_A_WIDTH - 1, D_CONV)
    new_b = jnp.swapaxes(new_b, 0, 1)[None]

    attn_s = _attend_sample(q_s, _split_heads(cache_mem_k[l]), _split_heads(cache_mem_v[l]))
    yp, ys = _xattn_prompt(yp, mk_b.reshape(batch, N_MEM, D_MODEL), mv_b.reshape(batch, N_MEM, D_MODEL), ys, attn_s,
                           row(g_xattn_pre), row(g_xattn_post), xattn_wq[l], xattn_wo[l], batch, seq)

    yp, ys = _ffn(yp, ys, *ffn2, sample_out_tiles=True)

    kv_lead = (1, batch, N_MEM)
    return (yp.reshape(batch, seq, D_MODEL), ys.reshape(n_s, 1, D_MODEL),
            _merge_heads(mk, kv_lead), _merge_heads(mv, kv_lead),
            tail_a[None], tail_b[None], tail_h[None], new_a, new_b, new_h[None])
```

```python
import functools
import math

import jax
import jax.numpy as jnp
from jax import lax
from jax.experimental import pallas as pl
from jax.experimental.pallas import tpu as pltpu

D_MODEL = 1024
D_CONV = 512
D_LRU = 512
N_LRU_HEADS = 8
LRU_HEAD_DIM = D_LRU // N_LRU_HEADS
LRU_C = 8.0
CONV_A_WIDTH = 3
CONV_B_WIDTH = 4
N_MIX_PARTS = 5
FFN_DIM = 2816
N_MEM = 256
N_XHEADS = 4
XHEAD_DIM = D_MODEL // N_XHEADS
RMS_EPS = 1e-6

F32 = jnp.float32
BF16 = jnp.bfloat16

V7X_VMEM_LIMIT_BYTES = 56 * 1024 * 1024
SUBLANES = 8
LANES = 128
HEAD_ROWS = N_XHEADS * XHEAD_DIM // LANES

FFN_ROW_TILE = 1024
FFN_COL_CHUNK = 256
FFN_CHUNKS = FFN_DIM // FFN_COL_CHUNK
MIX_STEPS = 32
MIX_BLOCK_STEPS = 64
MIX_CHUNK = 256
N_MIX_CHUNKS = D_CONV // MIX_CHUNK
XATTN_ROW_TILE = 1024
XATTN_SUB_ROWS = 512
KV_ROW_TILE = 512
SAMPLE_XATTN_BLOCK = 8


def _rms(x, g):
    y = x * lax.rsqrt(jnp.mean(x * x, axis=-1, keepdims=True) + RMS_EPS)
    return y * g


def _dot(a, b):
    return jnp.dot(a, b, preferred_element_type=F32)


def _sigmoid(x):
    return 1.0 / (1.0 + jnp.exp(-x))


def _gelu_tanh(x):
    c = math.sqrt(2.0 / math.pi)
    neg_two_z = x * ((-2.0 * c * 0.044715) * (x * x) - 2.0 * c)
    return x / (1.0 + jnp.exp(neg_two_z))


def _one_minus_exp2x(x):
    t = jnp.tanh(x)
    return (-2.0 * t) / (1.0 - t)


def _log1p(w):
    u = 1.0 + w
    return jnp.where(u == 1.0, w, jnp.log(u) * w / (u - 1.0))


def _softplus(x):
    return jnp.maximum(x, 0.0) + _log1p(jnp.exp(-jnp.abs(x)))


def _const_spec(shape):
    zeros = (0,) * len(shape)
    return pl.BlockSpec(shape, lambda *_: zeros, pipeline_mode=pl.Buffered(1))


def _params(*sem):
    return pltpu.CompilerParams(dimension_semantics=sem, vmem_limit_bytes=V7X_VMEM_LIMIT_BYTES)


def _lane_tiles_to_rows(ref):
    return jnp.concatenate([ref[:, j, :] for j in range(ref.shape[1])], axis=-1)


def _rows_to_lane_tiles(ref, val):
    for j in range(ref.shape[1]):
        ref[:, j, :] = val[:, j * LANES:(j + 1) * LANES]


def _ffn_up(h, wgb_ref, wub_ref, act_ref, c):
    rows = h.shape[0]
    sl = slice(c * FFN_COL_CHUNK, (c + 1) * FFN_COL_CHUNK)
    g = _dot(h, wgb_ref[:, sl])
    u = _dot(h, wub_ref[:, sl])
    act_ref[0:rows, sl] = ((g * _sigmoid(g)) * u).astype(BF16)


def _ffn_rows(x, gpre_ref, gpost_ref, wgb_ref, wub_ref, wdb_ref, act_ref, before_chunk=None, before_down=None):
    rows = x.shape[0]
    h = _rms(x, gpre_ref[...]).astype(BF16)
    for c in range(FFN_CHUNKS):
        if before_chunk is not None:
            before_chunk(c)
        _ffn_up(h, wgb_ref, wub_ref, act_ref, c)
    if before_down is not None:
        before_down()
    y = _dot(act_ref[0:rows, :], wdb_ref[...])
    return x + 0.5 * _rms(y, gpost_ref[...])


def _ffn_weight_copy(hbm_ref, stage_ref, sem_ref, which, c, by_columns):
    if by_columns:
        src = hbm_ref.at[:, pl.ds(c * FFN_COL_CHUNK, FFN_COL_CHUNK)]
    else:
        src = hbm_ref.at[pl.ds(c * FFN_COL_CHUNK, FFN_COL_CHUNK), :]
    return pltpu.make_async_copy(src, stage_ref.at[c % 2], sem_ref.at[which, c % 2])


def _ffn_kernel(xp_ref, xs_ref, gpre_ref, gpost_ref, wg_hbm, wu_hbm, wd_hbm, op_ref, os_ref,
                wgb_ref, wub_ref, wdb_ref, sg_ref, su_ref, sd_ref, sem_ref, act_ref, *, n_prompt):
    i = pl.program_id(0)
    weights = (gpre_ref, gpost_ref, wgb_ref, wub_ref, wdb_ref, act_ref)
    mats = ((wg_hbm, sg_ref, wgb_ref, True), (wu_hbm, su_ref, wub_ref, True), (wd_hbm, sd_ref, wdb_ref, False))

    def copy(m, c):
        hbm, stage, _, by_columns = mats[m]
        return _ffn_weight_copy(hbm, stage, sem_ref, m, c, by_columns)

    def land(m, c):
        _, stage, dst, by_columns = mats[m]
        copy(m, c).wait()
        sl = slice(c * FFN_COL_CHUNK, (c + 1) * FFN_COL_CHUNK)
        if by_columns:
            dst[:, sl] = stage[c % 2].astype(BF16)
        else:
            dst[sl, :] = stage[c % 2].astype(BF16)
        if c + 2 < FFN_CHUNKS:
            copy(m, c + 2).start()

    @pl.when(i == 0)
    def _():
        for c in range(2):
            for m in range(3):
                copy(m, c).start()

        def before_chunk(c):
            for m in range(3):
                land(m, c)

        op_ref[...] = _ffn_rows(xp_ref[...], *weights, before_chunk=before_chunk)

    @pl.when((i > 0) & (i < n_prompt))
    def _():
        op_ref[...] = _ffn_rows(xp_ref[...], *weights)

    @pl.when(i == n_prompt)
    def _():
        xs = _lane_tiles_to_rows(xs_ref) if len(xs_ref.shape) == 3 else xs_ref[...]
        out = _ffn_rows(xs, *weights)
        if len(os_ref.shape) == 3:
            _rows_to_lane_tiles(os_ref, out)
        else:
            os_ref[...] = out


def _ffn(xp, xs, g_pre, g_post, wg, wu, wd, sample_out_tiles):
    n_prompt = xp.shape[0] // FFN_ROW_TILE
    n_s = xs.shape[0]
    out_s = (n_s, D_MODEL // LANES, LANES) if sample_out_tiles else (n_s, D_MODEL)
    prompt_spec = pl.BlockSpec((FFN_ROW_TILE, D_MODEL), lambda i: (jnp.minimum(i, n_prompt - 1), 0))
    hbm_spec = pl.BlockSpec(memory_space=pl.ANY)
    return pl.pallas_call(
        functools.partial(_ffn_kernel, n_prompt=n_prompt),
        grid=(n_prompt + 1,),
        in_specs=[prompt_spec, _const_spec(xs.shape), _const_spec((1, D_MODEL)), _const_spec((1, D_MODEL)),
                  hbm_spec, hbm_spec, hbm_spec],
        out_specs=[prompt_spec, _const_spec(out_s)],
        out_shape=[jax.ShapeDtypeStruct(xp.shape, F32), jax.ShapeDtypeStruct(out_s, F32)],
        scratch_shapes=[pltpu.VMEM((D_MODEL, FFN_DIM), BF16), pltpu.VMEM((D_MODEL, FFN_DIM), BF16),
                        pltpu.VMEM((FFN_DIM, D_MODEL), BF16),
                        pltpu.VMEM((2, D_MODEL, FFN_COL_CHUNK), F32), pltpu.VMEM((2, D_MODEL, FFN_COL_CHUNK), F32),
                        pltpu.VMEM((2, FFN_COL_CHUNK, D_MODEL), F32),
                        pltpu.SemaphoreType.DMA((3, 2)),
                        pltpu.VMEM((FFN_ROW_TILE, FFN_DIM), BF16)],
        compiler_params=_params("arbitrary"),
        name="ffn",
    )(xp, xs, g_pre, g_post, wg, wu, wd)


def _chunk(ref, r, c):
    return ref[r:r + 1, c * MIX_CHUNK:(c + 1) * MIX_CHUNK]


def _mix_project(h, win_ref, c):
    parts = []
    for s in range(N_MIX_PARTS):
        o = s * D_CONV + c * MIX_CHUNK
        parts.append(_dot(h, win_ref[:, o:o + MIX_CHUNK].astype(BF16)))
    return jnp.concatenate(parts, axis=-1)


def _mix_out(ya, yb, wout_ref, c):
    ra = c * MIX_CHUNK
    rb = D_CONV + c * MIX_CHUNK
    return (_dot(ya.astype(BF16), wout_ref[ra:ra + MIX_CHUNK, :].astype(BF16))
            + _dot(yb.astype(BF16), wout_ref[rb:rb + MIX_CHUNK, :].astype(BF16)))


def _mix_front(zc, c, prev_a, prev_b, caw_ref, cbw_ref, cbb_ref, wax_ref):
    gb, gc, xa, xb, gg = (zc[:, s * MIX_CHUNK:(s + 1) * MIX_CHUNK] for s in range(N_MIX_PARTS))
    v = gc * xa
    ca = prev_a(v, 2) * _chunk(caw_ref, 0, c) + prev_a(v, 1) * _chunk(caw_ref, 1, c) + v * _chunk(caw_ref, 2, c)
    ya = gb * ca
    cb = (prev_b(xb, 3) * _chunk(cbw_ref, 0, c) + prev_b(xb, 2) * _chunk(cbw_ref, 1, c)
          + prev_b(xb, 1) * _chunk(cbw_ref, 2, c) + xb * _chunk(cbw_ref, 3, c))
    cb = cb + _chunk(cbb_ref, 0, c)
    gates = _dot(cb.astype(BF16), wax_ref[c].astype(BF16))
    return v, xb, ya, cb, gates, gg


def _lru_coeffs(gates, c, ba_ref, bx_ref, softplus_neg_lam):
    r = _sigmoid(gates[:, :MIX_CHUNK] + _chunk(ba_ref, 0, c))
    i = _sigmoid(gates[:, MIX_CHUNK:] + _chunk(bx_ref, 0, c))
    log_a = (-LRU_C * r) * softplus_neg_lam[:, c * MIX_CHUNK:(c + 1) * MIX_CHUNK]
    a = jnp.exp(log_a)
    mult = jnp.sqrt(_one_minus_exp2x(log_a))
    return a, mult, i


def _seq_copies(hbm_ref, buf_ref, sem_ref, slot, step_block, to_vmem):
    copies = []
    for b in range(SUBLANES):
        hbm = hbm_ref.at[b, pl.ds(step_block * MIX_BLOCK_STEPS, MIX_BLOCK_STEPS), :]
        vmem = buf_ref.at[slot, :, b, :]
        src, dst = (hbm, vmem) if to_vmem else (vmem, hbm)
        copies.append(pltpu.make_async_copy(src, dst, sem_ref.at[slot, b]))
    return copies


def _mix_kernel(x_hbm, xs_ref, sa_ref, sb_ref, h0_ref, gpre_ref, gpost_ref, win_ref, caw_ref, cbw_ref, cbb_ref,
                wa_ref, wx_ref, ba_ref, bx_ref, lam_ref, wout_ref, gq_ref, wq_ref,
                o_hbm, ta_ref, tb_ref, hl_ref, os_ref, na_ref, nb_ref, hs_ref, qs_ref,
                hista_ref, histb_ref, hc_ref, z_ref, xbuf_ref, obuf_ref, xsem_ref, osem_ref, winb_ref, waxb_ref,
                woutb_ref, *, n_tiles):
    i = pl.program_id(0)
    nb = SUBLANES
    rows = MIX_STEPS * nb
    hist_rows = (CONV_B_WIDTH - 1) * nb
    items = [(u, c) for u in range(MIX_BLOCK_STEPS // MIX_STEPS) for c in range(N_MIX_CHUNKS)]

    @pl.when(i == 0)
    def _():
        hista_ref[...] = jnp.zeros_like(hista_ref)
        histb_ref[...] = jnp.zeros_like(histb_ref)
        hc_ref[...] = jnp.zeros_like(hc_ref)
        for cp in _seq_copies(x_hbm, xbuf_ref, xsem_ref, 0, 0, True):
            cp.start()
        winb_ref[...] = win_ref[...].astype(BF16)
        woutb_ref[...] = wout_ref[...].astype(BF16)
        waxb_ref[...] = jnp.zeros_like(waxb_ref)
        heads_per_chunk = MIX_CHUNK // LRU_HEAD_DIM
        for hd in range(N_LRU_HEADS):
            c, j = divmod(hd, heads_per_chunk)
            rsl = slice(j * LRU_HEAD_DIM, (j + 1) * LRU_HEAD_DIM)
            waxb_ref[c, rsl, rsl] = wa_ref[hd].astype(BF16)
            waxb_ref[c, rsl, MIX_CHUNK + j * LRU_HEAD_DIM:MIX_CHUNK + (j + 1) * LRU_HEAD_DIM] = wx_ref[hd].astype(BF16)

    @pl.when(i + 1 < n_tiles)
    def _():
        for cp in _seq_copies(x_hbm, xbuf_ref, xsem_ref, (i + 1) % 3, i + 1, True):
            cp.start()

    @pl.when(i < n_tiles)
    def _():
        for cp in _seq_copies(x_hbm, xbuf_ref, xsem_ref, i % 3, i, True):
            cp.wait()

    @pl.when((i >= 3) & (i <= n_tiles))
    def _():
        for cp in _seq_copies(o_hbm, obuf_ref, osem_ref, (i - 3) % 2, i - 3, False):
            cp.wait()

    def prev(hist):
        return lambda cur, k: jnp.concatenate([hist[hist_rows - k * nb:, :], cur[:rows - k * nb, :]], axis=0)

    def x_rows(tile, u):
        return xbuf_ref[tile % 3, u * MIX_STEPS:(u + 1) * MIX_STEPS].reshape(rows, D_MODEL)

    def run(proj_tile, mix_tile):
        proj_tile, proj_slot = proj_tile if proj_tile is not None else (None, None)
        mix_tile, mix_slot = mix_tile if mix_tile is not None else (None, None)
        normed = {}
        if mix_tile is not None:
            sp = _softplus(-lam_ref[...])
            row = lax.broadcasted_iota(jnp.int32, (rows, MIX_CHUNK), 0)
            first_rows = jnp.where(mix_tile == 0, nb, 0)
            hist_a = hista_ref[...]
            hist_b = histb_ref[...]
            hcur = hc_ref[...]

        def project(u, c):
            if u not in normed:
                normed[u] = _rms(x_rows(proj_tile, u), gpre_ref[...]).astype(BF16)
            z_ref[proj_slot, u, c] = _mix_project(normed[u], winb_ref, c)

        for u, c in items:
            if mix_tile is None:
                project(u, c)
                continue
            if c == 0:
                y = None
                tails_a, tails_b, h_last = [], [], []
            csl = slice(c * MIX_CHUNK, (c + 1) * MIX_CHUNK)
            zc = z_ref[mix_slot, u, c]
            v, xb, ya, cb, gates, gg = _mix_front(zc, c, prev(hist_a[:, csl]), prev(hist_b[:, csl]),
                                                  caw_ref, cbw_ref, cbb_ref, waxb_ref)
            if proj_tile is not None:
                project(u, c)
            a, mult, gate_in = _lru_coeffs(gates, c, ba_ref, bx_ref, sp)
            if u == 0:
                mult = jnp.where(row < first_rows, 1.0, mult)
            b = (mult * gate_in) * cb
            hc = hcur[:, csl]
            hs = []
            for j in range(MIX_STEPS):
                sl = slice(j * nb, (j + 1) * nb)
                hc = a[sl, :] * hc + b[sl, :]
                hs.append(hc)
            yb = _gelu_tanh(gg) * jnp.concatenate(hs, axis=0)
            yc = _mix_out(ya, yb, woutb_ref, c)
            y = yc if y is None else y + yc
            tails_a.append(v[rows - hist_rows:, :])
            tails_b.append(xb[rows - hist_rows:, :])
            h_last.append(hc)
            if c == N_MIX_CHUNKS - 1:
                hist_a = jnp.concatenate(tails_a, axis=-1)
                hist_b = jnp.concatenate(tails_b, axis=-1)
                hcur = jnp.concatenate(h_last, axis=-1)
                out = x_rows(mix_tile, u) + _rms(y, gpost_ref[...])
                obuf_ref[mix_tile % 2, u * MIX_STEPS:(u + 1) * MIX_STEPS] = out.reshape(MIX_STEPS, nb, D_MODEL)
        if mix_tile is not None:
            hista_ref[...] = hist_a
            histb_ref[...] = hist_b
            hc_ref[...] = hcur

    @pl.when(i == 0)
    def _():
        run((i, 0), None)

    for parity in range(2):
        @pl.when((i > 0) & (i < n_tiles) & (i % 2 == parity))
        def _():
            run((i, parity), (i - 1, 1 - parity))

    @pl.when(i == n_tiles)
    def _():
        run(None, (i - 1, (n_tiles - 1) % 2))
        ta_ref[...] = hista_ref[...]
        tb_ref[...] = histb_ref[...]
        hl_ref[...] = hc_ref[...]

    @pl.when(i == n_tiles + 1)
    def _():
        _mix_sample_rows(xs_ref, sa_ref, sb_ref, h0_ref, gpre_ref, gpost_ref, winb_ref, caw_ref, cbw_ref, cbb_ref,
                         waxb_ref, ba_ref, bx_ref, lam_ref, woutb_ref, gq_ref, wq_ref,
                         os_ref, na_ref, nb_ref, hs_ref, qs_ref)

    @pl.when((i >= 1) & (i <= n_tiles))
    def _():
        for cp in _seq_copies(o_hbm, obuf_ref, osem_ref, (i - 1) % 2, i - 1, False):
            cp.start()

    @pl.when(i == n_tiles)
    def _():
        for tile in range(max(n_tiles - 2, 0), n_tiles):
            for cp in _seq_copies(o_hbm, obuf_ref, osem_ref, tile % 2, tile, False):
                cp.wait()


def _mix_sample_rows(x_ref, sa_ref, sb_ref, h0_ref, gpre_ref, gpost_ref, win_ref, caw_ref, cbw_ref, cbb_ref,
                     wax_ref, ba_ref, bx_ref, lam_ref, wout_ref, gq_ref, wq_ref,
                     o_ref, na_ref, nb_ref, hl_ref, q_ref):
    tiles_per_chunk = MIX_CHUNK // LANES
    n_a = CONV_A_WIDTH - 1
    x = x_ref[...]
    h = _rms(x, gpre_ref[...]).astype(BF16)
    sp = _softplus(-lam_ref[...])
    y = None
    for c in range(N_MIX_CHUNKS):
        csl = slice(c * MIX_CHUNK, (c + 1) * MIX_CHUNK)
        lane_tiles = range(c * tiles_per_chunk, (c + 1) * tiles_per_chunk)
        zc = _mix_project(h, win_ref, c)

        def state_a(k):
            return jnp.concatenate([sa_ref[:, j * n_a + (n_a - k), :] for j in lane_tiles], axis=-1)

        def state_b(k):
            return sb_ref[CONV_B_WIDTH - 1 - k, :, csl]

        v, xb, ya, cb, gates, gg = _mix_front(zc, c, lambda cur, k: state_a(k), lambda cur, k: state_b(k),
                                              caw_ref, cbw_ref, cbb_ref, wax_ref)
        a, mult, gate_in = _lru_coeffs(gates, c, ba_ref, bx_ref, sp)
        hnew = a * h0_ref[:, csl] + (mult * gate_in) * cb
        yb = _gelu_tanh(gg) * hnew
        yc = _mix_out(ya, yb, wout_ref, c)
        y = yc if y is None else y + yc

        for t, j in enumerate(lane_tiles):
            for age in range(n_a - 1):
                na_ref[:, j * n_a + age, :] = sa_ref[:, j * n_a + age + 1, :]
            na_ref[:, j * n_a + n_a - 1, :] = v[:, t * LANES:(t + 1) * LANES]
        for age in range(CONV_B_WIDTH - 2):
            nb_ref[age, :, csl] = sb_ref[age + 1, :, csl]
        nb_ref[CONV_B_WIDTH - 2, :, csl] = xb
        hl_ref[:, csl] = hnew
    out = x + _rms(y, gpost_ref[...])
    o_ref[...] = out
    q = _dot(_rms(out, gq_ref[...]).astype(BF16), wq_ref[...].astype(BF16))
    for hd in range(N_XHEADS):
        for half in range(XHEAD_DIM // LANES):
            col = hd * XHEAD_DIM + half * LANES
            q_ref[:, half * N_XHEADS + hd, :] = q[:, col:col + LANES]


def _mix(x, xs, sa, sb, h0, weights, g_q, wq, batch, seq):
    assert batch == SUBLANES and seq % MIX_BLOCK_STEPS == 0
    n_tiles = seq // MIX_BLOCK_STEPS
    n_sub = MIX_BLOCK_STEPS // MIX_STEPS
    n_s = xs.shape[0]
    hist_rows = (CONV_B_WIDTH - 1) * batch
    hbm_spec = pl.BlockSpec(memory_space=pl.ANY)
    hist_shape = jax.ShapeDtypeStruct((hist_rows, D_CONV), F32)
    h_shape = jax.ShapeDtypeStruct((batch, D_LRU), F32)
    q_shape = jax.ShapeDtypeStruct((n_s, HEAD_ROWS, LANES), F32)
    like = lambda a: jax.ShapeDtypeStruct(a.shape, F32)
    full = lambda a: _const_spec(a.shape)
    weight_specs = [_const_spec((1, D_MODEL)), _const_spec((1, D_MODEL)), _const_spec((D_MODEL, N_MIX_PARTS * D_CONV)),
                    _const_spec((CONV_A_WIDTH, D_CONV)), _const_spec((CONV_B_WIDTH, D_LRU)), _const_spec((1, D_LRU)),
                    _const_spec((N_LRU_HEADS, LRU_HEAD_DIM, LRU_HEAD_DIM)),
                    _const_spec((N_LRU_HEADS, LRU_HEAD_DIM, LRU_HEAD_DIM)),
                    _const_spec((1, D_LRU)), _const_spec((1, D_LRU)), _const_spec((1, D_LRU)),
                    _const_spec((D_MODEL, D_MODEL)), _const_spec((1, D_MODEL)), _const_spec((D_MODEL, D_MODEL))]
    return pl.pallas_call(
        functools.partial(_mix_kernel, n_tiles=n_tiles),
        grid=(n_tiles + 2,),
        in_specs=[hbm_spec, full(xs), full(sa), full(sb), full(h0)] + weight_specs,
        out_specs=[hbm_spec, full(hist_shape), full(hist_shape), full(h_shape),
                   full(xs), full(sa), full(sb), full(h0), full(q_shape)],
        out_shape=[like(x), hist_shape, hist_shape, h_shape, like(xs), like(sa), like(sb), like(h0), q_shape],
        scratch_shapes=[pltpu.VMEM(hist_shape.shape, F32), pltpu.VMEM(hist_shape.shape, F32),
                        pltpu.VMEM(h_shape.shape, F32),
                        pltpu.VMEM((2, n_sub, N_MIX_CHUNKS, MIX_STEPS * batch, N_MIX_PARTS * MIX_CHUNK), F32),
                        pltpu.VMEM((3, MIX_BLOCK_STEPS, batch, D_MODEL), F32),
                        pltpu.VMEM((2, MIX_BLOCK_STEPS, batch, D_MODEL), F32),
                        pltpu.SemaphoreType.DMA((3, batch)), pltpu.SemaphoreType.DMA((2, batch)),
                        pltpu.VMEM((D_MODEL, N_MIX_PARTS * D_CONV), BF16),
                        pltpu.VMEM((N_MIX_CHUNKS, MIX_CHUNK, 2 * MIX_CHUNK), BF16),
                        pltpu.VMEM((D_MODEL, D_MODEL), BF16)],
        compiler_params=_params("arbitrary"),
        name="mix",
    )(x, xs, sa, sb, h0, *weights, g_q, wq)


def _head_row_copies(buf_ref, hbm_ref, sem_ref, which, slot, step):
    copies = []
    for hd in range(N_XHEADS):
        for half in range(XHEAD_DIM // LANES):
            col = (hd * (XHEAD_DIM // LANES) + half) * LANES
            sub = half * N_XHEADS + hd
            src = buf_ref.at[slot, :, pl.ds(col, LANES)]
            dst = hbm_ref.at[pl.ds(step * KV_ROW_TILE, KV_ROW_TILE), sub, :]
            copies.append(pltpu.make_async_copy(src, dst, sem_ref.at[which, slot, sub]))
    return copies


def _kv_kernel(m_ref, g_ref, wk_ref, wv_ref, kb_ref, vb_ref, kt_hbm, vt_hbm, kbuf_ref, vbuf_ref, sem_ref, *, n_steps):
    i = pl.program_id(0)
    slot = i % 2
    m = _rms(m_ref[...], g_ref[...]).astype(BF16)
    k = _dot(m, wk_ref[...].astype(BF16))
    v = _dot(m, wv_ref[...].astype(BF16))
    kb_ref[...] = k.astype(BF16)
    vb_ref[...] = v.astype(BF16)
    kbuf_ref[slot] = k
    vbuf_ref[slot] = v

    def copies(s, step):
        return (_head_row_copies(kbuf_ref, kt_hbm, sem_ref, 0, s, step)
                + _head_row_copies(vbuf_ref, vt_hbm, sem_ref, 1, s, step))

    for cp in copies(slot, i):
        cp.start()

    @pl.when(i >= 1)
    def _():
        for cp in copies(1 - slot, i - 1):
            cp.wait()

    @pl.when(i == n_steps - 1)
    def _():
        for cp in copies(slot, i):
            cp.wait()


def _memory_kv(mem, g_mem, wk, wv):
    rows = mem.shape[0]
    n_steps = rows // KV_ROW_TILE
    row_spec = pl.BlockSpec((KV_ROW_TILE, D_MODEL), lambda i: (i, 0))
    hbm_spec = pl.BlockSpec(memory_space=pl.ANY)
    natural = jax.ShapeDtypeStruct((rows, D_MODEL), BF16)
    tiled = jax.ShapeDtypeStruct((rows, HEAD_ROWS, LANES), F32)
    buf = pltpu.VMEM((2, KV_ROW_TILE, D_MODEL), F32)
    return pl.pallas_call(
        functools.partial(_kv_kernel, n_steps=n_steps),
        grid=(n_steps,),
        in_specs=[row_spec, _const_spec((1, D_MODEL)), _const_spec((D_MODEL, D_MODEL)),
                  _const_spec((D_MODEL, D_MODEL))],
        out_specs=[row_spec, row_spec, hbm_spec, hbm_spec],
        out_shape=[natural, natural, tiled, tiled],
        scratch_shapes=[buf, buf, pltpu.SemaphoreType.DMA((2, 2, HEAD_ROWS))],
        compiler_params=_params("arbitrary"),
        name="memory_kv",
    )(mem, g_mem, wk, wv)


def _merge_heads(a, lead_shape):
    a = a.reshape(lead_shape + (XHEAD_DIM // LANES, N_XHEADS, LANES))
    return jnp.swapaxes(a, -3, -2).reshape(lead_shape + (N_XHEADS, XHEAD_DIM))


def _softmax_rows(s):
    e = jnp.exp(s - jnp.max(s, axis=-1, keepdims=True))
    return e * (1.0 / jnp.sum(e, axis=-1, keepdims=True))


def _xattn_prompt_kernel(x_ref, k_ref, v_ref, xs_ref, as_ref, gpre_ref, gpost_ref, wq_ref, wo_ref, o_ref, os_ref, *,
                         n_prompt):
    i = pl.program_id(0)

    @pl.when(i < n_prompt)
    def _():
        wq = wq_ref[...].astype(BF16)
        wo = wo_ref[...].astype(BF16)
        n_sub = x_ref.shape[0] // XATTN_SUB_ROWS
        subs = [slice(u * XATTN_SUB_ROWS, (u + 1) * XATTN_SUB_ROWS) for u in range(n_sub)]
        xs = [x_ref[r, :] for r in subs]
        qs = [_dot(_rms(x, gpre_ref[...]).astype(BF16), wq).astype(BF16) for x in xs]
        attn = []
        for q in qs:
            heads = []
            for hd in range(N_XHEADS):
                sl = slice(hd * XHEAD_DIM, (hd + 1) * XHEAD_DIM)
                s = lax.dot_general(q[:, sl], k_ref[0, :, sl], (((1,), (1,)), ((), ())), preferred_element_type=F32)
                p = _softmax_rows(s * (XHEAD_DIM ** -0.5)).astype(BF16)
                heads.append(_dot(p, v_ref[0, :, sl]).astype(BF16))
            attn.append(jnp.concatenate(heads, axis=-1))
        for r, x, a in zip(subs, xs, attn):
            o_ref[r, :] = x + _rms(_dot(a, wo), gpost_ref[...])

    @pl.when(i == n_prompt)
    def _():
        attn = jnp.concatenate([as_ref[:, half * N_XHEADS + hd, :] for hd in range(N_XHEADS)
                                for half in range(XHEAD_DIM // LANES)], axis=-1)
        y = _dot(attn.astype(BF16), wo_ref[...].astype(BF16))
        os_ref[...] = xs_ref[...] + _rms(y, gpost_ref[...])


def _xattn_prompt(x, mem_k, mem_v, xs, attn_s, g_pre, g_post, wq, wo, batch, seq):
    n_t = seq // XATTN_ROW_TILE
    n_prompt = batch * n_t
    tile = lambda i: jnp.minimum(i, n_prompt - 1)
    row_spec = pl.BlockSpec((XATTN_ROW_TILE, D_MODEL), lambda i: (tile(i), 0))
    kv_spec = pl.BlockSpec((1, N_MEM, D_MODEL), lambda i: (tile(i) // n_t, 0, 0))
    return pl.pallas_call(
        functools.partial(_xattn_prompt_kernel, n_prompt=n_prompt),
        grid=(n_prompt + 1,),
        in_specs=[row_spec, kv_spec, kv_spec, _const_spec(xs.shape), _const_spec(attn_s.shape),
                  _const_spec((1, D_MODEL)), _const_spec((1, D_MODEL)),
                  _const_spec((D_MODEL, D_MODEL)), _const_spec((D_MODEL, D_MODEL))],
        out_specs=[row_spec, _const_spec(xs.shape)],
        out_shape=[jax.ShapeDtypeStruct(x.shape, F32), jax.ShapeDtypeStruct(xs.shape, F32)],
        compiler_params=_params("arbitrary"),
        name="xattn_prompt",
    )(x, mem_k, mem_v, xs, attn_s, g_pre, g_post, wq, wo)


def _split_heads(a):
    n = a.shape[0]
    a = a.reshape(n, -1, N_XHEADS, 2, LANES)
    return jnp.swapaxes(a, 2, 3).reshape(n, -1, LANES)


def _attend_sample_kernel(q_ref, k_ref, v_ref, ones_ref, o_ref):
    rows = N_MEM * HEAD_ROWS
    for b in range(q_ref.shape[0]):
        qv = q_ref[b] * (XHEAD_DIM ** -0.5)
        prod = k_ref[b].reshape(N_MEM, HEAD_ROWS, LANES) * qv[None]
        part = _dot(prod.reshape(rows, LANES).astype(BF16), ones_ref[...]).reshape(N_MEM, HEAD_ROWS, LANES)
        s = part + pltpu.roll(part, N_XHEADS, 1)
        e = jnp.exp(s - jnp.max(s, axis=0, keepdims=True))
        den = jnp.sum(e, axis=0)
        num = jnp.sum(e * v_ref[b].reshape(N_MEM, HEAD_ROWS, LANES), axis=0)
        o_ref[b] = num * (1.0 / den)


def _attend_sample(q, cache_k, cache_v):
    n = q.shape[0]
    nb = SAMPLE_XATTN_BLOCK
    rows = N_MEM * HEAD_ROWS
    q_spec = pl.BlockSpec((nb, HEAD_ROWS, LANES), lambda i: (i, 0, 0))
    kv_spec = pl.BlockSpec((nb, rows, LANES), lambda i: (i, 0, 0))
    return pl.pallas_call(
        _attend_sample_kernel,
        grid=(n // nb,),
        in_specs=[q_spec, kv_spec, kv_spec, _const_spec((LANES, LANES))],
        out_specs=q_spec,
        out_shape=jax.ShapeDtypeStruct((n, HEAD_ROWS, LANES), F32),
        compiler_params=_params("parallel"),
        name="attend_sample",
    )(q, cache_k, cache_v, jnp.ones((LANES, LANES), BF16))


def kernel(x_prompt, x_sample, mem_prompt, cache_mem_k, cache_mem_v, state_conv_a, state_conv_b, state_lru, g_ffn1_pre, g_ffn1_post, ffn1_wg, ffn1_wu, ffn1_wd, g_mix_pre, g_mix_post, w_in, conv_a_w, conv_b_w, conv_b_b, lru_wa, lru_ba, lru_wx, lru_bx, lru_lam, w_out, g_xattn_pre, g_xattn_post, g_mem, xattn_wq, xattn_wk, xattn_wv, xattn_wo, g_ffn2_pre, g_ffn2_post, ffn2_wg, ffn2_wu, ffn2_wd):
    batch, seq, _ = x_prompt.shape
    n_s = x_sample.shape[0]
    depth = g_ffn1_pre.shape[0]
    assert depth == 1 and x_sample.shape[1] == 1
    l = 0
    row = lambda p: p[l].reshape(1, -1)

    yp = x_prompt.reshape(batch * seq, D_MODEL)
    ys = x_sample.reshape(n_s, D_MODEL // LANES, LANES)
    sa = state_conv_a[l].reshape(n_s, CONV_A_WIDTH - 1, D_CONV // LANES, LANES)
    sa = jnp.swapaxes(sa, 1, 2).reshape(n_s, (D_CONV // LANES) * (CONV_A_WIDTH - 1), LANES)
    sb = jnp.swapaxes(state_conv_b[l], 0, 1)

    ffn1 = (row(g_ffn1_pre), row(g_ffn1_post), ffn1_wg[l], ffn1_wu[l], ffn1_wd[l])
    ffn2 = (row(g_ffn2_pre), row(g_ffn2_post), ffn2_wg[l], ffn2_wu[l], ffn2_wd[l])
    mix_w = (row(g_mix_pre), row(g_mix_post), w_in[l], conv_a_w[l], conv_b_w[l], row(conv_b_b), lru_wa[l], lru_wx[l],
             row(lru_ba), row(lru_bx), row(lru_lam), w_out[l])

    mk_b, mv_b, mk, mv = _memory_kv(mem_prompt.reshape(batch * N_MEM, D_MODEL), row(g_mem), xattn_wk[l], xattn_wv[l])

    yp, ys = _ffn(yp, ys, *ffn1, sample_out_tiles=False)

    yp, tail_a, tail_b, tail_h, ys, new_a, new_b, new_h, q_s = _mix(
        yp.reshape(batch, seq, D_MODEL), ys, sa, sb, state_lru[l], mix_w, row(g_xattn_pre), xattn_wq[l], batch, seq)
    yp = yp.reshape(batch * seq, D_MODEL)
    tail_a = tail_a.reshape(CONV_B_WIDTH - 1, batch, D_CONV)[CONV_B_WIDTH - CONV_A_WIDTH:].transpose(1, 0, 2)
    tail_b = tail_b.reshape(CONV_B_WIDTH - 1, batch, D_LRU).transpose(1, 0, 2)
    new_a = jnp.swapaxes(new_a.reshape(n_s, D_CONV // LANES, CONV_A_WIDTH - 1, LANES), 1, 2)
    new_a = new_a.reshape(1, n_s, CONV_A_WIDTH - 1, D_CONV)
    new_b = jnp.swapaxes(new_b, 0, 1)[None]

    attn_s = _attend_sample(q_s, _split_heads(cache_mem_k[l]), _split_heads(cache_mem_v[l]))
    yp, ys = _xattn_prompt(yp, mk_b.reshape(batch, N_MEM, D_MODEL), mv_b.reshape(batch, N_MEM, D_MODEL), ys, attn_s,
                           row(g_xattn_pre), row(g_xattn_post), xattn_wq[l], xattn_wo[l], batch, seq)

    yp, ys = _ffn(yp, ys, *ffn2, sample_out_tiles=True)

    kv_lead = (1, batch, N_MEM)
    return (yp.reshape(batch, seq, D_MODEL), ys.reshape(n_s, 1, D_MODEL),
            _merge_heads(mk, kv_lead), _merge_heads(mv, kv_lead),
            tail_a[None], tail_b[None], tail_h[None], new_a, new_b, new_h[None])
```

```python
import functools
import math

import jax
import jax.numpy as jnp
from jax import lax
from jax.experimental import pallas as pl
from jax.experimental.pallas import tpu as pltpu

D_MODEL = 1024
D_CONV = 512
D_LRU = 512
N_LRU_HEADS = 8
LRU_HEAD_DIM = D_LRU // N_LRU_HEADS
LRU_C = 8.0
CONV_A_WIDTH = 3
CONV_B_WIDTH = 4
N_MIX_PARTS = 5
FFN_DIM = 2816
N_MEM = 256
N_XHEADS = 4
XHEAD_DIM = D_MODEL // N_XHEADS
RMS_EPS = 1e-6

F32 = jnp.float32
BF16 = jnp.bfloat16

V7X_VMEM_LIMIT_BYTES = 56 * 1024 * 1024
SUBLANES = 8
LANES = 128
HEAD_ROWS = N_XHEADS * XHEAD_DIM // LANES

FFN_ROW_TILE = 1024
FFN_COL_CHUNK = 256
FFN_CHUNKS = FFN_DIM // FFN_COL_CHUNK
MIX_STEPS = 32
MIX_BLOCK_STEPS = 64
MIX_CHUNK = 256
N_MIX_CHUNKS = D_CONV // MIX_CHUNK
XATTN_ROW_TILE = 1024
XATTN_SUB_ROWS = 512
KV_ROW_TILE = 512
XATTN_SAMPLES_PER_STEP = 8
XATTN_BANK_ROWS = 2


def _rms(x, g):
    y = x * lax.rsqrt(jnp.mean(x * x, axis=-1, keepdims=True) + RMS_EPS)
    return y * g


def _dot(a, b):
    return jnp.dot(a, b, preferred_element_type=F32)


def _sigmoid(x):
    return 1.0 / (1.0 + jnp.exp(-x))


def _gelu_tanh(x):
    c = math.sqrt(2.0 / math.pi)
    neg_two_z = x * ((-2.0 * c * 0.044715) * (x * x) - 2.0 * c)
    return x / (1.0 + jnp.exp(neg_two_z))


def _one_minus_exp2x(x):
    t = jnp.tanh(x)
    return (-2.0 * t) / (1.0 - t)


def _log1p(w):
    u = 1.0 + w
    return jnp.where(u == 1.0, w, jnp.log(u) * w / (u - 1.0))


def _softplus(x):
    return jnp.maximum(x, 0.0) + _log1p(jnp.exp(-jnp.abs(x)))


def _const_spec(shape):
    zeros = (0,) * len(shape)
    return pl.BlockSpec(shape, lambda *_: zeros, pipeline_mode=pl.Buffered(1))


def _params(*sem):
    return pltpu.CompilerParams(dimension_semantics=sem, vmem_limit_bytes=V7X_VMEM_LIMIT_BYTES)


def _lane_tiles_to_rows(ref):
    return jnp.concatenate([ref[:, j, :] for j in range(ref.shape[1])], axis=-1)


def _rows_to_lane_tiles(ref, val):
    for j in range(ref.shape[1]):
        ref[:, j, :] = val[:, j * LANES:(j + 1) * LANES]


def _ffn_up(h, wgb_ref, wub_ref, act_ref, c):
    rows = h.shape[0]
    sl = slice(c * FFN_COL_CHUNK, (c + 1) * FFN_COL_CHUNK)
    g = _dot(h, wgb_ref[:, sl])
    u = _dot(h, wub_ref[:, sl])
    act_ref[0:rows, sl] = ((g * _sigmoid(g)) * u).astype(BF16)


def _ffn_rows(x, gpre_ref, gpost_ref, wgb_ref, wub_ref, wdb_ref, act_ref, before_chunk=None, before_down=None):
    rows = x.shape[0]
    h = _rms(x, gpre_ref[...]).astype(BF16)
    for c in range(FFN_CHUNKS):
        if before_chunk is not None:
            before_chunk(c)
        _ffn_up(h, wgb_ref, wub_ref, act_ref, c)
    if before_down is not None:
        before_down()
    y = _dot(act_ref[0:rows, :], wdb_ref[...])
    return x + 0.5 * _rms(y, gpost_ref[...])


def _ffn_weight_copy(hbm_ref, stage_ref, sem_ref, which, c, by_columns):
    if by_columns:
        src = hbm_ref.at[:, pl.ds(c * FFN_COL_CHUNK, FFN_COL_CHUNK)]
    else:
        src = hbm_ref.at[pl.ds(c * FFN_COL_CHUNK, FFN_COL_CHUNK), :]
    return pltpu.make_async_copy(src, stage_ref.at[c % 2], sem_ref.at[which, c % 2])


def _ffn_kernel(xp_ref, xs_ref, gpre_ref, gpost_ref, wg_hbm, wu_hbm, wd_hbm, op_ref, os_ref,
                wgb_ref, wub_ref, wdb_ref, sg_ref, su_ref, sd_ref, sem_ref, act_ref, *, n_prompt):
    i = pl.program_id(0)
    weights = (gpre_ref, gpost_ref, wgb_ref, wub_ref, wdb_ref, act_ref)
    mats = ((wg_hbm, sg_ref, wgb_ref, True), (wu_hbm, su_ref, wub_ref, True), (wd_hbm, sd_ref, wdb_ref, False))

    def copy(m, c):
        hbm, stage, _, by_columns = mats[m]
        return _ffn_weight_copy(hbm, stage, sem_ref, m, c, by_columns)

    def land(m, c):
        _, stage, dst, by_columns = mats[m]
        copy(m, c).wait()
        sl = slice(c * FFN_COL_CHUNK, (c + 1) * FFN_COL_CHUNK)
        if by_columns:
            dst[:, sl] = stage[c % 2].astype(BF16)
        else:
            dst[sl, :] = stage[c % 2].astype(BF16)
        if c + 2 < FFN_CHUNKS:
            copy(m, c + 2).start()

    @pl.when(i == 0)
    def _():
        for c in range(2):
            for m in range(3):
                copy(m, c).start()

        def before_chunk(c):
            for m in range(3):
                land(m, c)

        op_ref[...] = _ffn_rows(xp_ref[...], *weights, before_chunk=before_chunk)

    @pl.when((i > 0) & (i < n_prompt))
    def _():
        op_ref[...] = _ffn_rows(xp_ref[...], *weights)

    @pl.when(i == n_prompt)
    def _():
        xs = _lane_tiles_to_rows(xs_ref) if len(xs_ref.shape) == 3 else xs_ref[...]
        out = _ffn_rows(xs, *weights)
        if len(os_ref.shape) == 3:
            _rows_to_lane_tiles(os_ref, out)
        else:
            os_ref[...] = out


def _ffn(xp, xs, g_pre, g_post, wg, wu, wd, sample_out_tiles):
    n_prompt = xp.shape[0] // FFN_ROW_TILE
    n_s = xs.shape[0]
    out_s = (n_s, D_MODEL // LANES, LANES) if sample_out_tiles else (n_s, D_MODEL)
    prompt_spec = pl.BlockSpec((FFN_ROW_TILE, D_MODEL), lambda i: (jnp.minimum(i, n_prompt - 1), 0))
    hbm_spec = pl.BlockSpec(memory_space=pl.ANY)
    return pl.pallas_call(
        functools.partial(_ffn_kernel, n_prompt=n_prompt),
        grid=(n_prompt + 1,),
        in_specs=[prompt_spec, _const_spec(xs.shape), _const_spec((1, D_MODEL)), _const_spec((1, D_MODEL)),
                  hbm_spec, hbm_spec, hbm_spec],
        out_specs=[prompt_spec, _const_spec(out_s)],
        out_shape=[jax.ShapeDtypeStruct(xp.shape, F32), jax.ShapeDtypeStruct(out_s, F32)],
        scratch_shapes=[pltpu.VMEM((D_MODEL, FFN_DIM), BF16), pltpu.VMEM((D_MODEL, FFN_DIM), BF16),
                        pltpu.VMEM((FFN_DIM, D_MODEL), BF16),
                        pltpu.VMEM((2, D_MODEL, FFN_COL_CHUNK), F32), pltpu.VMEM((2, D_MODEL, FFN_COL_CHUNK), F32),
                        pltpu.VMEM((2, FFN_COL_CHUNK, D_MODEL), F32),
                        pltpu.SemaphoreType.DMA((3, 2)),
                        pltpu.VMEM((FFN_ROW_TILE, FFN_DIM), BF16)],
        compiler_params=_params("arbitrary"),
        name="ffn",
    )(xp, xs, g_pre, g_post, wg, wu, wd)


def _chunk(ref, r, c):
    return ref[r:r + 1, c * MIX_CHUNK:(c + 1) * MIX_CHUNK]


def _mix_project(h, win_ref, c):
    parts = []
    for s in range(N_MIX_PARTS):
        o = s * D_CONV + c * MIX_CHUNK
        parts.append(_dot(h, win_ref[:, o:o + MIX_CHUNK].astype(BF16)))
    return jnp.concatenate(parts, axis=-1)


def _mix_out(ya, yb, wout_ref, c):
    ra = c * MIX_CHUNK
    rb = D_CONV + c * MIX_CHUNK
    return (_dot(ya.astype(BF16), wout_ref[ra:ra + MIX_CHUNK, :].astype(BF16))
            + _dot(yb.astype(BF16), wout_ref[rb:rb + MIX_CHUNK, :].astype(BF16)))


def _mix_front(zc, c, prev_a, prev_b, caw_ref, cbw_ref, cbb_ref, wax_ref):
    gb, gc, xa, xb, gg = (zc[:, s * MIX_CHUNK:(s + 1) * MIX_CHUNK] for s in range(N_MIX_PARTS))
    v = gc * xa
    ca = prev_a(v, 2) * _chunk(caw_ref, 0, c) + prev_a(v, 1) * _chunk(caw_ref, 1, c) + v * _chunk(caw_ref, 2, c)
    ya = gb * ca
    cb = (prev_b(xb, 3) * _chunk(cbw_ref, 0, c) + prev_b(xb, 2) * _chunk(cbw_ref, 1, c)
          + prev_b(xb, 1) * _chunk(cbw_ref, 2, c) + xb * _chunk(cbw_ref, 3, c))
    cb = cb + _chunk(cbb_ref, 0, c)
    gates = _dot(cb.astype(BF16), wax_ref[c].astype(BF16))
    return v, xb, ya, cb, gates, gg


def _lru_coeffs(gates, c, ba_ref, bx_ref, softplus_neg_lam):
    r = _sigmoid(gates[:, :MIX_CHUNK] + _chunk(ba_ref, 0, c))
    i = _sigmoid(gates[:, MIX_CHUNK:] + _chunk(bx_ref, 0, c))
    log_a = (-LRU_C * r) * softplus_neg_lam[:, c * MIX_CHUNK:(c + 1) * MIX_CHUNK]
    a = jnp.exp(log_a)
    mult = jnp.sqrt(_one_minus_exp2x(log_a))
    return a, mult, i


def _seq_copies(hbm_ref, buf_ref, sem_ref, slot, step_block, to_vmem):
    copies = []
    for b in range(SUBLANES):
        hbm = hbm_ref.at[b, pl.ds(step_block * MIX_BLOCK_STEPS, MIX_BLOCK_STEPS), :]
        vmem = buf_ref.at[slot, :, b, :]
        src, dst = (hbm, vmem) if to_vmem else (vmem, hbm)
        copies.append(pltpu.make_async_copy(src, dst, sem_ref.at[slot, b]))
    return copies


def _mix_kernel(x_hbm, xs_ref, sa_ref, sb_ref, h0_ref, gpre_ref, gpost_ref, win_ref, caw_ref, cbw_ref, cbb_ref,
                wa_ref, wx_ref, ba_ref, bx_ref, lam_ref, wout_ref, gq_ref, wq_ref,
                o_hbm, ta_ref, tb_ref, hl_ref, os_ref, na_ref, nb_ref, hs_ref, qs_ref,
                hista_ref, histb_ref, hc_ref, z_ref, xbuf_ref, obuf_ref, xsem_ref, osem_ref, winb_ref, waxb_ref,
                woutb_ref, *, n_tiles):
    i = pl.program_id(0)
    nb = SUBLANES
    rows = MIX_STEPS * nb
    hist_rows = (CONV_B_WIDTH - 1) * nb
    items = [(u, c) for u in range(MIX_BLOCK_STEPS // MIX_STEPS) for c in range(N_MIX_CHUNKS)]

    @pl.when(i == 0)
    def _():
        hista_ref[...] = jnp.zeros_like(hista_ref)
        histb_ref[...] = jnp.zeros_like(histb_ref)
        hc_ref[...] = jnp.zeros_like(hc_ref)
        for cp in _seq_copies(x_hbm, xbuf_ref, xsem_ref, 0, 0, True):
            cp.start()
        winb_ref[...] = win_ref[...].astype(BF16)
        woutb_ref[...] = wout_ref[...].astype(BF16)
        waxb_ref[...] = jnp.zeros_like(waxb_ref)
        heads_per_chunk = MIX_CHUNK // LRU_HEAD_DIM
        for hd in range(N_LRU_HEADS):
            c, j = divmod(hd, heads_per_chunk)
            rsl = slice(j * LRU_HEAD_DIM, (j + 1) * LRU_HEAD_DIM)
            waxb_ref[c, rsl, rsl] = wa_ref[hd].astype(BF16)
            waxb_ref[c, rsl, MIX_CHUNK + j * LRU_HEAD_DIM:MIX_CHUNK + (j + 1) * LRU_HEAD_DIM] = wx_ref[hd].astype(BF16)

    @pl.when(i + 1 < n_tiles)
    def _():
        for cp in _seq_copies(x_hbm, xbuf_ref, xsem_ref, (i + 1) % 3, i + 1, True):
            cp.start()

    @pl.when(i < n_tiles)
    def _():
        for cp in _seq_copies(x_hbm, xbuf_ref, xsem_ref, i % 3, i, True):
            cp.wait()

    @pl.when((i >= 3) & (i <= n_tiles))
    def _():
        for cp in _seq_copies(o_hbm, obuf_ref, osem_ref, (i - 3) % 2, i - 3, False):
            cp.wait()

    def prev(hist):
        return lambda cur, k: jnp.concatenate([hist[hist_rows - k * nb:, :], cur[:rows - k * nb, :]], axis=0)

    def x_rows(tile, u):
        return xbuf_ref[tile % 3, u * MIX_STEPS:(u + 1) * MIX_STEPS].reshape(rows, D_MODEL)

    def run(proj_tile, mix_tile):
        proj_tile, proj_slot = proj_tile if proj_tile is not None else (None, None)
        mix_tile, mix_slot = mix_tile if mix_tile is not None else (None, None)
        normed = {}
        if mix_tile is not None:
            sp = _softplus(-lam_ref[...])
            row = lax.broadcasted_iota(jnp.int32, (rows, MIX_CHUNK), 0)
            first_rows = jnp.where(mix_tile == 0, nb, 0)
            hist_a = hista_ref[...]
            hist_b = histb_ref[...]
            hcur = hc_ref[...]

        def project(u, c):
            if u not in normed:
                normed[u] = _rms(x_rows(proj_tile, u), gpre_ref[...]).astype(BF16)
            z_ref[proj_slot, u, c] = _mix_project(normed[u], winb_ref, c)

        for u, c in items:
            if mix_tile is None:
                project(u, c)
                continue
            if c == 0:
                y = None
                tails_a, tails_b, h_last = [], [], []
            csl = slice(c * MIX_CHUNK, (c + 1) * MIX_CHUNK)
            zc = z_ref[mix_slot, u, c]
            v, xb, ya, cb, gates, gg = _mix_front(zc, c, prev(hist_a[:, csl]), prev(hist_b[:, csl]),
                                                  caw_ref, cbw_ref, cbb_ref, waxb_ref)
            if proj_tile is not None:
                project(u, c)
            a, mult, gate_in = _lru_coeffs(gates, c, ba_ref, bx_ref, sp)
            if u == 0:
                mult = jnp.where(row < first_rows, 1.0, mult)
            b = (mult * gate_in) * cb
            hc = hcur[:, csl]
            hs = []
            for j in range(MIX_STEPS):
                sl = slice(j * nb, (j + 1) * nb)
                hc = a[sl, :] * hc + b[sl, :]
                hs.append(hc)
            yb = _gelu_tanh(gg) * jnp.concatenate(hs, axis=0)
            yc = _mix_out(ya, yb, woutb_ref, c)
            y = yc if y is None else y + yc
            tails_a.append(v[rows - hist_rows:, :])
            tails_b.append(xb[rows - hist_rows:, :])
            h_last.append(hc)
            if c == N_MIX_CHUNKS - 1:
                hist_a = jnp.concatenate(tails_a, axis=-1)
                hist_b = jnp.concatenate(tails_b, axis=-1)
                hcur = jnp.concatenate(h_last, axis=-1)
                out = x_rows(mix_tile, u) + _rms(y, gpost_ref[...])
                obuf_ref[mix_tile % 2, u * MIX_STEPS:(u + 1) * MIX_STEPS] = out.reshape(MIX_STEPS, nb, D_MODEL)
        if mix_tile is not None:
            hista_ref[...] = hist_a
            histb_ref[...] = hist_b
            hc_ref[...] = hcur

    @pl.when(i == 0)
    def _():
        run((i, 0), None)

    for parity in range(2):
        @pl.when((i > 0) & (i < n_tiles) & (i % 2 == parity))
        def _():
            run((i, parity), (i - 1, 1 - parity))

    @pl.when(i == n_tiles)
    def _():
        run(None, (i - 1, (n_tiles - 1) % 2))
        ta_ref[...] = hista_ref[...]
        tb_ref[...] = histb_ref[...]
        hl_ref[...] = hc_ref[...]

    @pl.when(i == n_tiles + 1)
    def _():
        _mix_sample_rows(xs_ref, sa_ref, sb_ref, h0_ref, gpre_ref, gpost_ref, winb_ref, caw_ref, cbw_ref, cbb_ref,
                         waxb_ref, ba_ref, bx_ref, lam_ref, woutb_ref, gq_ref, wq_ref,
                         os_ref, na_ref, nb_ref, hs_ref, qs_ref)

    @pl.when((i >= 1) & (i <= n_tiles))
    def _():
        for cp in _seq_copies(o_hbm, obuf_ref, osem_ref, (i - 1) % 2, i - 1, False):
            cp.start()

    @pl.when(i == n_tiles)
    def _():
        for tile in range(max(n_tiles - 2, 0), n_tiles):
            for cp in _seq_copies(o_hbm, obuf_ref, osem_ref, tile % 2, tile, False):
                cp.wait()


def _mix_sample_rows(x_ref, sa_ref, sb_ref, h0_ref, gpre_ref, gpost_ref, win_ref, caw_ref, cbw_ref, cbb_ref,
                     wax_ref, ba_ref, bx_ref, lam_ref, wout_ref, gq_ref, wq_ref,
                     o_ref, na_ref, nb_ref, hl_ref, q_ref):
    tiles_per_chunk = MIX_CHUNK // LANES
    n_a = CONV_A_WIDTH - 1
    x = x_ref[...]
    h = _rms(x, gpre_ref[...]).astype(BF16)
    sp = _softplus(-lam_ref[...])
    y = None
    for c in range(N_MIX_CHUNKS):
        csl = slice(c * MIX_CHUNK, (c + 1) * MIX_CHUNK)
        lane_tiles = range(c * tiles_per_chunk, (c + 1) * tiles_per_chunk)
        zc = _mix_project(h, win_ref, c)

        def state_a(k):
            return jnp.concatenate([sa_ref[:, j * n_a + (n_a - k), :] for j in lane_tiles], axis=-1)

        def state_b(k):
            return sb_ref[CONV_B_WIDTH - 1 - k, :, csl]

        v, xb, ya, cb, gates, gg = _mix_front(zc, c, lambda cur, k: state_a(k), lambda cur, k: state_b(k),
                                              caw_ref, cbw_ref, cbb_ref, wax_ref)
        a, mult, gate_in = _lru_coeffs(gates, c, ba_ref, bx_ref, sp)
        hnew = a * h0_ref[:, csl] + (mult * gate_in) * cb
        yb = _gelu_tanh(gg) * hnew
        yc = _mix_out(ya, yb, wout_ref, c)
        y = yc if y is None else y + yc

        for t, j in enumerate(lane_tiles):
            for age in range(n_a - 1):
                na_ref[:, j * n_a + age, :] = sa_ref[:, j * n_a + age + 1, :]
            na_ref[:, j * n_a + n_a - 1, :] = v[:, t * LANES:(t + 1) * LANES]
        for age in range(CONV_B_WIDTH - 2):
            nb_ref[age, :, csl] = sb_ref[age + 1, :, csl]
        nb_ref[CONV_B_WIDTH - 2, :, csl] = xb
        hl_ref[:, csl] = hnew
    out = x + _rms(y, gpost_ref[...])
    o_ref[...] = out
    q = _dot(_rms(out, gq_ref[...]).astype(BF16), wq_ref[...].astype(BF16))
    for hd in range(N_XHEADS):
        for half in range(XHEAD_DIM // LANES):
            col = hd * XHEAD_DIM + half * LANES
            q_ref[:, half * N_XHEADS + hd, :] = q[:, col:col + LANES]


def _mix(x, xs, sa, sb, h0, weights, g_q, wq, batch, seq):
    assert batch == SUBLANES and seq % MIX_BLOCK_STEPS == 0
    n_tiles = seq // MIX_BLOCK_STEPS
    n_sub = MIX_BLOCK_STEPS // MIX_STEPS
    n_s = xs.shape[0]
    hist_rows = (CONV_B_WIDTH - 1) * batch
    hbm_spec = pl.BlockSpec(memory_space=pl.ANY)
    hist_shape = jax.ShapeDtypeStruct((hist_rows, D_CONV), F32)
    h_shape = jax.ShapeDtypeStruct((batch, D_LRU), F32)
    q_shape = jax.ShapeDtypeStruct((n_s, HEAD_ROWS, LANES), F32)
    like = lambda a: jax.ShapeDtypeStruct(a.shape, F32)
    full = lambda a: _const_spec(a.shape)
    weight_specs = [_const_spec((1, D_MODEL)), _const_spec((1, D_MODEL)), _const_spec((D_MODEL, N_MIX_PARTS * D_CONV)),
                    _const_spec((CONV_A_WIDTH, D_CONV)), _const_spec((CONV_B_WIDTH, D_LRU)), _const_spec((1, D_LRU)),
                    _const_spec((N_LRU_HEADS, LRU_HEAD_DIM, LRU_HEAD_DIM)),
                    _const_spec((N_LRU_HEADS, LRU_HEAD_DIM, LRU_HEAD_DIM)),
                    _const_spec((1, D_LRU)), _const_spec((1, D_LRU)), _const_spec((1, D_LRU)),
                    _const_spec((D_MODEL, D_MODEL)), _const_spec((1, D_MODEL)), _const_spec((D_MODEL, D_MODEL))]
    return pl.pallas_call(
        functools.partial(_mix_kernel, n_tiles=n_tiles),
        grid=(n_tiles + 2,),
        in_specs=[hbm_spec, full(xs), full(sa), full(sb), full(h0)] + weight_specs,
        out_specs=[hbm_spec, full(hist_shape), full(hist_shape), full(h_shape),
                   full(xs), full(sa), full(sb), full(h0), full(q_shape)],
        out_shape=[like(x), hist_shape, hist_shape, h_shape, like(xs), like(sa), like(sb), like(h0), q_shape],
        scratch_shapes=[pltpu.VMEM(hist_shape.shape, F32), pltpu.VMEM(hist_shape.shape, F32),
                        pltpu.VMEM(h_shape.shape, F32),
                        pltpu.VMEM((2, n_sub, N_MIX_CHUNKS, MIX_STEPS * batch, N_MIX_PARTS * MIX_CHUNK), F32),
                        pltpu.VMEM((3, MIX_BLOCK_STEPS, batch, D_MODEL), F32),
                        pltpu.VMEM((2, MIX_BLOCK_STEPS, batch, D_MODEL), F32),
                        pltpu.SemaphoreType.DMA((3, batch)), pltpu.SemaphoreType.DMA((2, batch)),
                        pltpu.VMEM((D_MODEL, N_MIX_PARTS * D_CONV), BF16),
                        pltpu.VMEM((N_MIX_CHUNKS, MIX_CHUNK, 2 * MIX_CHUNK), BF16),
                        pltpu.VMEM((D_MODEL, D_MODEL), BF16)],
        compiler_params=_params("arbitrary"),
        name="mix",
    )(x, xs, sa, sb, h0, *weights, g_q, wq)


def _head_row_copies(buf_ref, hbm_ref, sem_ref, which, slot, step):
    copies = []
    for hd in range(N_XHEADS):
        for half in range(XHEAD_DIM // LANES):
            col = (hd * (XHEAD_DIM // LANES) + half) * LANES
            sub = half * N_XHEADS + hd
            src = buf_ref.at[slot, :, pl.ds(col, LANES)]
            dst = hbm_ref.at[pl.ds(step * KV_ROW_TILE, KV_ROW_TILE), sub, :]
            copies.append(pltpu.make_async_copy(src, dst, sem_ref.at[which, slot, sub]))
    return copies


def _kv_kernel(m_ref, g_ref, wk_ref, wv_ref, kb_ref, vb_ref, kt_hbm, vt_hbm, kbuf_ref, vbuf_ref, sem_ref, *, n_steps):
    i = pl.program_id(0)
    slot = i % 2
    m = _rms(m_ref[...], g_ref[...]).astype(BF16)
    k = _dot(m, wk_ref[...].astype(BF16))
    v = _dot(m, wv_ref[...].astype(BF16))
    kb_ref[...] = k.astype(BF16)
    vb_ref[...] = v.astype(BF16)
    kbuf_ref[slot] = k
    vbuf_ref[slot] = v

    def copies(s, step):
        return (_head_row_copies(kbuf_ref, kt_hbm, sem_ref, 0, s, step)
                + _head_row_copies(vbuf_ref, vt_hbm, sem_ref, 1, s, step))

    for cp in copies(slot, i):
        cp.start()

    @pl.when(i >= 1)
    def _():
        for cp in copies(1 - slot, i - 1):
            cp.wait()

    @pl.when(i == n_steps - 1)
    def _():
        for cp in copies(slot, i):
            cp.wait()


def _memory_kv(mem, g_mem, wk, wv):
    rows = mem.shape[0]
    n_steps = rows // KV_ROW_TILE
    row_spec = pl.BlockSpec((KV_ROW_TILE, D_MODEL), lambda i: (i, 0))
    hbm_spec = pl.BlockSpec(memory_space=pl.ANY)
    natural = jax.ShapeDtypeStruct((rows, D_MODEL), BF16)
    tiled = jax.ShapeDtypeStruct((rows, HEAD_ROWS, LANES), F32)
    buf = pltpu.VMEM((2, KV_ROW_TILE, D_MODEL), F32)
    return pl.pallas_call(
        functools.partial(_kv_kernel, n_steps=n_steps),
        grid=(n_steps,),
        in_specs=[row_spec, _const_spec((1, D_MODEL)), _const_spec((D_MODEL, D_MODEL)),
                  _const_spec((D_MODEL, D_MODEL))],
        out_specs=[row_spec, row_spec, hbm_spec, hbm_spec],
        out_shape=[natural, natural, tiled, tiled],
        scratch_shapes=[buf, buf, pltpu.SemaphoreType.DMA((2, 2, HEAD_ROWS))],
        compiler_params=_params("arbitrary"),
        name="memory_kv",
    )(mem, g_mem, wk, wv)


def _merge_heads(a, lead_shape):
    a = a.reshape(lead_shape + (XHEAD_DIM // LANES, N_XHEADS, LANES))
    return jnp.swapaxes(a, -3, -2).reshape(lead_shape + (N_XHEADS, XHEAD_DIM))


def _softmax_rows(s):
    e = jnp.exp(s - jnp.max(s, axis=-1, keepdims=True))
    return e * (1.0 / jnp.sum(e, axis=-1, keepdims=True))


def _split_heads(a):
    n = a.shape[0]
    a = a.reshape(n, -1, N_XHEADS, 2, LANES)
    return jnp.swapaxes(a, 2, 3).reshape(n, -1, LANES)


def _attend_one(qv, k, v, ones_ref):
    rows = N_MEM * HEAD_ROWS
    prod = (k.reshape(N_MEM, HEAD_ROWS, LANES) * (qv * (XHEAD_DIM ** -0.5))[None]).reshape(rows, LANES).astype(BF16)
    pair = jnp.concatenate([prod[:rows // 2], prod[rows // 2:]], axis=-1)
    sums = _dot(pair, ones_ref[...])
    part = jnp.concatenate([sums[:, :LANES], sums[:, LANES:]], axis=0).reshape(N_MEM, HEAD_ROWS, LANES)
    s = part + pltpu.roll(part, N_XHEADS, 1)
    e = jnp.exp(s - jnp.max(s, axis=0, keepdims=True))
    den = jnp.sum(e, axis=0)
    num = jnp.sum(e * v.reshape(N_MEM, HEAD_ROWS, LANES), axis=0)
    return num * (1.0 / den)


def _cache_copies(ck_hbm, cv_hbm, kbuf_ref, vbuf_ref, sem_ref, first_row, bank):
    rows = pl.ds(first_row, XATTN_BANK_ROWS)
    slots = pl.ds(bank * XATTN_BANK_ROWS, XATTN_BANK_ROWS)
    return [pltpu.make_async_copy(ck_hbm.at[rows], kbuf_ref.at[slots], sem_ref.at[0, bank]),
            pltpu.make_async_copy(cv_hbm.at[rows], vbuf_ref.at[slots], sem_ref.at[1, bank])]


def _xattn_kernel(x_ref, k_ref, v_ref, xs_ref, qs_ref, ones_ref, gpre_ref, gpost_ref, wq_ref, wo_ref, ck_hbm, cv_hbm,
                  o_ref, os_ref, kbuf_ref, vbuf_ref, sem_ref, attn_ref, wqb_ref, wob_ref, *, n_prompt, n_sample):
    i = pl.program_id(0)
    cols = D_MODEL // 2
    per_step = XATTN_SAMPLES_PER_STEP

    def cache(first_row, bank):
        return _cache_copies(ck_hbm, cv_hbm, kbuf_ref, vbuf_ref, sem_ref, first_row, bank)

    @pl.when(i == 0)
    def _():
        for bank in range(2):
            for cp in cache(bank * XATTN_BANK_ROWS, bank):
                cp.start()
        wqb_ref[...] = wq_ref[...].astype(BF16)
        wob_ref[...] = wo_ref[...].astype(BF16)

    @pl.when(i < n_prompt)
    def _():
        def project(h, w_ref, group, as_bf16):
            bank = group % 2
            first = i * per_step + group * XATTN_BANK_ROWS
            for cp in cache(first, bank):
                cp.wait()
            pieces = []
            for c in range(XATTN_BANK_ROWS):
                piece = _dot(h, w_ref[:, c * cols:(c + 1) * cols])
                pieces.append(piece.astype(BF16) if as_bf16 else piece)
                slot = bank * XATTN_BANK_ROWS + c
                attn_ref[first + c] = _attend_one(qs_ref[first + c], kbuf_ref[slot], vbuf_ref[slot], ones_ref)
            nxt = jnp.minimum(first + 2 * XATTN_BANK_ROWS, n_sample - XATTN_BANK_ROWS)
            for cp in cache(nxt, bank):
                cp.start()
            return jnp.concatenate(pieces, axis=-1)

        def scores(q):
            return [lax.dot_general(q[:, hd * XHEAD_DIM:(hd + 1) * XHEAD_DIM], k_ref[0, :, hd * XHEAD_DIM:(hd + 1) * XHEAD_DIM],
                                    (((1,), (1,)), ((), ())), preferred_element_type=F32) for hd in range(N_XHEADS)]

        def weighted(ss):
            ps = [_softmax_rows(s * (XHEAD_DIM ** -0.5)).astype(BF16) for s in ss]
            return jnp.concatenate([_dot(p, v_ref[0, :, hd * XHEAD_DIM:(hd + 1) * XHEAD_DIM]).astype(BF16)
                                    for hd, p in enumerate(ps)], axis=-1)

        ra, rb = (slice(u * XATTN_SUB_ROWS, (u + 1) * XATTN_SUB_ROWS) for u in range(2))
        xa, xb = x_ref[ra, :], x_ref[rb, :]
        ha = _rms(xa, gpre_ref[...]).astype(BF16)
        hb = _rms(xb, gpre_ref[...]).astype(BF16)
        qa = project(ha, wqb_ref, 0, True)
        sa = scores(qa)
        qb = project(hb, wqb_ref, 1, True)
        aa = weighted(sa)
        sb = scores(qb)
        ya = project(aa, wob_ref, 2, False)
        ab = weighted(sb)
        yb = project(ab, wob_ref, 3, False)
        o_ref[ra, :] = xa + _rms(ya, gpost_ref[...])
        o_ref[rb, :] = xb + _rms(yb, gpost_ref[...])

    @pl.when(i == n_prompt)
    def _():
        for bank in range(2):
            for cp in cache(n_sample - XATTN_BANK_ROWS, bank):
                cp.wait()
        attn = jnp.concatenate([attn_ref[:, part * N_XHEADS + hd, :] for hd in range(N_XHEADS)
                                for part in range(XHEAD_DIM // LANES)], axis=-1)
        y = _dot(attn.astype(BF16), wob_ref[...])
        os_ref[...] = xs_ref[...] + _rms(y, gpost_ref[...])


def _xattn(x, mem_k, mem_v, xs, q_s, cache_k, cache_v, g_pre, g_post, wq, wo, batch, seq):
    n_t = seq // XATTN_ROW_TILE
    n_prompt = batch * n_t
    n_s = xs.shape[0]
    assert n_prompt * XATTN_SAMPLES_PER_STEP == n_s and XATTN_ROW_TILE == 2 * XATTN_SUB_ROWS
    assert XATTN_SAMPLES_PER_STEP == 4 * XATTN_BANK_ROWS
    tile = lambda i: jnp.minimum(i, n_prompt - 1)
    row_spec = pl.BlockSpec((XATTN_ROW_TILE, D_MODEL), lambda i: (tile(i), 0))
    kv_spec = pl.BlockSpec((1, N_MEM, D_MODEL), lambda i: (tile(i) // n_t, 0, 0))
    hbm_spec = pl.BlockSpec(memory_space=pl.ANY)
    eye2 = jnp.eye(2, dtype=BF16)
    ones = jnp.kron(eye2, jnp.ones((LANES, LANES), BF16))
    cache_buf = pltpu.VMEM((2 * XATTN_BANK_ROWS, N_MEM * HEAD_ROWS, LANES), F32)
    return pl.pallas_call(
        functools.partial(_xattn_kernel, n_prompt=n_prompt, n_sample=n_s),
        grid=(n_prompt + 1,),
        in_specs=[row_spec, kv_spec, kv_spec, _const_spec(xs.shape), _const_spec(q_s.shape),
                  _const_spec((2 * LANES, 2 * LANES)), _const_spec((1, D_MODEL)), _const_spec((1, D_MODEL)),
                  _const_spec((D_MODEL, D_MODEL)), _const_spec((D_MODEL, D_MODEL)), hbm_spec, hbm_spec],
        out_specs=[row_spec, _const_spec(xs.shape)],
        out_shape=[jax.ShapeDtypeStruct(x.shape, F32), jax.ShapeDtypeStruct(xs.shape, F32)],
        scratch_shapes=[cache_buf, cache_buf, pltpu.SemaphoreType.DMA((2, 2)),
                        pltpu.VMEM((n_s, HEAD_ROWS, LANES), F32),
                        pltpu.VMEM((D_MODEL, D_MODEL), BF16), pltpu.VMEM((D_MODEL, D_MODEL), BF16)],
        compiler_params=_params("arbitrary"),
        name="xattn",
    )(x, mem_k, mem_v, xs, q_s, ones, g_pre, g_post, wq, wo, cache_k, cache_v)


def kernel(x_prompt, x_sample, mem_prompt, cache_mem_k, cache_mem_v, state_conv_a, state_conv_b, state_lru, g_ffn1_pre, g_ffn1_post, ffn1_wg, ffn1_wu, ffn1_wd, g_mix_pre, g_mix_post, w_in, conv_a_w, conv_b_w, conv_b_b, lru_wa, lru_ba, lru_wx, lru_bx, lru_lam, w_out, g_xattn_pre, g_xattn_post, g_mem, xattn_wq, xattn_wk, xattn_wv, xattn_wo, g_ffn2_pre, g_ffn2_post, ffn2_wg, ffn2_wu, ffn2_wd):
    batch, seq, _ = x_prompt.shape
    n_s = x_sample.shape[0]
    depth = g_ffn1_pre.shape[0]
    assert depth == 1 and x_sample.shape[1] == 1
    l = 0
    row = lambda p: p[l].reshape(1, -1)

    yp = x_prompt.reshape(batch * seq, D_MODEL)
    ys = x_sample.reshape(n_s, D_MODEL // LANES, LANES)
    sa = state_conv_a[l].reshape(n_s, CONV_A_WIDTH - 1, D_CONV // LANES, LANES)
    sa = jnp.swapaxes(sa, 1, 2).reshape(n_s, (D_CONV // LANES) * (CONV_A_WIDTH - 1), LANES)
    sb = jnp.swapaxes(state_conv_b[l], 0, 1)

    ffn1 = (row(g_ffn1_pre), row(g_ffn1_post), ffn1_wg[l], ffn1_wu[l], ffn1_wd[l])
    ffn2 = (row(g_ffn2_pre), row(g_ffn2_post), ffn2_wg[l], ffn2_wu[l], ffn2_wd[l])
    mix_w = (row(g_mix_pre), row(g_mix_post), w_in[l], conv_a_w[l], conv_b_w[l], row(conv_b_b), lru_wa[l], lru_wx[l],
             row(lru_ba), row(lru_bx), row(lru_lam), w_out[l])

    mk_b, mv_b, mk, mv = _memory_kv(mem_prompt.reshape(batch * N_MEM, D_MODEL), row(g_mem), xattn_wk[l], xattn_wv[l])

    yp, ys = _ffn(yp, ys, *ffn1, sample_out_tiles=False)

    yp, tail_a, tail_b, tail_h, ys, new_a, new_b, new_h, q_s = _mix(
        yp.reshape(batch, seq, D_MODEL), ys, sa, sb, state_lru[l], mix_w, row(g_xattn_pre), xattn_wq[l], batch, seq)
    yp = yp.reshape(batch * seq, D_MODEL)
    tail_a = tail_a.reshape(CONV_B_WIDTH - 1, batch, D_CONV)[CONV_B_WIDTH - CONV_A_WIDTH:].transpose(1, 0, 2)
    tail_b = tail_b.reshape(CONV_B_WIDTH - 1, batch, D_LRU).transpose(1, 0, 2)
    new_a = jnp.swapaxes(new_a.reshape(n_s, D_CONV // LANES, CONV_A_WIDTH - 1, LANES), 1, 2)
    new_a = new_a.reshape(1, n_s, CONV_A_WIDTH - 1, D_CONV)
    new_b = jnp.swapaxes(new_b, 0, 1)[None]

    yp, ys = _xattn(yp, mk_b.reshape(batch, N_MEM, D_MODEL), mv_b.reshape(batch, N_MEM, D_MODEL), ys, q_s,
                    _split_heads(cache_mem_k[l]), _split_heads(cache_mem_v[l]),
                    row(g_xattn_pre), row(g_xattn_post), xattn_wq[l], xattn_wo[l], batch, seq)

    yp, ys = _ffn(yp, ys, *ffn2, sample_out_tiles=True)

    kv_lead = (1, batch, N_MEM)
    return (yp.reshape(batch, seq, D_MODEL), ys.reshape(n_s, 1, D_MODEL),
            _merge_heads(mk, kv_lead), _merge_heads(mv, kv_lead),
            tail_a[None], tail_b[None], tail_h[None], new_a, new_b, new_h[None])
```

```python
import functools
import math

import jax
import jax.numpy as jnp
from jax import lax
from jax.experimental import pallas as pl
from jax.experimental.pallas import tpu as pltpu

D_MODEL = 1024
D_CONV = 512
D_LRU = 512
N_LRU_HEADS = 8
LRU_HEAD_DIM = D_LRU // N_LRU_HEADS
LRU_C = 8.0
CONV_A_WIDTH = 3
CONV_B_WIDTH = 4
N_MIX_PARTS = 5
FFN_DIM = 2816
N_MEM = 256
N_XHEADS = 4
XHEAD_DIM = D_MODEL // N_XHEADS
RMS_EPS = 1e-6

F32 = jnp.float32
BF16 = jnp.bfloat16

V7X_VMEM_LIMIT_BYTES = 56 * 1024 * 1024
SUBLANES = 8
LANES = 128
HEAD_ROWS = N_XHEADS * XHEAD_DIM // LANES

FFN_ROW_TILE = 1024
FFN_COL_CHUNK = 256
FFN_CHUNKS = FFN_DIM // FFN_COL_CHUNK
MIX_STEPS = 32
MIX_BLOCK_STEPS = 64
MIX_CHUNK = 256
N_MIX_CHUNKS = D_CONV // MIX_CHUNK
XATTN_ROW_TILE = 512
XATTN_SUB_ROWS = 256
KV_ROW_TILE = 512
XATTN_SAMPLES_PER_STEP = 4


def _rms(x, g):
    y = x * lax.rsqrt(jnp.mean(x * x, axis=-1, keepdims=True) + RMS_EPS)
    return y * g


def _dot(a, b):
    return jnp.dot(a, b, preferred_element_type=F32)


def _sigmoid(x):
    return 1.0 / (1.0 + jnp.exp(-x))


def _gelu_tanh(x):
    c = math.sqrt(2.0 / math.pi)
    neg_two_z = x * ((-2.0 * c * 0.044715) * (x * x) - 2.0 * c)
    return x / (1.0 + jnp.exp(neg_two_z))


def _one_minus_exp2x(x):
    t = jnp.tanh(x)
    return (-2.0 * t) / (1.0 - t)


def _log1p(w):
    u = 1.0 + w
    return jnp.where(u == 1.0, w, jnp.log(u) * w / (u - 1.0))


def _softplus(x):
    return jnp.maximum(x, 0.0) + _log1p(jnp.exp(-jnp.abs(x)))


def _const_spec(shape):
    zeros = (0,) * len(shape)
    return pl.BlockSpec(shape, lambda *_: zeros, pipeline_mode=pl.Buffered(1))


def _params(*sem):
    return pltpu.CompilerParams(dimension_semantics=sem, vmem_limit_bytes=V7X_VMEM_LIMIT_BYTES)


def _lane_tiles_to_rows(ref):
    return jnp.concatenate([ref[:, j, :] for j in range(ref.shape[1])], axis=-1)


def _rows_to_lane_tiles(ref, val):
    for j in range(ref.shape[1]):
        ref[:, j, :] = val[:, j * LANES:(j + 1) * LANES]


def _ffn_up(h, wgb_ref, wub_ref, act_ref, c):
    rows = h.shape[0]
    sl = slice(c * FFN_COL_CHUNK, (c + 1) * FFN_COL_CHUNK)
    g = _dot(h, wgb_ref[:, sl])
    u = _dot(h, wub_ref[:, sl])
    act_ref[0:rows, sl] = ((g * _sigmoid(g)) * u).astype(BF16)


def _ffn_rows(x, gpre_ref, gpost_ref, wgb_ref, wub_ref, wdb_ref, act_ref, before_chunk=None, before_down=None):
    rows = x.shape[0]
    h = _rms(x, gpre_ref[...]).astype(BF16)
    for c in range(FFN_CHUNKS):
        if before_chunk is not None:
            before_chunk(c)
        _ffn_up(h, wgb_ref, wub_ref, act_ref, c)
    if before_down is not None:
        before_down()
    y = _dot(act_ref[0:rows, :], wdb_ref[...])
    return x + 0.5 * _rms(y, gpost_ref[...])


def _ffn_weight_copy(hbm_ref, stage_ref, sem_ref, which, c, by_columns):
    if by_columns:
        src = hbm_ref.at[:, pl.ds(c * FFN_COL_CHUNK, FFN_COL_CHUNK)]
    else:
        src = hbm_ref.at[pl.ds(c * FFN_COL_CHUNK, FFN_COL_CHUNK), :]
    return pltpu.make_async_copy(src, stage_ref.at[c % 2], sem_ref.at[which, c % 2])


def _ffn_kernel(xp_ref, xs_ref, gpre_ref, gpost_ref, wg_hbm, wu_hbm, wd_hbm, op_ref, os_ref,
                wgb_ref, wub_ref, wdb_ref, sg_ref, su_ref, sd_ref, sem_ref, act_ref, *, n_prompt):
    i = pl.program_id(0)
    weights = (gpre_ref, gpost_ref, wgb_ref, wub_ref, wdb_ref, act_ref)
    mats = ((wg_hbm, sg_ref, wgb_ref, True), (wu_hbm, su_ref, wub_ref, True), (wd_hbm, sd_ref, wdb_ref, False))

    def copy(m, c):
        hbm, stage, _, by_columns = mats[m]
        return _ffn_weight_copy(hbm, stage, sem_ref, m, c, by_columns)

    def land(m, c):
        _, stage, dst, by_columns = mats[m]
        copy(m, c).wait()
        sl = slice(c * FFN_COL_CHUNK, (c + 1) * FFN_COL_CHUNK)
        if by_columns:
            dst[:, sl] = stage[c % 2].astype(BF16)
        else:
            dst[sl, :] = stage[c % 2].astype(BF16)
        if c + 2 < FFN_CHUNKS:
            copy(m, c + 2).start()

    @pl.when(i == 0)
    def _():
        for c in range(2):
            for m in range(3):
                copy(m, c).start()

        def before_chunk(c):
            for m in range(3):
                land(m, c)

        op_ref[...] = _ffn_rows(xp_ref[...], *weights, before_chunk=before_chunk)

    @pl.when((i > 0) & (i < n_prompt))
    def _():
        op_ref[...] = _ffn_rows(xp_ref[...], *weights)

    @pl.when(i == n_prompt)
    def _():
        xs = _lane_tiles_to_rows(xs_ref) if len(xs_ref.shape) == 3 else xs_ref[...]
        out = _ffn_rows(xs, *weights)
        if len(os_ref.shape) == 3:
            _rows_to_lane_tiles(os_ref, out)
        else:
            os_ref[...] = out


def _ffn(xp, xs, g_pre, g_post, wg, wu, wd, sample_out_tiles):
    n_prompt = xp.shape[0] // FFN_ROW_TILE
    n_s = xs.shape[0]
    out_s = (n_s, D_MODEL // LANES, LANES) if sample_out_tiles else (n_s, D_MODEL)
    prompt_spec = pl.BlockSpec((FFN_ROW_TILE, D_MODEL), lambda i: (jnp.minimum(i, n_prompt - 1), 0))
    hbm_spec = pl.BlockSpec(memory_space=pl.ANY)
    return pl.pallas_call(
        functools.partial(_ffn_kernel, n_prompt=n_prompt),
        grid=(n_prompt + 1,),
        in_specs=[prompt_spec, _const_spec(xs.shape), _const_spec((1, D_MODEL)), _const_spec((1, D_MODEL)),
                  hbm_spec, hbm_spec, hbm_spec],
        out_specs=[prompt_spec, _const_spec(out_s)],
        out_shape=[jax.ShapeDtypeStruct(xp.shape, F32), jax.ShapeDtypeStruct(out_s, F32)],
        scratch_shapes=[pltpu.VMEM((D_MODEL, FFN_DIM), BF16), pltpu.VMEM((D_MODEL, FFN_DIM), BF16),
                        pltpu.VMEM((FFN_DIM, D_MODEL), BF16),
                        pltpu.VMEM((2, D_MODEL, FFN_COL_CHUNK), F32), pltpu.VMEM((2, D_MODEL, FFN_COL_CHUNK), F32),
                        pltpu.VMEM((2, FFN_COL_CHUNK, D_MODEL), F32),
                        pltpu.SemaphoreType.DMA((3, 2)),
                        pltpu.VMEM((FFN_ROW_TILE, FFN_DIM), BF16)],
        compiler_params=_params("arbitrary"),
        name="ffn",
    )(xp, xs, g_pre, g_post, wg, wu, wd)


def _chunk(ref, r, c):
    return ref[r:r + 1, c * MIX_CHUNK:(c + 1) * MIX_CHUNK]


def _mix_project(h, win_ref, c):
    parts = []
    for s in range(N_MIX_PARTS):
        o = s * D_CONV + c * MIX_CHUNK
        parts.append(_dot(h, win_ref[:, o:o + MIX_CHUNK].astype(BF16)))
    return jnp.concatenate(parts, axis=-1)


def _mix_out(ya, yb, wout_ref, c):
    ra = c * MIX_CHUNK
    rb = D_CONV + c * MIX_CHUNK
    return (_dot(ya.astype(BF16), wout_ref[ra:ra + MIX_CHUNK, :].astype(BF16))
            + _dot(yb.astype(BF16), wout_ref[rb:rb + MIX_CHUNK, :].astype(BF16)))


def _mix_front(zc, c, prev_a, prev_b, caw_ref, cbw_ref, cbb_ref, wax_ref):
    gb, gc, xa, xb, gg = (zc[:, s * MIX_CHUNK:(s + 1) * MIX_CHUNK] for s in range(N_MIX_PARTS))
    v = gc * xa
    ca = prev_a(v, 2) * _chunk(caw_ref, 0, c) + prev_a(v, 1) * _chunk(caw_ref, 1, c) + v * _chunk(caw_ref, 2, c)
    ya = gb * ca
    cb = (prev_b(xb, 3) * _chunk(cbw_ref, 0, c) + prev_b(xb, 2) * _chunk(cbw_ref, 1, c)
          + prev_b(xb, 1) * _chunk(cbw_ref, 2, c) + xb * _chunk(cbw_ref, 3, c))
    cb = cb + _chunk(cbb_ref, 0, c)
    gates = _dot(cb.astype(BF16), wax_ref[c].astype(BF16))
    return v, xb, ya, cb, gates, gg


def _lru_coeffs(gates, c, ba_ref, bx_ref, softplus_neg_lam):
    r = _sigmoid(gates[:, :MIX_CHUNK] + _chunk(ba_ref, 0, c))
    i = _sigmoid(gates[:, MIX_CHUNK:] + _chunk(bx_ref, 0, c))
    log_a = (-LRU_C * r) * softplus_neg_lam[:, c * MIX_CHUNK:(c + 1) * MIX_CHUNK]
    a = jnp.exp(log_a)
    mult = jnp.sqrt(_one_minus_exp2x(log_a))
    return a, mult, i


def _seq_copies(hbm_ref, buf_ref, sem_ref, slot, step_block, to_vmem):
    copies = []
    for b in range(SUBLANES):
        hbm = hbm_ref.at[b, pl.ds(step_block * MIX_BLOCK_STEPS, MIX_BLOCK_STEPS), :]
        vmem = buf_ref.at[slot, :, b, :]
        src, dst = (hbm, vmem) if to_vmem else (vmem, hbm)
        copies.append(pltpu.make_async_copy(src, dst, sem_ref.at[slot, b]))
    return copies


def _mix_kernel(x_hbm, xs_ref, sa_ref, sb_ref, h0_ref, gpre_ref, gpost_ref, win_ref, caw_ref, cbw_ref, cbb_ref,
                wa_ref, wx_ref, ba_ref, bx_ref, lam_ref, wout_ref, gq_ref, wq_ref,
                o_hbm, ta_ref, tb_ref, hl_ref, os_ref, na_ref, nb_ref, hs_ref, qs_ref,
                hista_ref, histb_ref, hc_ref, z_ref, xbuf_ref, obuf_ref, xsem_ref, osem_ref, winb_ref, waxb_ref,
                woutb_ref, *, n_tiles):
    i = pl.program_id(0)
    nb = SUBLANES
    rows = MIX_STEPS * nb
    hist_rows = (CONV_B_WIDTH - 1) * nb
    items = [(u, c) for u in range(MIX_BLOCK_STEPS // MIX_STEPS) for c in range(N_MIX_CHUNKS)]

    @pl.when(i == 0)
    def _():
        hista_ref[...] = jnp.zeros_like(hista_ref)
        histb_ref[...] = jnp.zeros_like(histb_ref)
        hc_ref[...] = jnp.zeros_like(hc_ref)
        for cp in _seq_copies(x_hbm, xbuf_ref, xsem_ref, 0, 0, True):
            cp.start()
        winb_ref[...] = win_ref[...].astype(BF16)
        woutb_ref[...] = wout_ref[...].astype(BF16)
        waxb_ref[...] = jnp.zeros_like(waxb_ref)
        heads_per_chunk = MIX_CHUNK // LRU_HEAD_DIM
        for hd in range(N_LRU_HEADS):
            c, j = divmod(hd, heads_per_chunk)
            rsl = slice(j * LRU_HEAD_DIM, (j + 1) * LRU_HEAD_DIM)
            waxb_ref[c, rsl, rsl] = wa_ref[hd].astype(BF16)
            waxb_ref[c, rsl, MIX_CHUNK + j * LRU_HEAD_DIM:MIX_CHUNK + (j + 1) * LRU_HEAD_DIM] = wx_ref[hd].astype(BF16)

    @pl.when(i + 1 < n_tiles)
    def _():
        for cp in _seq_copies(x_hbm, xbuf_ref, xsem_ref, (i + 1) % 3, i + 1, True):
            cp.start()

    @pl.when(i < n_tiles)
    def _():
        for cp in _seq_copies(x_hbm, xbuf_ref, xsem_ref, i % 3, i, True):
            cp.wait()

    @pl.when((i >= 3) & (i <= n_tiles))
    def _():
        for cp in _seq_copies(o_hbm, obuf_ref, osem_ref, (i - 3) % 2, i - 3, False):
            cp.wait()

    def prev(hist):
        return lambda cur, k: jnp.concatenate([hist[hist_rows - k * nb:, :], cur[:rows - k * nb, :]], axis=0)

    def x_rows(tile, u):
        return xbuf_ref[tile % 3, u * MIX_STEPS:(u + 1) * MIX_STEPS].reshape(rows, D_MODEL)

    def run(proj_tile, mix_tile):
        proj_tile, proj_slot = proj_tile if proj_tile is not None else (None, None)
        mix_tile, mix_slot = mix_tile if mix_tile is not None else (None, None)
        normed = {}
        if mix_tile is not None:
            sp = _softplus(-lam_ref[...])
            row = lax.broadcasted_iota(jnp.int32, (rows, MIX_CHUNK), 0)
            first_rows = jnp.where(mix_tile == 0, nb, 0)
            hist_a = hista_ref[...]
            hist_b = histb_ref[...]
            hcur = hc_ref[...]

        def project(u, c):
            if u not in normed:
                normed[u] = _rms(x_rows(proj_tile, u), gpre_ref[...]).astype(BF16)
            z_ref[proj_slot, u, c] = _mix_project(normed[u], winb_ref, c)

        for u, c in items:
            if mix_tile is None:
                project(u, c)
                continue
            if c == 0:
                y = None
                tails_a, tails_b, h_last = [], [], []
            csl = slice(c * MIX_CHUNK, (c + 1) * MIX_CHUNK)
            zc = z_ref[mix_slot, u, c]
            v, xb, ya, cb, gates, gg = _mix_front(zc, c, prev(hist_a[:, csl]), prev(hist_b[:, csl]),
                                                  caw_ref, cbw_ref, cbb_ref, waxb_ref)
            if proj_tile is not None:
                project(u, c)
            a, mult, gate_in = _lru_coeffs(gates, c, ba_ref, bx_ref, sp)
            if u == 0:
                mult = jnp.where(row < first_rows, 1.0, mult)
            b = (mult * gate_in) * cb
            hc = hcur[:, csl]
            hs = []
            for j in range(MIX_STEPS):
                sl = slice(j * nb, (j + 1) * nb)
                hc = a[sl, :] * hc + b[sl, :]
                hs.append(hc)
            yb = _gelu_tanh(gg) * jnp.concatenate(hs, axis=0)
            yc = _mix_out(ya, yb, woutb_ref, c)
            y = yc if y is None else y + yc
            tails_a.append(v[rows - hist_rows:, :])
            tails_b.append(xb[rows - hist_rows:, :])
            h_last.append(hc)
            if c == N_MIX_CHUNKS - 1:
                hist_a = jnp.concatenate(tails_a, axis=-1)
                hist_b = jnp.concatenate(tails_b, axis=-1)
                hcur = jnp.concatenate(h_last, axis=-1)
                out = x_rows(mix_tile, u) + _rms(y, gpost_ref[...])
                obuf_ref[mix_tile % 2, u * MIX_STEPS:(u + 1) * MIX_STEPS] = out.reshape(MIX_STEPS, nb, D_MODEL)
        if mix_tile is not None:
            hista_ref[...] = hist_a
            histb_ref[...] = hist_b
            hc_ref[...] = hcur

    @pl.when(i == 0)
    def _():
        run((i, 0), None)

    for parity in range(2):
        @pl.when((i > 0) & (i < n_tiles) & (i % 2 == parity))
        def _():
            run((i, parity), (i - 1, 1 - parity))

    @pl.when(i == n_tiles)
    def _():
        run(None, (i - 1, (n_tiles - 1) % 2))
        ta_ref[...] = hista_ref[...]
        tb_ref[...] = histb_ref[...]
        hl_ref[...] = hc_ref[...]

    @pl.when(i == n_tiles + 1)
    def _():
        _mix_sample_rows(xs_ref, sa_ref, sb_ref, h0_ref, gpre_ref, gpost_ref, winb_ref, caw_ref, cbw_ref, cbb_ref,
                         waxb_ref, ba_ref, bx_ref, lam_ref, woutb_ref, gq_ref, wq_ref,
                         os_ref, na_ref, nb_ref, hs_ref, qs_ref)

    @pl.when((i >= 1) & (i <= n_tiles))
    def _():
        for cp in _seq_copies(o_hbm, obuf_ref, osem_ref, (i - 1) % 2, i - 1, False):
            cp.start()

    @pl.when(i == n_tiles)
    def _():
        for tile in range(max(n_tiles - 2, 0), n_tiles):
            for cp in _seq_copies(o_hbm, obuf_ref, osem_ref, tile % 2, tile, False):
                cp.wait()


def _mix_sample_rows(x_ref, sa_ref, sb_ref, h0_ref, gpre_ref, gpost_ref, win_ref, caw_ref, cbw_ref, cbb_ref,
                     wax_ref, ba_ref, bx_ref, lam_ref, wout_ref, gq_ref, wq_ref,
                     o_ref, na_ref, nb_ref, hl_ref, q_ref):
    tiles_per_chunk = MIX_CHUNK // LANES
    n_a = CONV_A_WIDTH - 1
    x = x_ref[...]
    h = _rms(x, gpre_ref[...]).astype(BF16)
    sp = _softplus(-lam_ref[...])
    y = None
    for c in range(N_MIX_CHUNKS):
        csl = slice(c * MIX_CHUNK, (c + 1) * MIX_CHUNK)
        lane_tiles = range(c * tiles_per_chunk, (c + 1) * tiles_per_chunk)
        zc = _mix_project(h, win_ref, c)

        def state_a(k):
            return jnp.concatenate([sa_ref[:, j * n_a + (n_a - k), :] for j in lane_tiles], axis=-1)

        def state_b(k):
            return sb_ref[CONV_B_WIDTH - 1 - k, :, csl]

        v, xb, ya, cb, gates, gg = _mix_front(zc, c, lambda cur, k: state_a(k), lambda cur, k: state_b(k),
                                              caw_ref, cbw_ref, cbb_ref, wax_ref)
        a, mult, gate_in = _lru_coeffs(gates, c, ba_ref, bx_ref, sp)
        hnew = a * h0_ref[:, csl] + (mult * gate_in) * cb
        yb = _gelu_tanh(gg) * hnew
        yc = _mix_out(ya, yb, wout_ref, c)
        y = yc if y is None else y + yc

        for t, j in enumerate(lane_tiles):
            for age in range(n_a - 1):
                na_ref[:, j * n_a + age, :] = sa_ref[:, j * n_a + age + 1, :]
            na_ref[:, j * n_a + n_a - 1, :] = v[:, t * LANES:(t + 1) * LANES]
        for age in range(CONV_B_WIDTH - 2):
            nb_ref[age, :, csl] = sb_ref[age + 1, :, csl]
        nb_ref[CONV_B_WIDTH - 2, :, csl] = xb
        hl_ref[:, csl] = hnew
    out = x + _rms(y, gpost_ref[...])
    o_ref[...] = out
    q = _dot(_rms(out, gq_ref[...]).astype(BF16), wq_ref[...].astype(BF16))
    for hd in range(N_XHEADS):
        for half in range(XHEAD_DIM // LANES):
            col = hd * XHEAD_DIM + half * LANES
            q_ref[:, half * N_XHEADS + hd, :] = q[:, col:col + LANES]


def _mix(x, xs, sa, sb, h0, weights, g_q, wq, batch, seq):
    assert batch == SUBLANES and seq % MIX_BLOCK_STEPS == 0
    n_tiles = seq // MIX_BLOCK_STEPS
    n_sub = MIX_BLOCK_STEPS // MIX_STEPS
    n_s = xs.shape[0]
    hist_rows = (CONV_B_WIDTH - 1) * batch
    hbm_spec = pl.BlockSpec(memory_space=pl.ANY)
    hist_shape = jax.ShapeDtypeStruct((hist_rows, D_CONV), F32)
    h_shape = jax.ShapeDtypeStruct((batch, D_LRU), F32)
    q_shape = jax.ShapeDtypeStruct((n_s, HEAD_ROWS, LANES), F32)
    like = lambda a: jax.ShapeDtypeStruct(a.shape, F32)
    full = lambda a: _const_spec(a.shape)
    weight_specs = [_const_spec((1, D_MODEL)), _const_spec((1, D_MODEL)), _const_spec((D_MODEL, N_MIX_PARTS * D_CONV)),
                    _const_spec((CONV_A_WIDTH, D_CONV)), _const_spec((CONV_B_WIDTH, D_LRU)), _const_spec((1, D_LRU)),
                    _const_spec((N_LRU_HEADS, LRU_HEAD_DIM, LRU_HEAD_DIM)),
                    _const_spec((N_LRU_HEADS, LRU_HEAD_DIM, LRU_HEAD_DIM)),
                    _const_spec((1, D_LRU)), _const_spec((1, D_LRU)), _const_spec((1, D_LRU)),
                    _const_spec((D_MODEL, D_MODEL)), _const_spec((1, D_MODEL)), _const_spec((D_MODEL, D_MODEL))]
    return pl.pallas_call(
        functools.partial(_mix_kernel, n_tiles=n_tiles),
        grid=(n_tiles + 2,),
        in_specs=[hbm_spec, full(xs), full(sa), full(sb), full(h0)] + weight_specs,
        out_specs=[hbm_spec, full(hist_shape), full(hist_shape), full(h_shape),
                   full(xs), full(sa), full(sb), full(h0), full(q_shape)],
        out_shape=[like(x), hist_shape, hist_shape, h_shape, like(xs), like(sa), like(sb), like(h0), q_shape],
        scratch_shapes=[pltpu.VMEM(hist_shape.shape, F32), pltpu.VMEM(hist_shape.shape, F32),
                        pltpu.VMEM(h_shape.shape, F32),
                        pltpu.VMEM((2, n_sub, N_MIX_CHUNKS, MIX_STEPS * batch, N_MIX_PARTS * MIX_CHUNK), F32),
                        pltpu.VMEM((3, MIX_BLOCK_STEPS, batch, D_MODEL), F32),
                        pltpu.VMEM((2, MIX_BLOCK_STEPS, batch, D_MODEL), F32),
                        pltpu.SemaphoreType.DMA((3, batch)), pltpu.SemaphoreType.DMA((2, batch)),
                        pltpu.VMEM((D_MODEL, N_MIX_PARTS * D_CONV), BF16),
                        pltpu.VMEM((N_MIX_CHUNKS, MIX_CHUNK, 2 * MIX_CHUNK), BF16),
                        pltpu.VMEM((D_MODEL, D_MODEL), BF16)],
        compiler_params=_params("arbitrary"),
        name="mix",
    )(x, xs, sa, sb, h0, *weights, g_q, wq)


def _head_row_copies(buf_ref, hbm_ref, sem_ref, which, slot, step):
    copies = []
    for hd in range(N_XHEADS):
        for half in range(XHEAD_DIM // LANES):
            col = (hd * (XHEAD_DIM // LANES) + half) * LANES
            sub = half * N_XHEADS + hd
            src = buf_ref.at[slot, :, pl.ds(col, LANES)]
            dst = hbm_ref.at[pl.ds(step * KV_ROW_TILE, KV_ROW_TILE), sub, :]
            copies.append(pltpu.make_async_copy(src, dst, sem_ref.at[which, slot, sub]))
    return copies


def _kv_kernel(m_ref, g_ref, wk_ref, wv_ref, kb_ref, vb_ref, kt_hbm, vt_hbm, kbuf_ref, vbuf_ref, sem_ref, *, n_steps):
    i = pl.program_id(0)
    slot = i % 2
    m = _rms(m_ref[...], g_ref[...]).astype(BF16)
    k = _dot(m, wk_ref[...].astype(BF16))
    v = _dot(m, wv_ref[...].astype(BF16))
    kb_ref[...] = k.astype(BF16)
    vb_ref[...] = v.astype(BF16)
    kbuf_ref[slot] = k
    vbuf_ref[slot] = v

    def copies(s, step):
        return (_head_row_copies(kbuf_ref, kt_hbm, sem_ref, 0, s, step)
                + _head_row_copies(vbuf_ref, vt_hbm, sem_ref, 1, s, step))

    for cp in copies(slot, i):
        cp.start()

    @pl.when(i >= 1)
    def _():
        for cp in copies(1 - slot, i - 1):
            cp.wait()

    @pl.when(i == n_steps - 1)
    def _():
        for cp in copies(slot, i):
            cp.wait()


def _memory_kv(mem, g_mem, wk, wv):
    rows = mem.shape[0]
    n_steps = rows // KV_ROW_TILE
    row_spec = pl.BlockSpec((KV_ROW_TILE, D_MODEL), lambda i: (i, 0))
    hbm_spec = pl.BlockSpec(memory_space=pl.ANY)
    natural = jax.ShapeDtypeStruct((rows, D_MODEL), BF16)
    tiled = jax.ShapeDtypeStruct((rows, HEAD_ROWS, LANES), F32)
    buf = pltpu.VMEM((2, KV_ROW_TILE, D_MODEL), F32)
    return pl.pallas_call(
        functools.partial(_kv_kernel, n_steps=n_steps),
        grid=(n_steps,),
        in_specs=[row_spec, _const_spec((1, D_MODEL)), _const_spec((D_MODEL, D_MODEL)),
                  _const_spec((D_MODEL, D_MODEL))],
        out_specs=[row_spec, row_spec, hbm_spec, hbm_spec],
        out_shape=[natural, natural, tiled, tiled],
        scratch_shapes=[buf, buf, pltpu.SemaphoreType.DMA((2, 2, HEAD_ROWS))],
        compiler_params=_params("arbitrary"),
        name="memory_kv",
    )(mem, g_mem, wk, wv)


def _merge_heads(a, lead_shape):
    a = a.reshape(lead_shape + (XHEAD_DIM // LANES, N_XHEADS, LANES))
    return jnp.swapaxes(a, -3, -2).reshape(lead_shape + (N_XHEADS, XHEAD_DIM))


def _softmax_rows(s):
    e = jnp.exp(s - jnp.max(s, axis=-1, keepdims=True))
    return e * (1.0 / jnp.sum(e, axis=-1, keepdims=True))


def _split_heads(a):
    n = a.shape[0]
    a = a.reshape(n, -1, N_XHEADS, 2, LANES)
    return jnp.swapaxes(a, 2, 3).reshape(n, -1, LANES)


def _attend_one(qv, k, v, ones_ref):
    rows = N_MEM * HEAD_ROWS
    prod = (k.reshape(N_MEM, HEAD_ROWS, LANES) * (qv * (XHEAD_DIM ** -0.5))[None]).reshape(rows, LANES).astype(BF16)
    pair = jnp.concatenate([prod[:rows // 2], prod[rows // 2:]], axis=-1)
    sums = _dot(pair, ones_ref[...])
    part = jnp.concatenate([sums[:, :LANES], sums[:, LANES:]], axis=0).reshape(N_MEM, HEAD_ROWS, LANES)
    s = part + pltpu.roll(part, N_XHEADS, 1)
    e = jnp.exp(s - jnp.max(s, axis=0, keepdims=True))
    den = jnp.sum(e, axis=0)
    num = jnp.sum(e * v.reshape(N_MEM, HEAD_ROWS, LANES), axis=0)
    return num * (1.0 / den)


def _xattn_kernel(x_ref, k_ref, v_ref, xs_ref, qs_ref, ones_ref, gpre_ref, gpost_ref, wq_ref, wo_ref, ck_ref, cv_ref,
                  o_ref, os_ref, attn_ref, wqb_ref, wob_ref, *, n_prompt):
    i = pl.program_id(0)

    @pl.when(i == 0)
    def _():
        wqb_ref[...] = wq_ref[...].astype(BF16)
        wob_ref[...] = wo_ref[...].astype(BF16)

    @pl.when(i < n_prompt)
    def _():
        def project(h, w_ref, b, as_bf16):
            out = _dot(h, w_ref[...])
            row = i * XATTN_SAMPLES_PER_STEP + b
            attn_ref[row] = _attend_one(qs_ref[row], ck_ref[b], cv_ref[b], ones_ref)
            return out.astype(BF16) if as_bf16 else out

        def scores(q):
            return [lax.dot_general(q[:, hd * XHEAD_DIM:(hd + 1) * XHEAD_DIM], k_ref[0, :, hd * XHEAD_DIM:(hd + 1) * XHEAD_DIM],
                                    (((1,), (1,)), ((), ())), preferred_element_type=F32) for hd in range(N_XHEADS)]

        def weighted(ss):
            ps = [_softmax_rows(s * (XHEAD_DIM ** -0.5)).astype(BF16) for s in ss]
            return jnp.concatenate([_dot(p, v_ref[0, :, hd * XHEAD_DIM:(hd + 1) * XHEAD_DIM]).astype(BF16)
                                    for hd, p in enumerate(ps)], axis=-1)

        ra, rb = (slice(u * XATTN_SUB_ROWS, (u + 1) * XATTN_SUB_ROWS) for u in range(2))
        xa, xb = x_ref[ra, :], x_ref[rb, :]
        ha = _rms(xa, gpre_ref[...]).astype(BF16)
        hb = _rms(xb, gpre_ref[...]).astype(BF16)
        qa = project(ha, wqb_ref, 0, True)
        sa = scores(qa)
        qb = project(hb, wqb_ref, 1, True)
        aa = weighted(sa)
        sb = scores(qb)
        ya = project(aa, wob_ref, 2, False)
        ab = weighted(sb)
        yb = project(ab, wob_ref, 3, False)
        o_ref[ra, :] = xa + _rms(ya, gpost_ref[...])
        o_ref[rb, :] = xb + _rms(yb, gpost_ref[...])

    @pl.when(i == n_prompt)
    def _():
        attn = jnp.concatenate([attn_ref[:, part * N_XHEADS + hd, :] for hd in range(N_XHEADS)
                                for part in range(XHEAD_DIM // LANES)], axis=-1)
        y = _dot(attn.astype(BF16), wob_ref[...])
        os_ref[...] = xs_ref[...] + _rms(y, gpost_ref[...])


def _xattn(x, mem_k, mem_v, xs, q_s, cache_k, cache_v, g_pre, g_post, wq, wo, batch, seq):
    n_t = seq // XATTN_ROW_TILE
    n_prompt = batch * n_t
    n_s = xs.shape[0]
    assert n_prompt * XATTN_SAMPLES_PER_STEP == n_s and XATTN_ROW_TILE == 2 * XATTN_SUB_ROWS
    tile = lambda i: jnp.minimum(i, n_prompt - 1)
    row_spec = pl.BlockSpec((XATTN_ROW_TILE, D_MODEL), lambda i: (tile(i), 0))
    kv_spec = pl.BlockSpec((1, N_MEM, D_MODEL), lambda i: (tile(i) // n_t, 0, 0))
    cache_spec = pl.BlockSpec((XATTN_SAMPLES_PER_STEP, N_MEM * HEAD_ROWS, LANES), lambda i: (tile(i), 0, 0))
    eye2 = jnp.eye(2, dtype=BF16)
    ones = jnp.kron(eye2, jnp.ones((LANES, LANES), BF16))
    return pl.pallas_call(
        functools.partial(_xattn_kernel, n_prompt=n_prompt),
        grid=(n_prompt + 1,),
        in_specs=[row_spec, kv_spec, kv_spec, _const_spec(xs.shape), _const_spec(q_s.shape),
                  _const_spec((2 * LANES, 2 * LANES)), _const_spec((1, D_MODEL)), _const_spec((1, D_MODEL)),
                  _const_spec((D_MODEL, D_MODEL)), _const_spec((D_MODEL, D_MODEL)), cache_spec, cache_spec],
        out_specs=[row_spec, _const_spec(xs.shape)],
        out_shape=[jax.ShapeDtypeStruct(x.shape, F32), jax.ShapeDtypeStruct(xs.shape, F32)],
        scratch_shapes=[pltpu.VMEM((n_s, HEAD_ROWS, LANES), F32),
                        pltpu.VMEM((D_MODEL, D_MODEL), BF16), pltpu.VMEM((D_MODEL, D_MODEL), BF16)],
        compiler_params=_params("arbitrary"),
        name="xattn",
    )(x, mem_k, mem_v, xs, q_s, ones, g_pre, g_post, wq, wo, cache_k, cache_v)


def kernel(x_prompt, x_sample, mem_prompt, cache_mem_k, cache_mem_v, state_conv_a, state_conv_b, state_lru, g_ffn1_pre, g_ffn1_post, ffn1_wg, ffn1_wu, ffn1_wd, g_mix_pre, g_mix_post, w_in, conv_a_w, conv_b_w, conv_b_b, lru_wa, lru_ba, lru_wx, lru_bx, lru_lam, w_out, g_xattn_pre, g_xattn_post, g_mem, xattn_wq, xattn_wk, xattn_wv, xattn_wo, g_ffn2_pre, g_ffn2_post, ffn2_wg, ffn2_wu, ffn2_wd):
    batch, seq, _ = x_prompt.shape
    n_s = x_sample.shape[0]
    depth = g_ffn1_pre.shape[0]
    assert depth == 1 and x_sample.shape[1] == 1
    l = 0
    row = lambda p: p[l].reshape(1, -1)

    yp = x_prompt.reshape(batch * seq, D_MODEL)
    ys = x_sample.reshape(n_s, D_MODEL // LANES, LANES)
    sa = state_conv_a[l].reshape(n_s, CONV_A_WIDTH - 1, D_CONV // LANES, LANES)
    sa = jnp.swapaxes(sa, 1, 2).reshape(n_s, (D_CONV // LANES) * (CONV_A_WIDTH - 1), LANES)
    sb = jnp.swapaxes(state_conv_b[l], 0, 1)

    ffn1 = (row(g_ffn1_pre), row(g_ffn1_post), ffn1_wg[l], ffn1_wu[l], ffn1_wd[l])
    ffn2 = (row(g_ffn2_pre), row(g_ffn2_post), ffn2_wg[l], ffn2_wu[l], ffn2_wd[l])
    mix_w = (row(g_mix_pre), row(g_mix_post), w_in[l], conv_a_w[l], conv_b_w[l], row(conv_b_b), lru_wa[l], lru_wx[l],
             row(lru_ba), row(lru_bx), row(lru_lam), w_out[l])

    mk_b, mv_b, mk, mv = _memory_kv(mem_prompt.reshape(batch * N_MEM, D_MODEL), row(g_mem), xattn_wk[l], xattn_wv[l])

    yp, ys = _ffn(yp, ys, *ffn1, sample_out_tiles=False)

    yp, tail_a, tail_b, tail_h, ys, new_a, new_b, new_h, q_s = _mix(
        yp.reshape(batch, seq, D_MODEL), ys, sa, sb, state_lru[l], mix_w, row(g_xattn_pre), xattn_wq[l], batch, seq)
    yp = yp.reshape(batch * seq, D_MODEL)
    tail_a = tail_a.reshape(CONV_B_WIDTH - 1, batch, D_CONV)[CONV_B_WIDTH - CONV_A_WIDTH:].transpose(1, 0, 2)
    tail_b = tail_b.reshape(CONV_B_WIDTH - 1, batch, D_LRU).transpose(1, 0, 2)
    new_a = jnp.swapaxes(new_a.reshape(n_s, D_CONV // LANES, CONV_A_WIDTH - 1, LANES), 1, 2)
    new_a = new_a.reshape(1, n_s, CONV_A_WIDTH - 1, D_CONV)
    new_b = jnp.swapaxes(new_b, 0, 1)[None]

    yp, ys = _xattn(yp, mk_b.reshape(batch, N_MEM, D_MODEL), mv_b.reshape(batch, N_MEM, D_MODEL), ys, q_s,
                    _split_heads(cache_mem_k[l]), _split_heads(cache_mem_v[l]),
                    row(g_xattn_pre), row(g_xattn_post), xattn_wq[l], xattn_wo[l], batch, seq)

    yp, ys = _ffn(yp, ys, *ffn2, sample_out_tiles=True)

    kv_lead = (1, batch, N_MEM)
    return (yp.reshape(batch, seq, D_MODEL), ys.reshape(n_s, 1, D_MODEL),
            _merge_heads(mk, kv_lead), _merge_heads(mv, kv_lead),
            tail_a[None], tail_b[None], tail_h[None], new_a, new_b, new_h[None])
```

```python
import functools
import math

import jax
import jax.numpy as jnp
from jax import lax
from jax.experimental import pallas as pl
from jax.experimental.pallas import tpu as pltpu

D_MODEL = 1024
D_CONV = 512
D_LRU = 512
N_LRU_HEADS = 8
LRU_HEAD_DIM = D_LRU // N_LRU_HEADS
LRU_C = 8.0
CONV_A_WIDTH = 3
CONV_B_WIDTH = 4
N_MIX_PARTS = 5
FFN_DIM = 2816
N_MEM = 256
N_XHEADS = 4
XHEAD_DIM = D_MODEL // N_XHEADS
RMS_EPS = 1e-6

F32 = jnp.float32
BF16 = jnp.bfloat16

V7X_VMEM_LIMIT_BYTES = 56 * 1024 * 1024
SUBLANES = 8
LANES = 128
HEAD_ROWS = N_XHEADS * XHEAD_DIM // LANES

FFN_ROW_TILE = 1024
FFN_COL_CHUNK = 256
FFN_CHUNKS = FFN_DIM // FFN_COL_CHUNK
MIX_STEPS = 32
MIX_BLOCK_STEPS = 64
MIX_CHUNK = 256
N_MIX_CHUNKS = D_CONV // MIX_CHUNK
XATTN_ROW_TILE = 512
XATTN_SUB_ROWS = 256
KV_ROW_TILE = 512
XATTN_SAMPLES_PER_STEP = 4


def _rms(x, g):
    y = x * lax.rsqrt(jnp.mean(x * x, axis=-1, keepdims=True) + RMS_EPS)
    return y * g


def _dot(a, b):
    return jnp.dot(a, b, preferred_element_type=F32)


def _sigmoid(x):
    return 1.0 / (1.0 + jnp.exp(-x))


def _gelu_tanh(x):
    c = math.sqrt(2.0 / math.pi)
    neg_two_z = x * ((-2.0 * c * 0.044715) * (x * x) - 2.0 * c)
    return x / (1.0 + jnp.exp(neg_two_z))


def _one_minus_exp2x(x):
    t = jnp.tanh(x)
    return (-2.0 * t) / (1.0 - t)


def _log1p(w):
    u = 1.0 + w
    return jnp.where(u == 1.0, w, jnp.log(u) * w / (u - 1.0))


def _softplus(x):
    return jnp.maximum(x, 0.0) + _log1p(jnp.exp(-jnp.abs(x)))


def _const_spec(shape):
    zeros = (0,) * len(shape)
    return pl.BlockSpec(shape, lambda *_: zeros, pipeline_mode=pl.Buffered(1))


def _params(*sem):
    return pltpu.CompilerParams(dimension_semantics=sem, vmem_limit_bytes=V7X_VMEM_LIMIT_BYTES)


def _lane_tiles_to_rows(ref):
    return jnp.concatenate([ref[:, j, :] for j in range(ref.shape[1])], axis=-1)


def _rows_to_lane_tiles(ref, val):
    for j in range(ref.shape[1]):
        ref[:, j, :] = val[:, j * LANES:(j + 1) * LANES]


def _ffn_up(h, wgb_ref, wub_ref, act_ref, c):
    rows = h.shape[0]
    sl = slice(c * FFN_COL_CHUNK, (c + 1) * FFN_COL_CHUNK)
    g = _dot(h, wgb_ref[:, sl])
    u = _dot(h, wub_ref[:, sl])
    act_ref[0:rows, sl] = ((g * _sigmoid(g)) * u).astype(BF16)


def _ffn_rows(x, gpre_ref, gpost_ref, wgb_ref, wub_ref, wdb_ref, act_ref, before_chunk=None, before_down=None):
    rows = x.shape[0]
    h = _rms(x, gpre_ref[...]).astype(BF16)
    for c in range(FFN_CHUNKS):
        if before_chunk is not None:
            before_chunk(c)
        _ffn_up(h, wgb_ref, wub_ref, act_ref, c)
    if before_down is not None:
        before_down()
    y = _dot(act_ref[0:rows, :], wdb_ref[...])
    return x + 0.5 * _rms(y, gpost_ref[...])


def _ffn_weight_copy(hbm_ref, stage_ref, sem_ref, which, c, by_columns):
    if by_columns:
        src = hbm_ref.at[:, pl.ds(c * FFN_COL_CHUNK, FFN_COL_CHUNK)]
    else:
        src = hbm_ref.at[pl.ds(c * FFN_COL_CHUNK, FFN_COL_CHUNK), :]
    return pltpu.make_async_copy(src, stage_ref.at[c % 2], sem_ref.at[which, c % 2])


def _ffn_kernel(xp_ref, xs_ref, gpre_ref, gpost_ref, wg_hbm, wu_hbm, wd_hbm, op_ref, os_ref,
                wgb_ref, wub_ref, wdb_ref, sg_ref, su_ref, sd_ref, sem_ref, act_ref, *, n_prompt):
    i = pl.program_id(0)
    weights = (gpre_ref, gpost_ref, wgb_ref, wub_ref, wdb_ref, act_ref)
    mats = ((wg_hbm, sg_ref, wgb_ref, True), (wu_hbm, su_ref, wub_ref, True), (wd_hbm, sd_ref, wdb_ref, False))

    def copy(m, c):
        hbm, stage, _, by_columns = mats[m]
        return _ffn_weight_copy(hbm, stage, sem_ref, m, c, by_columns)

    def land(m, c):
        _, stage, dst, by_columns = mats[m]
        copy(m, c).wait()
        sl = slice(c * FFN_COL_CHUNK, (c + 1) * FFN_COL_CHUNK)
        if by_columns:
            dst[:, sl] = stage[c % 2].astype(BF16)
        else:
            dst[sl, :] = stage[c % 2].astype(BF16)
        if c + 2 < FFN_CHUNKS:
            copy(m, c + 2).start()

    @pl.when(i == 0)
    def _():
        for c in range(2):
            for m in range(3):
                copy(m, c).start()

        def before_chunk(c):
            for m in range(3):
                land(m, c)

        op_ref[...] = _ffn_rows(xp_ref[...], *weights, before_chunk=before_chunk)

    @pl.when((i > 0) & (i < n_prompt))
    def _():
        op_ref[...] = _ffn_rows(xp_ref[...], *weights)

    @pl.when(i == n_prompt)
    def _():
        xs = _lane_tiles_to_rows(xs_ref) if len(xs_ref.shape) == 3 else xs_ref[...]
        out = _ffn_rows(xs, *weights)
        if len(os_ref.shape) == 3:
            _rows_to_lane_tiles(os_ref, out)
        else:
            os_ref[...] = out


def _ffn(xp, xs, g_pre, g_post, wg, wu, wd, sample_out_tiles):
    n_prompt = xp.shape[0] // FFN_ROW_TILE
    n_s = xs.shape[0]
    out_s = (n_s, D_MODEL // LANES, LANES) if sample_out_tiles else (n_s, D_MODEL)
    prompt_spec = pl.BlockSpec((FFN_ROW_TILE, D_MODEL), lambda i: (jnp.minimum(i, n_prompt - 1), 0))
    hbm_spec = pl.BlockSpec(memory_space=pl.ANY)
    return pl.pallas_call(
        functools.partial(_ffn_kernel, n_prompt=n_prompt),
        grid=(n_prompt + 1,),
        in_specs=[prompt_spec, _const_spec(xs.shape), _const_spec((1, D_MODEL)), _const_spec((1, D_MODEL)),
                  hbm_spec, hbm_spec, hbm_spec],
        out_specs=[prompt_spec, _const_spec(out_s)],
        out_shape=[jax.ShapeDtypeStruct(xp.shape, F32), jax.ShapeDtypeStruct(out_s, F32)],
        scratch_shapes=[pltpu.VMEM((D_MODEL, FFN_DIM), BF16), pltpu.VMEM((D_MODEL, FFN_DIM), BF16),
                        pltpu.VMEM((FFN_DIM, D_MODEL), BF16),
                        pltpu.VMEM((2, D_MODEL, FFN_COL_CHUNK), F32), pltpu.VMEM((2, D_MODEL, FFN_COL_CHUNK), F32),
                        pltpu.VMEM((2, FFN_COL_CHUNK, D_MODEL), F32),
                        pltpu.SemaphoreType.DMA((3, 2)),
                        pltpu.VMEM((FFN_ROW_TILE, FFN_DIM), BF16)],
        compiler_params=_params("arbitrary"),
        name="ffn",
    )(xp, xs, g_pre, g_post, wg, wu, wd)


def _chunk(ref, r, c):
    return ref[r:r + 1, c * MIX_CHUNK:(c + 1) * MIX_CHUNK]


def _mix_project(h, win_ref, c):
    parts = []
    for s in range(N_MIX_PARTS):
        o = s * D_CONV + c * MIX_CHUNK
        parts.append(_dot(h, win_ref[:, o:o + MIX_CHUNK].astype(BF16)))
    return jnp.concatenate(parts, axis=-1)


def _mix_out(ya, yb, wout_ref, c):
    ra = c * MIX_CHUNK
    rb = D_CONV + c * MIX_CHUNK
    return (_dot(ya.astype(BF16), wout_ref[ra:ra + MIX_CHUNK, :].astype(BF16))
            + _dot(yb.astype(BF16), wout_ref[rb:rb + MIX_CHUNK, :].astype(BF16)))


def _mix_front(zc, c, prev_a, prev_b, caw_ref, cbw_ref, cbb_ref, wax_ref):
    gb, gc, xa, xb, gg = (zc[:, s * MIX_CHUNK:(s + 1) * MIX_CHUNK] for s in range(N_MIX_PARTS))
    v = gc * xa
    ca = prev_a(v, 2) * _chunk(caw_ref, 0, c) + prev_a(v, 1) * _chunk(caw_ref, 1, c) + v * _chunk(caw_ref, 2, c)
    ya = gb * ca
    cb = (prev_b(xb, 3) * _chunk(cbw_ref, 0, c) + prev_b(xb, 2) * _chunk(cbw_ref, 1, c)
          + prev_b(xb, 1) * _chunk(cbw_ref, 2, c) + xb * _chunk(cbw_ref, 3, c))
    cb = cb + _chunk(cbb_ref, 0, c)
    gates = _dot(cb.astype(BF16), wax_ref[c].astype(BF16))
    return v, xb, ya, cb, gates, gg


def _lru_coeffs(gates, c, ba_ref, bx_ref, softplus_neg_lam):
    r = _sigmoid(gates[:, :MIX_CHUNK] + _chunk(ba_ref, 0, c))
    i = _sigmoid(gates[:, MIX_CHUNK:] + _chunk(bx_ref, 0, c))
    log_a = (-LRU_C * r) * softplus_neg_lam[:, c * MIX_CHUNK:(c + 1) * MIX_CHUNK]
    a = jnp.exp(log_a)
    mult = jnp.sqrt(_one_minus_exp2x(log_a))
    return a, mult, i


def _seq_copies(hbm_ref, buf_ref, sem_ref, slot, step_block, to_vmem):
    copies = []
    for b in range(SUBLANES):
        hbm = hbm_ref.at[b, pl.ds(step_block * MIX_BLOCK_STEPS, MIX_BLOCK_STEPS), :]
        vmem = buf_ref.at[slot, :, b, :]
        src, dst = (hbm, vmem) if to_vmem else (vmem, hbm)
        copies.append(pltpu.make_async_copy(src, dst, sem_ref.at[slot, b]))
    return copies


def _mix_kernel(x_hbm, xs_ref, sa_ref, sb_ref, h0_ref, gpre_ref, gpost_ref, win_ref, caw_ref, cbw_ref, cbb_ref,
                wa_ref, wx_ref, ba_ref, bx_ref, lam_ref, wout_ref, gq_ref, wq_ref,
                o_hbm, ta_ref, tb_ref, hl_ref, os_ref, na_ref, nb_ref, hs_ref, qs_ref,
                hista_ref, histb_ref, hc_ref, z_ref, xbuf_ref, obuf_ref, xsem_ref, osem_ref, winb_ref, waxb_ref,
                woutb_ref, *, n_tiles):
    i = pl.program_id(0)
    nb = SUBLANES
    rows = MIX_STEPS * nb
    hist_rows = (CONV_B_WIDTH - 1) * nb
    items = [(u, c) for u in range(MIX_BLOCK_STEPS // MIX_STEPS) for c in range(N_MIX_CHUNKS)]

    @pl.when(i == 0)
    def _():
        hista_ref[...] = jnp.zeros_like(hista_ref)
        histb_ref[...] = jnp.zeros_like(histb_ref)
        hc_ref[...] = jnp.zeros_like(hc_ref)
        for cp in _seq_copies(x_hbm, xbuf_ref, xsem_ref, 0, 0, True):
            cp.start()
        winb_ref[...] = win_ref[...].astype(BF16)
        woutb_ref[...] = wout_ref[...].astype(BF16)
        waxb_ref[...] = jnp.zeros_like(waxb_ref)
        heads_per_chunk = MIX_CHUNK // LRU_HEAD_DIM
        for hd in range(N_LRU_HEADS):
            c, j = divmod(hd, heads_per_chunk)
            rsl = slice(j * LRU_HEAD_DIM, (j + 1) * LRU_HEAD_DIM)
            waxb_ref[c, rsl, rsl] = wa_ref[hd].astype(BF16)
            waxb_ref[c, rsl, MIX_CHUNK + j * LRU_HEAD_DIM:MIX_CHUNK + (j + 1) * LRU_HEAD_DIM] = wx_ref[hd].astype(BF16)

    @pl.when(i + 1 < n_tiles)
    def _():
        for cp in _seq_copies(x_hbm, xbuf_ref, xsem_ref, (i + 1) % 3, i + 1, True):
            cp.start()

    @pl.when(i < n_tiles)
    def _():
        for cp in _seq_copies(x_hbm, xbuf_ref, xsem_ref, i % 3, i, True):
            cp.wait()

    @pl.when((i >= 3) & (i <= n_tiles))
    def _():
        for cp in _seq_copies(o_hbm, obuf_ref, osem_ref, (i - 3) % 2, i - 3, False):
            cp.wait()

    def prev(hist):
        return lambda cur, k: jnp.concatenate([hist[hist_rows - k * nb:, :], cur[:rows - k * nb, :]], axis=0)

    def x_rows(tile, u):
        return xbuf_ref[tile % 3, u * MIX_STEPS:(u + 1) * MIX_STEPS].reshape(rows, D_MODEL)

    def run(proj_tile, mix_tile):
        proj_tile, proj_slot = proj_tile if proj_tile is not None else (None, None)
        mix_tile, mix_slot = mix_tile if mix_tile is not None else (None, None)
        normed = {}
        if mix_tile is not None:
            sp = _softplus(-lam_ref[...])
            row = lax.broadcasted_iota(jnp.int32, (rows, MIX_CHUNK), 0)
            first_rows = jnp.where(mix_tile == 0, nb, 0)
            hist_a = hista_ref[...]
            hist_b = histb_ref[...]
            hcur = hc_ref[...]

        def project(u, c):
            if u not in normed:
                normed[u] = _rms(x_rows(proj_tile, u), gpre_ref[...]).astype(BF16)
            z_ref[proj_slot, u, c] = _mix_project(normed[u], winb_ref, c)

        for u, c in items:
            if mix_tile is None:
                project(u, c)
                continue
            if c == 0:
                y = None
                tails_a, tails_b, h_last = [], [], []
            csl = slice(c * MIX_CHUNK, (c + 1) * MIX_CHUNK)
            zc = z_ref[mix_slot, u, c]
            v, xb, ya, cb, gates, gg = _mix_front(zc, c, prev(hist_a[:, csl]), prev(hist_b[:, csl]),
                                                  caw_ref, cbw_ref, cbb_ref, waxb_ref)
            if proj_tile is not None:
                project(u, c)
            a, mult, gate_in = _lru_coeffs(gates, c, ba_ref, bx_ref, sp)
            if u == 0:
                mult = jnp.where(row < first_rows, 1.0, mult)
            b = (mult * gate_in) * cb
            hc = hcur[:, csl]
            hs = []
            for j in range(MIX_STEPS):
                sl = slice(j * nb, (j + 1) * nb)
                hc = a[sl, :] * hc + b[sl, :]
                hs.append(hc)
            yb = _gelu_tanh(gg) * jnp.concatenate(hs, axis=0)
            yc = _mix_out(ya, yb, woutb_ref, c)
            y = yc if y is None else y + yc
            tails_a.append(v[rows - hist_rows:, :])
            tails_b.append(xb[rows - hist_rows:, :])
            h_last.append(hc)
            if c == N_MIX_CHUNKS - 1:
                hist_a = jnp.concatenate(tails_a, axis=-1)
                hist_b = jnp.concatenate(tails_b, axis=-1)
                hcur = jnp.concatenate(h_last, axis=-1)
                out = x_rows(mix_tile, u) + _rms(y, gpost_ref[...])
                obuf_ref[mix_tile % 2, u * MIX_STEPS:(u + 1) * MIX_STEPS] = out.reshape(MIX_STEPS, nb, D_MODEL)
        if mix_tile is not None:
            hista_ref[...] = hist_a
            histb_ref[...] = hist_b
            hc_ref[...] = hcur

    @pl.when(i == 0)
    def _():
        run((i, 0), None)

    for parity in range(2):
        @pl.when((i > 0) & (i < n_tiles) & (i % 2 == parity))
        def _():
            run((i, parity), (i - 1, 1 - parity))

    @pl.when(i == n_tiles)
    def _():
        run(None, (i - 1, (n_tiles - 1) % 2))
        ta_ref[...] = hista_ref[...]
        tb_ref[...] = histb_ref[...]
        hl_ref[...] = hc_ref[...]

    @pl.when(i == n_tiles + 1)
    def _():
        _mix_sample_rows(xs_ref, sa_ref, sb_ref, h0_ref, gpre_ref, gpost_ref, winb_ref, caw_ref, cbw_ref, cbb_ref,
                         waxb_ref, ba_ref, bx_ref, lam_ref, woutb_ref, gq_ref, wq_ref,
                         os_ref, na_ref, nb_ref, hs_ref, qs_ref)

    @pl.when((i >= 1) & (i <= n_tiles))
    def _():
        for cp in _seq_copies(o_hbm, obuf_ref, osem_ref, (i - 1) % 2, i - 1, False):
            cp.start()

    @pl.when(i == n_tiles)
    def _():
        for tile in range(max(n_tiles - 2, 0), n_tiles):
            for cp in _seq_copies(o_hbm, obuf_ref, osem_ref, tile % 2, tile, False):
                cp.wait()


def _mix_sample_rows(x_ref, sa_ref, sb_ref, h0_ref, gpre_ref, gpost_ref, win_ref, caw_ref, cbw_ref, cbb_ref,
                     wax_ref, ba_ref, bx_ref, lam_ref, wout_ref, gq_ref, wq_ref,
                     o_ref, na_ref, nb_ref, hl_ref, q_ref):
    tiles_per_chunk = MIX_CHUNK // LANES
    n_a = CONV_A_WIDTH - 1
    x = x_ref[...]
    h = _rms(x, gpre_ref[...]).astype(BF16)
    sp = _softplus(-lam_ref[...])
    y = None
    for c in range(N_MIX_CHUNKS):
        csl = slice(c * MIX_CHUNK, (c + 1) * MIX_CHUNK)
        lane_tiles = range(c * tiles_per_chunk, (c + 1) * tiles_per_chunk)
        zc = _mix_project(h, win_ref, c)

        def state_a(k):
            return jnp.concatenate([sa_ref[:, j * n_a + (n_a - k), :] for j in lane_tiles], axis=-1)

        def state_b(k):
            return sb_ref[CONV_B_WIDTH - 1 - k, :, csl]

        v, xb, ya, cb, gates, gg = _mix_front(zc, c, lambda cur, k: state_a(k), lambda cur, k: state_b(k),
                                              caw_ref, cbw_ref, cbb_ref, wax_ref)
        a, mult, gate_in = _lru_coeffs(gates, c, ba_ref, bx_ref, sp)
        hnew = a * h0_ref[:, csl] + (mult * gate_in) * cb
        yb = _gelu_tanh(gg) * hnew
        yc = _mix_out(ya, yb, wout_ref, c)
        y = yc if y is None else y + yc

        for t, j in enumerate(lane_tiles):
            for age in range(n_a - 1):
                na_ref[:, j * n_a + age, :] = sa_ref[:, j * n_a + age + 1, :]
            na_ref[:, j * n_a + n_a - 1, :] = v[:, t * LANES:(t + 1) * LANES]
        for age in range(CONV_B_WIDTH - 2):
            nb_ref[age, :, csl] = sb_ref[age + 1, :, csl]
        nb_ref[CONV_B_WIDTH - 2, :, csl] = xb
        hl_ref[:, csl] = hnew
    out = x + _rms(y, gpost_ref[...])
    o_ref[...] = out
    q = _dot(_rms(out, gq_ref[...]).astype(BF16), wq_ref[...].astype(BF16))
    for hd in range(N_XHEADS):
        for half in range(XHEAD_DIM // LANES):
            col = hd * XHEAD_DIM + half * LANES
            q_ref[:, half * N_XHEADS + hd, :] = q[:, col:col + LANES]


def _mix(x, xs, sa, sb, h0, weights, g_q, wq, batch, seq):
    assert batch == SUBLANES and seq % MIX_BLOCK_STEPS == 0
    n_tiles = seq // MIX_BLOCK_STEPS
    n_sub = MIX_BLOCK_STEPS // MIX_STEPS
    n_s = xs.shape[0]
    hist_rows = (CONV_B_WIDTH - 1) * batch
    hbm_spec = pl.BlockSpec(memory_space=pl.ANY)
    hist_shape = jax.ShapeDtypeStruct((hist_rows, D_CONV), F32)
    h_shape = jax.ShapeDtypeStruct((batch, D_LRU), F32)
    q_shape = jax.ShapeDtypeStruct((n_s, HEAD_ROWS, LANES), F32)
    like = lambda a: jax.ShapeDtypeStruct(a.shape, F32)
    full = lambda a: _const_spec(a.shape)
    weight_specs = [_const_spec((1, D_MODEL)), _const_spec((1, D_MODEL)), _const_spec((D_MODEL, N_MIX_PARTS * D_CONV)),
                    _const_spec((CONV_A_WIDTH, D_CONV)), _const_spec((CONV_B_WIDTH, D_LRU)), _const_spec((1, D_LRU)),
                    _const_spec((N_LRU_HEADS, LRU_HEAD_DIM, LRU_HEAD_DIM)),
                    _const_spec((N_LRU_HEADS, LRU_HEAD_DIM, LRU_HEAD_DIM)),
                    _const_spec((1, D_LRU)), _const_spec((1, D_LRU)), _const_spec((1, D_LRU)),
                    _const_spec((D_MODEL, D_MODEL)), _const_spec((1, D_MODEL)), _const_spec((D_MODEL, D_MODEL))]
    return pl.pallas_call(
        functools.partial(_mix_kernel, n_tiles=n_tiles),
        grid=(n_tiles + 2,),
        in_specs=[hbm_spec, full(xs), full(sa), full(sb), full(h0)] + weight_specs,
        out_specs=[hbm_spec, full(hist_shape), full(hist_shape), full(h_shape),
                   full(xs), full(sa), full(sb), full(h0), full(q_shape)],
        out_shape=[like(x), hist_shape, hist_shape, h_shape, like(xs), like(sa), like(sb), like(h0), q_shape],
        scratch_shapes=[pltpu.VMEM(hist_shape.shape, F32), pltpu.VMEM(hist_shape.shape, F32),
                        pltpu.VMEM(h_shape.shape, F32),
                        pltpu.VMEM((2, n_sub, N_MIX_CHUNKS, MIX_STEPS * batch, N_MIX_PARTS * MIX_CHUNK), F32),
                        pltpu.VMEM((3, MIX_BLOCK_STEPS, batch, D_MODEL), F32),
                        pltpu.VMEM((2, MIX_BLOCK_STEPS, batch, D_MODEL), F32),
                        pltpu.SemaphoreType.DMA((3, batch)), pltpu.SemaphoreType.DMA((2, batch)),
                        pltpu.VMEM((D_MODEL, N_MIX_PARTS * D_CONV), BF16),
                        pltpu.VMEM((N_MIX_CHUNKS, MIX_CHUNK, 2 * MIX_CHUNK), BF16),
                        pltpu.VMEM((D_MODEL, D_MODEL), BF16)],
        compiler_params=_params("arbitrary"),
        name="mix",
    )(x, xs, sa, sb, h0, *weights, g_q, wq)


def _head_row_copies(buf_ref, hbm_ref, sem_ref, which, slot, step):
    copies = []
    for hd in range(N_XHEADS):
        for half in range(XHEAD_DIM // LANES):
            col = (hd * (XHEAD_DIM // LANES) + half) * LANES
            sub = half * N_XHEADS + hd
            src = buf_ref.at[slot, :, pl.ds(col, LANES)]
            dst = hbm_ref.at[pl.ds(step * KV_ROW_TILE, KV_ROW_TILE), sub, :]
            copies.append(pltpu.make_async_copy(src, dst, sem_ref.at[which, slot, sub]))
    return copies


def _kv_kernel(m_ref, g_ref, wk_ref, wv_ref, kb_ref, vb_ref, kt_hbm, vt_hbm, kbuf_ref, vbuf_ref, sem_ref, *, n_steps):
    i = pl.program_id(0)
    slot = i % 2
    m = _rms(m_ref[...], g_ref[...]).astype(BF16)
    k = _dot(m, wk_ref[...].astype(BF16))
    v = _dot(m, wv_ref[...].astype(BF16))
    kb_ref[...] = k.astype(BF16)
    vb_ref[...] = v.astype(BF16)
    kbuf_ref[slot] = k
    vbuf_ref[slot] = v

    def copies(s, step):
        return (_head_row_copies(kbuf_ref, kt_hbm, sem_ref, 0, s, step)
                + _head_row_copies(vbuf_ref, vt_hbm, sem_ref, 1, s, step))

    for cp in copies(slot, i):
        cp.start()

    @pl.when(i >= 1)
    def _():
        for cp in copies(1 - slot, i - 1):
            cp.wait()

    @pl.when(i == n_steps - 1)
    def _():
        for cp in copies(slot, i):
            cp.wait()


def _memory_kv(mem, g_mem, wk, wv):
    rows = mem.shape[0]
    n_steps = rows // KV_ROW_TILE
    row_spec = pl.BlockSpec((KV_ROW_TILE, D_MODEL), lambda i: (i, 0))
    hbm_spec = pl.BlockSpec(memory_space=pl.ANY)
    natural = jax.ShapeDtypeStruct((rows, D_MODEL), BF16)
    tiled = jax.ShapeDtypeStruct((rows, HEAD_ROWS, LANES), F32)
    buf = pltpu.VMEM((2, KV_ROW_TILE, D_MODEL), F32)
    return pl.pallas_call(
        functools.partial(_kv_kernel, n_steps=n_steps),
        grid=(n_steps,),
        in_specs=[row_spec, _const_spec((1, D_MODEL)), _const_spec((D_MODEL, D_MODEL)),
                  _const_spec((D_MODEL, D_MODEL))],
        out_specs=[row_spec, row_spec, hbm_spec, hbm_spec],
        out_shape=[natural, natural, tiled, tiled],
        scratch_shapes=[buf, buf, pltpu.SemaphoreType.DMA((2, 2, HEAD_ROWS))],
        compiler_params=_params("arbitrary"),
        name="memory_kv",
    )(mem, g_mem, wk, wv)


def _merge_heads(a, lead_shape):
    a = a.reshape(lead_shape + (XHEAD_DIM // LANES, N_XHEADS, LANES))
    return jnp.swapaxes(a, -3, -2).reshape(lead_shape + (N_XHEADS, XHEAD_DIM))


def _softmax_rows(s):
    e = jnp.exp(s - jnp.max(s, axis=-1, keepdims=True))
    return e * (1.0 / jnp.sum(e, axis=-1, keepdims=True))


def _split_heads(a):
    n = a.shape[0]
    a = a.reshape(n, -1, N_XHEADS, 2, LANES)
    return jnp.swapaxes(a, 2, 3).reshape(n, -1, LANES)


def _attend_one(qv, k, v, ones_ref):
    rows = N_MEM * HEAD_ROWS
    prod = (k.reshape(N_MEM, HEAD_ROWS, LANES) * (qv * (XHEAD_DIM ** -0.5))[None]).reshape(rows, LANES).astype(BF16)
    pair = jnp.concatenate([prod[:rows // 2], prod[rows // 2:]], axis=-1)
    sums = _dot(pair, ones_ref[...])
    part = jnp.concatenate([sums[:, :LANES], sums[:, LANES:]], axis=0).reshape(N_MEM, HEAD_ROWS, LANES)
    s = part + pltpu.roll(part, N_XHEADS, 1)
    e = jnp.exp(s - jnp.max(s, axis=0, keepdims=True))
    den = jnp.sum(e, axis=0)
    num = jnp.sum(e * v.reshape(N_MEM, HEAD_ROWS, LANES), axis=0)
    return num * (1.0 / den)


def _xattn_kernel(x_ref, k_ref, v_ref, xs_ref, qs_ref, ones_ref, gpre_ref, gpost_ref, wq_ref, wo_ref, ck_ref, cv_ref,
                  o_ref, os_ref, attn_ref, wqb_ref, wob_ref, *, n_prompt):
    i = pl.program_id(0)

    @pl.when(i == 0)
    def _():
        wqb_ref[...] = wq_ref[...].astype(BF16)
        wob_ref[...] = wo_ref[...].astype(BF16)

    @pl.when(i < n_prompt)
    def _():
        def project(h, w_ref, b, as_bf16):
            row = i * XATTN_SAMPLES_PER_STEP + b
            attn_ref[row] = _attend_one(qs_ref[row], ck_ref[b], cv_ref[b], ones_ref)
            out = _dot(h, w_ref[...])
            return out.astype(BF16) if as_bf16 else out

        def scores(q):
            return [lax.dot_general(q[:, hd * XHEAD_DIM:(hd + 1) * XHEAD_DIM], k_ref[0, :, hd * XHEAD_DIM:(hd + 1) * XHEAD_DIM],
                                    (((1,), (1,)), ((), ())), preferred_element_type=F32) for hd in range(N_XHEADS)]

        def weighted(ss):
            ps = [_softmax_rows(s * (XHEAD_DIM ** -0.5)).astype(BF16) for s in ss]
            return jnp.concatenate([_dot(p, v_ref[0, :, hd * XHEAD_DIM:(hd + 1) * XHEAD_DIM]).astype(BF16)
                                    for hd, p in enumerate(ps)], axis=-1)

        ra, rb = (slice(u * XATTN_SUB_ROWS, (u + 1) * XATTN_SUB_ROWS) for u in range(2))
        xa, xb = x_ref[ra, :], x_ref[rb, :]
        ha = _rms(xa, gpre_ref[...]).astype(BF16)
        hb = _rms(xb, gpre_ref[...]).astype(BF16)
        qa = project(ha, wqb_ref, 0, True)
        sa = scores(qa)
        qb = project(hb, wqb_ref, 1, True)
        aa = weighted(sa)
        sb = scores(qb)
        ya = project(aa, wob_ref, 2, False)
        ab = weighted(sb)
        yb = project(ab, wob_ref, 3, False)
        o_ref[ra, :] = xa + _rms(ya, gpost_ref[...])
        o_ref[rb, :] = xb + _rms(yb, gpost_ref[...])

    @pl.when(i == n_prompt)
    def _():
        attn = jnp.concatenate([attn_ref[:, part * N_XHEADS + hd, :] for hd in range(N_XHEADS)
                                for part in range(XHEAD_DIM // LANES)], axis=-1)
        y = _dot(attn.astype(BF16), wob_ref[...])
        os_ref[...] = xs_ref[...] + _rms(y, gpost_ref[...])


def _xattn(x, mem_k, mem_v, xs, q_s, cache_k, cache_v, g_pre, g_post, wq, wo, batch, seq):
    n_t = seq // XATTN_ROW_TILE
    n_prompt = batch * n_t
    n_s = xs.shape[0]
    assert n_prompt * XATTN_SAMPLES_PER_STEP == n_s and XATTN_ROW_TILE == 2 * XATTN_SUB_ROWS
    tile = lambda i: jnp.minimum(i, n_prompt - 1)
    row_spec = pl.BlockSpec((XATTN_ROW_TILE, D_MODEL), lambda i: (tile(i), 0))
    kv_spec = pl.BlockSpec((1, N_MEM, D_MODEL), lambda i: (tile(i) // n_t, 0, 0))
    cache_spec = pl.BlockSpec((XATTN_SAMPLES_PER_STEP, N_MEM * HEAD_ROWS, LANES), lambda i: (tile(i), 0, 0))
    eye2 = jnp.eye(2, dtype=BF16)
    ones = jnp.kron(eye2, jnp.ones((LANES, LANES), BF16))
    return pl.pallas_call(
        functools.partial(_xattn_kernel, n_prompt=n_prompt),
        grid=(n_prompt + 1,),
        in_specs=[row_spec, kv_spec, kv_spec, _const_spec(xs.shape), _const_spec(q_s.shape),
                  _const_spec((2 * LANES, 2 * LANES)), _const_spec((1, D_MODEL)), _const_spec((1, D_MODEL)),
                  _const_spec((D_MODEL, D_MODEL)), _const_spec((D_MODEL, D_MODEL)), cache_spec, cache_spec],
        out_specs=[row_spec, _const_spec(xs.shape)],
        out_shape=[jax.ShapeDtypeStruct(x.shape, F32), jax.ShapeDtypeStruct(xs.shape, F32)],
        scratch_shapes=[pltpu.VMEM((n_s, HEAD_ROWS, LANES), F32),
                        pltpu.VMEM((D_MODEL, D_MODEL), BF16), pltpu.VMEM((D_MODEL, D_MODEL), BF16)],
        compiler_params=_params("arbitrary"),
        name="xattn",
    )(x, mem_k, mem_v, xs, q_s, ones, g_pre, g_post, wq, wo, cache_k, cache_v)


def kernel(x_prompt, x_sample, mem_prompt, cache_mem_k, cache_mem_v, state_conv_a, state_conv_b, state_lru, g_ffn1_pre, g_ffn1_post, ffn1_wg, ffn1_wu, ffn1_wd, g_mix_pre, g_mix_post, w_in, conv_a_w, conv_b_w, conv_b_b, lru_wa, lru_ba, lru_wx, lru_bx, lru_lam, w_out, g_xattn_pre, g_xattn_post, g_mem, xattn_wq, xattn_wk, xattn_wv, xattn_wo, g_ffn2_pre, g_ffn2_post, ffn2_wg, ffn2_wu, ffn2_wd):
    batch, seq, _ = x_prompt.shape
    n_s = x_sample.shape[0]
    depth = g_ffn1_pre.shape[0]
    assert depth == 1 and x_sample.shape[1] == 1
    l = 0
    row = lambda p: p[l].reshape(1, -1)

    yp = x_prompt.reshape(batch * seq, D_MODEL)
    ys = x_sample.reshape(n_s, D_MODEL // LANES, LANES)
    sa = state_conv_a[l].reshape(n_s, CONV_A_WIDTH - 1, D_CONV // LANES, LANES)
    sa = jnp.swapaxes(sa, 1, 2).reshape(n_s, (D_CONV // LANES) * (CONV_A_WIDTH - 1), LANES)
    sb = jnp.swapaxes(state_conv_b[l], 0, 1)

    ffn1 = (row(g_ffn1_pre), row(g_ffn1_post), ffn1_wg[l], ffn1_wu[l], ffn1_wd[l])
    ffn2 = (row(g_ffn2_pre), row(g_ffn2_post), ffn2_wg[l], ffn2_wu[l], ffn2_wd[l])
    mix_w = (row(g_mix_pre), row(g_mix_post), w_in[l], conv_a_w[l], conv_b_w[l], row(conv_b_b), lru_wa[l], lru_wx[l],
             row(lru_ba), row(lru_bx), row(lru_lam), w_out[l])

    mk_b, mv_b, mk, mv = _memory_kv(mem_prompt.reshape(batch * N_MEM, D_MODEL), row(g_mem), xattn_wk[l], xattn_wv[l])

    yp, ys = _ffn(yp, ys, *ffn1, sample_out_tiles=False)

    yp, tail_a, tail_b, tail_h, ys, new_a, new_b, new_h, q_s = _mix(
        yp.reshape(batch, seq, D_MODEL), ys, sa, sb, state_lru[l], mix_w, row(g_xattn_pre), xattn_wq[l], batch, seq)
    yp = yp.reshape(batch * seq, D_MODEL)
    tail_a = tail_a.reshape(CONV_B_WIDTH - 1, batch, D_CONV)[CONV_B_WIDTH - CONV_A_WIDTH:].transpose(1, 0, 2)
    tail_b = tail_b.reshape(CONV_B_WIDTH - 1, batch, D_LRU).transpose(1, 0, 2)
    new_a = jnp.swapaxes(new_a.reshape(n_s, D_CONV // LANES, CONV_A_WIDTH - 1, LANES), 1, 2)
    new_a = new_a.reshape(1, n_s, CONV_A_WIDTH - 1, D_CONV)
    new_b = jnp.swapaxes(new_b, 0, 1)[None]

    yp, ys = _xattn(yp, mk_b.reshape(batch, N_MEM, D_MODEL), mv_b.reshape(batch, N_MEM, D_MODEL), ys, q_s,
                    _split_heads(cache_mem_k[l]), _split_heads(cache_mem_v[l]),
                    row(g_xattn_pre), row(g_xattn_post), xattn_wq[l], xattn_wo[l], batch, seq)

    yp, ys = _ffn(yp, ys, *ffn2, sample_out_tiles=True)

    kv_lead = (1, batch, N_MEM)
    return (yp.reshape(batch, seq, D_MODEL), ys.reshape(n_s, 1, D_MODEL),
            _merge_heads(mk, kv_lead), _merge_heads(mv, kv_lead),
            tail_a[None], tail_b[None], tail_h[None], new_a, new_b, new_h[None])
```

```python
import functools
import math

import jax
import jax.numpy as jnp
from jax import lax
from jax.experimental import pallas as pl
from jax.experimental.pallas import tpu as pltpu

D_MODEL = 1024
D_CONV = 512
D_LRU = 512
N_LRU_HEADS = 8
LRU_HEAD_DIM = D_LRU // N_LRU_HEADS
LRU_C = 8.0
CONV_A_WIDTH = 3
CONV_B_WIDTH = 4
N_MIX_PARTS = 5
FFN_DIM = 2816
N_MEM = 256
N_XHEADS = 4
XHEAD_DIM = D_MODEL // N_XHEADS
RMS_EPS = 1e-6

F32 = jnp.float32
BF16 = jnp.bfloat16

V7X_VMEM_LIMIT_BYTES = 56 * 1024 * 1024
SUBLANES = 8
LANES = 128
HEAD_ROWS = N_XHEADS * XHEAD_DIM // LANES

FFN_ROW_TILE = 1024
FFN_COL_CHUNK = 256
FFN_CHUNKS = FFN_DIM // FFN_COL_CHUNK
MIX_STEPS = 32
MIX_BLOCK_STEPS = 64
MIX_CHUNK = 256
N_MIX_CHUNKS = D_CONV // MIX_CHUNK
XATTN_ROW_TILE = 512
XATTN_SUB_ROWS = 256
KV_ROW_TILE = 512
XATTN_SAMPLES_PER_STEP = 4


def _rms(x, g):
    y = x * lax.rsqrt(jnp.mean(x * x, axis=-1, keepdims=True) + RMS_EPS)
    return y * g


def _dot(a, b):
    return jnp.dot(a, b, preferred_element_type=F32)


def _sigmoid(x):
    return 1.0 / (1.0 + jnp.exp(-x))


def _gelu_tanh(x):
    c = math.sqrt(2.0 / math.pi)
    neg_two_z = x * ((-2.0 * c * 0.044715) * (x * x) - 2.0 * c)
    return x / (1.0 + jnp.exp(neg_two_z))


def _one_minus_exp2x(x):
    t = jnp.tanh(x)
    return (-2.0 * t) / (1.0 - t)


def _log1p(w):
    u = 1.0 + w
    return jnp.where(u == 1.0, w, jnp.log(u) * w / (u - 1.0))


def _softplus(x):
    return jnp.maximum(x, 0.0) + _log1p(jnp.exp(-jnp.abs(x)))


def _const_spec(shape):
    zeros = (0,) * len(shape)
    return pl.BlockSpec(shape, lambda *_: zeros, pipeline_mode=pl.Buffered(1))


def _params(*sem):
    return pltpu.CompilerParams(dimension_semantics=sem, vmem_limit_bytes=V7X_VMEM_LIMIT_BYTES)


def _lane_tiles_to_rows(ref):
    return jnp.concatenate([ref[:, j, :] for j in range(ref.shape[1])], axis=-1)


def _rows_to_lane_tiles(ref, val):
    for j in range(ref.shape[1]):
        ref[:, j, :] = val[:, j * LANES:(j + 1) * LANES]


def _ffn_up(h, wgb_ref, wub_ref, act_ref, c):
    rows = h.shape[0]
    sl = slice(c * FFN_COL_CHUNK, (c + 1) * FFN_COL_CHUNK)
    g = _dot(h, wgb_ref[:, sl])
    u = _dot(h, wub_ref[:, sl])
    act_ref[0:rows, sl] = ((g * _sigmoid(g)) * u).astype(BF16)


def _ffn_rows(x, gpre_ref, gpost_ref, wgb_ref, wub_ref, wdb_ref, act_ref, before_chunk=None, before_down=None):
    rows = x.shape[0]
    h = _rms(x, gpre_ref[...]).astype(BF16)
    for c in range(FFN_CHUNKS):
        if before_chunk is not None:
            before_chunk(c)
        _ffn_up(h, wgb_ref, wub_ref, act_ref, c)
    if before_down is not None:
        before_down()
    y = _dot(act_ref[0:rows, :], wdb_ref[...])
    return x + 0.5 * _rms(y, gpost_ref[...])


def _ffn_weight_copy(hbm_ref, stage_ref, sem_ref, which, c, by_columns):
    if by_columns:
        src = hbm_ref.at[:, pl.ds(c * FFN_COL_CHUNK, FFN_COL_CHUNK)]
    else:
        src = hbm_ref.at[pl.ds(c * FFN_COL_CHUNK, FFN_COL_CHUNK), :]
    return pltpu.make_async_copy(src, stage_ref.at[c % 2], sem_ref.at[which, c % 2])


def _ffn_kernel(xp_ref, xs_ref, gpre_ref, gpost_ref, wg_hbm, wu_hbm, wd_hbm, op_ref, os_ref,
                wgb_ref, wub_ref, wdb_ref, sg_ref, su_ref, sd_ref, sem_ref, act_ref, *, n_prompt):
    i = pl.program_id(0)
    weights = (gpre_ref, gpost_ref, wgb_ref, wub_ref, wdb_ref, act_ref)
    mats = ((wg_hbm, sg_ref, wgb_ref, True), (wu_hbm, su_ref, wub_ref, True), (wd_hbm, sd_ref, wdb_ref, False))

    def copy(m, c):
        hbm, stage, _, by_columns = mats[m]
        return _ffn_weight_copy(hbm, stage, sem_ref, m, c, by_columns)

    def land(m, c):
        _, stage, dst, by_columns = mats[m]
        copy(m, c).wait()
        sl = slice(c * FFN_COL_CHUNK, (c + 1) * FFN_COL_CHUNK)
        if by_columns:
            dst[:, sl] = stage[c % 2].astype(BF16)
        else:
            dst[sl, :] = stage[c % 2].astype(BF16)
        if c + 2 < FFN_CHUNKS:
            copy(m, c + 2).start()

    @pl.when(i == 0)
    def _():
        for c in range(2):
            for m in range(3):
                copy(m, c).start()

        def before_chunk(c):
            for m in range(3):
                land(m, c)

        op_ref[...] = _ffn_rows(xp_ref[...], *weights, before_chunk=before_chunk)

    @pl.when((i > 0) & (i < n_prompt))
    def _():
        op_ref[...] = _ffn_rows(xp_ref[...], *weights)

    @pl.when(i == n_prompt)
    def _():
        xs = _lane_tiles_to_rows(xs_ref) if len(xs_ref.shape) == 3 else xs_ref[...]
        out = _ffn_rows(xs, *weights)
        if len(os_ref.shape) == 3:
            _rows_to_lane_tiles(os_ref, out)
        else:
            os_ref[...] = out


def _ffn(xp, xs, g_pre, g_post, wg, wu, wd, sample_out_tiles):
    n_prompt = xp.shape[0] // FFN_ROW_TILE
    n_s = xs.shape[0]
    out_s = (n_s, D_MODEL // LANES, LANES) if sample_out_tiles else (n_s, D_MODEL)
    prompt_spec = pl.BlockSpec((FFN_ROW_TILE, D_MODEL), lambda i: (jnp.minimum(i, n_prompt - 1), 0))
    hbm_spec = pl.BlockSpec(memory_space=pl.ANY)
    return pl.pallas_call(
        functools.partial(_ffn_kernel, n_prompt=n_prompt),
        grid=(n_prompt + 1,),
        in_specs=[prompt_spec, _const_spec(xs.shape), _const_spec((1, D_MODEL)), _const_spec((1, D_MODEL)),
                  hbm_spec, hbm_spec, hbm_spec],
        out_specs=[prompt_spec, _const_spec(out_s)],
        out_shape=[jax.ShapeDtypeStruct(xp.shape, F32), jax.ShapeDtypeStruct(out_s, F32)],
        scratch_shapes=[pltpu.VMEM((D_MODEL, FFN_DIM), BF16), pltpu.VMEM((D_MODEL, FFN_DIM), BF16),
                        pltpu.VMEM((FFN_DIM, D_MODEL), BF16),
                        pltpu.VMEM((2, D_MODEL, FFN_COL_CHUNK), F32), pltpu.VMEM((2, D_MODEL, FFN_COL_CHUNK), F32),
                        pltpu.VMEM((2, FFN_COL_CHUNK, D_MODEL), F32),
                        pltpu.SemaphoreType.DMA((3, 2)),
                        pltpu.VMEM((FFN_ROW_TILE, FFN_DIM), BF16)],
        compiler_params=_params("arbitrary"),
        name="ffn",
    )(xp, xs, g_pre, g_post, wg, wu, wd)


def _chunk(ref, r, c):
    return ref[r:r + 1, c * MIX_CHUNK:(c + 1) * MIX_CHUNK]


def _mix_project(h, win_ref, c):
    parts = []
    for s in range(N_MIX_PARTS):
        o = s * D_CONV + c * MIX_CHUNK
        parts.append(_dot(h, win_ref[:, o:o + MIX_CHUNK].astype(BF16)))
    return jnp.concatenate(parts, axis=-1)


def _mix_out(ya, yb, wout_ref, c):
    ra = c * MIX_CHUNK
    rb = D_CONV + c * MIX_CHUNK
    return (_dot(ya.astype(BF16), wout_ref[ra:ra + MIX_CHUNK, :].astype(BF16))
            + _dot(yb.astype(BF16), wout_ref[rb:rb + MIX_CHUNK, :].astype(BF16)))


def _mix_front(zc, c, prev_a, prev_b, caw_ref, cbw_ref, cbb_ref, wax_ref):
    gb, gc, xa, xb, gg = (zc[:, s * MIX_CHUNK:(s + 1) * MIX_CHUNK] for s in range(N_MIX_PARTS))
    v = gc * xa
    ca = prev_a(v, 2) * _chunk(caw_ref, 0, c) + prev_a(v, 1) * _chunk(caw_ref, 1, c) + v * _chunk(caw_ref, 2, c)
    ya = gb * ca
    cb = (prev_b(xb, 3) * _chunk(cbw_ref, 0, c) + prev_b(xb, 2) * _chunk(cbw_ref, 1, c)
          + prev_b(xb, 1) * _chunk(cbw_ref, 2, c) + xb * _chunk(cbw_ref, 3, c))
    cb = cb + _chunk(cbb_ref, 0, c)
    gates = _dot(cb.astype(BF16), wax_ref[c].astype(BF16))
    return v, xb, ya, cb, gates, gg


def _lru_coeffs(gates, c, ba_ref, bx_ref, softplus_neg_lam):
    r = _sigmoid(gates[:, :MIX_CHUNK] + _chunk(ba_ref, 0, c))
    i = _sigmoid(gates[:, MIX_CHUNK:] + _chunk(bx_ref, 0, c))
    log_a = (-LRU_C * r) * softplus_neg_lam[:, c * MIX_CHUNK:(c + 1) * MIX_CHUNK]
    a = jnp.exp(log_a)
    mult = jnp.sqrt(_one_minus_exp2x(log_a))
    return a, mult, i


def _seq_copies(hbm_ref, buf_ref, sem_ref, slot, step_block, to_vmem):
    copies = []
    for b in range(SUBLANES):
        hbm = hbm_ref.at[b, pl.ds(step_block * MIX_BLOCK_STEPS, MIX_BLOCK_STEPS), :]
        vmem = buf_ref.at[slot, :, b, :]
        src, dst = (hbm, vmem) if to_vmem else (vmem, hbm)
        copies.append(pltpu.make_async_copy(src, dst, sem_ref.at[slot, b]))
    return copies


def _mix_kernel(x_hbm, xs_ref, sa_ref, sb_ref, h0_ref, gpre_ref, gpost_ref, win_ref, caw_ref, cbw_ref, cbb_ref,
                wa_ref, wx_ref, ba_ref, bx_ref, lam_ref, wout_ref, gq_ref, wq_ref,
                o_hbm, ta_ref, tb_ref, hl_ref, os_ref, na_ref, nb_ref, hs_ref, qs_ref,
                hista_ref, histb_ref, hc_ref, z_ref, xbuf_ref, obuf_ref, xsem_ref, osem_ref, winb_ref, waxb_ref,
                woutb_ref, *, n_tiles):
    i = pl.program_id(0)
    nb = SUBLANES
    rows = MIX_STEPS * nb
    hist_rows = (CONV_B_WIDTH - 1) * nb
    items = [(u, c) for u in range(MIX_BLOCK_STEPS // MIX_STEPS) for c in range(N_MIX_CHUNKS)]

    @pl.when(i == 0)
    def _():
        hista_ref[...] = jnp.zeros_like(hista_ref)
        histb_ref[...] = jnp.zeros_like(histb_ref)
        hc_ref[...] = jnp.zeros_like(hc_ref)
        for cp in _seq_copies(x_hbm, xbuf_ref, xsem_ref, 0, 0, True):
            cp.start()
        winb_ref[...] = win_ref[...].astype(BF16)
        woutb_ref[...] = wout_ref[...].astype(BF16)
        waxb_ref[...] = jnp.zeros_like(waxb_ref)
        heads_per_chunk = MIX_CHUNK // LRU_HEAD_DIM
        for hd in range(N_LRU_HEADS):
            c, j = divmod(hd, heads_per_chunk)
            rsl = slice(j * LRU_HEAD_DIM, (j + 1) * LRU_HEAD_DIM)
            waxb_ref[c, rsl, rsl] = wa_ref[hd].astype(BF16)
            waxb_ref[c, rsl, MIX_CHUNK + j * LRU_HEAD_DIM:MIX_CHUNK + (j + 1) * LRU_HEAD_DIM] = wx_ref[hd].astype(BF16)

    @pl.when(i + 1 < n_tiles)
    def _():
        for cp in _seq_copies(x_hbm, xbuf_ref, xsem_ref, (i + 1) % 3, i + 1, True):
            cp.start()

    @pl.when(i < n_tiles)
    def _():
        for cp in _seq_copies(x_hbm, xbuf_ref, xsem_ref, i % 3, i, True):
            cp.wait()

    @pl.when((i >= 3) & (i <= n_tiles))
    def _():
        for cp in _seq_copies(o_hbm, obuf_ref, osem_ref, (i - 3) % 2, i - 3, False):
            cp.wait()

    def prev(hist):
        return lambda cur, k: jnp.concatenate([hist[hist_rows - k * nb:, :], cur[:rows - k * nb, :]], axis=0)

    def x_rows(tile, u):
        return xbuf_ref[tile % 3, u * MIX_STEPS:(u + 1) * MIX_STEPS].reshape(rows, D_MODEL)

    def run(proj_tile, mix_tile):
        proj_tile, proj_slot = proj_tile if proj_tile is not None else (None, None)
        mix_tile, mix_slot = mix_tile if mix_tile is not None else (None, None)
        normed = {}
        if mix_tile is not None:
            sp = _softplus(-lam_ref[...])
            row = lax.broadcasted_iota(jnp.int32, (rows, MIX_CHUNK), 0)
            first_rows = jnp.where(mix_tile == 0, nb, 0)
            hist_a = hista_ref[...]
            hist_b = histb_ref[...]
            hcur = hc_ref[...]

        def project(u, c):
            if u not in normed:
                normed[u] = _rms(x_rows(proj_tile, u), gpre_ref[...]).astype(BF16)
            z_ref[proj_slot, u, c] = _mix_project(normed[u], winb_ref, c)

        for u, c in items:
            if mix_tile is None:
                project(u, c)
                continue
            if c == 0:
                y = None
                tails_a, tails_b, h_last = [], [], []
            csl = slice(c * MIX_CHUNK, (c + 1) * MIX_CHUNK)
            zc = z_ref[mix_slot, u, c]
            v, xb, ya, cb, gates, gg = _mix_front(zc, c, prev(hist_a[:, csl]), prev(hist_b[:, csl]),
                                                  caw_ref, cbw_ref, cbb_ref, waxb_ref)
            if proj_tile is not None:
                project(u, c)
            a, mult, gate_in = _lru_coeffs(gates, c, ba_ref, bx_ref, sp)
            if u == 0:
                mult = jnp.where(row < first_rows, 1.0, mult)
            b = (mult * gate_in) * cb
            hc = hcur[:, csl]
            hs = []
            for j in range(MIX_STEPS):
                sl = slice(j * nb, (j + 1) * nb)
                hc = a[sl, :] * hc + b[sl, :]
                hs.append(hc)
            yb = _gelu_tanh(gg) * jnp.concatenate(hs, axis=0)
            yc = _mix_out(ya, yb, woutb_ref, c)
            y = yc if y is None else y + yc
            tails_a.append(v[rows - hist_rows:, :])
            tails_b.append(xb[rows - hist_rows:, :])
            h_last.append(hc)
            if c == N_MIX_CHUNKS - 1:
                hist_a = jnp.concatenate(tails_a, axis=-1)
                hist_b = jnp.concatenate(tails_b, axis=-1)
                hcur = jnp.concatenate(h_last, axis=-1)
                out = x_rows(mix_tile, u) + _rms(y, gpost_ref[...])
                obuf_ref[mix_tile % 2, u * MIX_STEPS:(u + 1) * MIX_STEPS] = out.reshape(MIX_STEPS, nb, D_MODEL)
        if mix_tile is not None:
            hista_ref[...] = hist_a
            histb_ref[...] = hist_b
            hc_ref[...] = hcur

    @pl.when(i == 0)
    def _():
        run((i, 0), None)

    for parity in range(2):
        @pl.when((i > 0) & (i < n_tiles) & (i % 2 == parity))
        def _():
            run((i, parity), (i - 1, 1 - parity))

    @pl.when(i == n_tiles)
    def _():
        run(None, (i - 1, (n_tiles - 1) % 2))
        ta_ref[...] = hista_ref[...]
        tb_ref[...] = histb_ref[...]
        hl_ref[...] = hc_ref[...]

    @pl.when(i == n_tiles + 1)
    def _():
        _mix_sample_rows(xs_ref, sa_ref, sb_ref, h0_ref, gpre_ref, gpost_ref, winb_ref, caw_ref, cbw_ref, cbb_ref,
                         waxb_ref, ba_ref, bx_ref, lam_ref, woutb_ref, gq_ref, wq_ref,
                         os_ref, na_ref, nb_ref, hs_ref, qs_ref)

    @pl.when((i >= 1) & (i <= n_tiles))
    def _():
        for cp in _seq_copies(o_hbm, obuf_ref, osem_ref, (i - 1) % 2, i - 1, False):
            cp.start()

    @pl.when(i == n_tiles)
    def _():
        for tile in range(max(n_tiles - 2, 0), n_tiles):
            for cp in _seq_copies(o_hbm, obuf_ref, osem_ref, tile % 2, tile, False):
                cp.wait()


def _mix_sample_rows(x_ref, sa_ref, sb_ref, h0_ref, gpre_ref, gpost_ref, win_ref, caw_ref, cbw_ref, cbb_ref,
                     wax_ref, ba_ref, bx_ref, lam_ref, wout_ref, gq_ref, wq_ref,
                     o_ref, na_ref, nb_ref, hl_ref, q_ref):
    tiles_per_chunk = MIX_CHUNK // LANES
    n_a = CONV_A_WIDTH - 1
    x = x_ref[...]
    h = _rms(x, gpre_ref[...]).astype(BF16)
    sp = _softplus(-lam_ref[...])
    y = None
    for c in range(N_MIX_CHUNKS):
        csl = slice(c * MIX_CHUNK, (c + 1) * MIX_CHUNK)
        lane_tiles = range(c * tiles_per_chunk, (c + 1) * tiles_per_chunk)
        zc = _mix_project(h, win_ref, c)

        def state_a(k):
            return jnp.concatenate([sa_ref[:, j * n_a + (n_a - k), :] for j in lane_tiles], axis=-1)

        def state_b(k):
            return sb_ref[CONV_B_WIDTH - 1 - k, :, csl]

        v, xb, ya, cb, gates, gg = _mix_front(zc, c, lambda cur, k: state_a(k), lambda cur, k: state_b(k),
                                              caw_ref, cbw_ref, cbb_ref, wax_ref)
        a, mult, gate_in = _lru_coeffs(gates, c, ba_ref, bx_ref, sp)
        hnew = a * h0_ref[:, csl] + (mult * gate_in) * cb
        yb = _gelu_tanh(gg) * hnew
        yc = _mix_out(ya, yb, wout_ref, c)
        y = yc if y is None else y + yc

        for t, j in enumerate(lane_tiles):
            for age in range(n_a - 1):
                na_ref[:, j * n_a + age, :] = sa_ref[:, j * n_a + age + 1, :]
            na_ref[:, j * n_a + n_a - 1, :] = v[:, t * LANES:(t + 1) * LANES]
        for age in range(CONV_B_WIDTH - 2):
            nb_ref[age, :, csl] = sb_ref[age + 1, :, csl]
        nb_ref[CONV_B_WIDTH - 2, :, csl] = xb
        hl_ref[:, csl] = hnew
    out = x + _rms(y, gpost_ref[...])
    o_ref[...] = out
    q = _dot(_rms(out, gq_ref[...]).astype(BF16), wq_ref[...].astype(BF16))
    for hd in range(N_XHEADS):
        for half in range(XHEAD_DIM // LANES):
            col = hd * XHEAD_DIM + half * LANES
            q_ref[:, half * N_XHEADS + hd, :] = q[:, col:col + LANES]


def _mix(x, xs, sa, sb, h0, weights, g_q, wq, batch, seq):
    assert batch == SUBLANES and seq % MIX_BLOCK_STEPS == 0
    n_tiles = seq // MIX_BLOCK_STEPS
    n_sub = MIX_BLOCK_STEPS // MIX_STEPS
    n_s = xs.shape[0]
    hist_rows = (CONV_B_WIDTH - 1) * batch
    hbm_spec = pl.BlockSpec(memory_space=pl.ANY)
    hist_shape = jax.ShapeDtypeStruct((hist_rows, D_CONV), F32)
    h_shape = jax.ShapeDtypeStruct((batch, D_LRU), F32)
    q_shape = jax.ShapeDtypeStruct((n_s, HEAD_ROWS, LANES), F32)
    like = lambda a: jax.ShapeDtypeStruct(a.shape, F32)
    full = lambda a: _const_spec(a.shape)
    weight_specs = [_const_spec((1, D_MODEL)), _const_spec((1, D_MODEL)), _const_spec((D_MODEL, N_MIX_PARTS * D_CONV)),
                    _const_spec((CONV_A_WIDTH, D_CONV)), _const_spec((CONV_B_WIDTH, D_LRU)), _const_spec((1, D_LRU)),
                    _const_spec((N_LRU_HEADS, LRU_HEAD_DIM, LRU_HEAD_DIM)),
                    _const_spec((N_LRU_HEADS, LRU_HEAD_DIM, LRU_HEAD_DIM)),
                    _const_spec((1, D_LRU)), _const_spec((1, D_LRU)), _const_spec((1, D_LRU)),
                    _const_spec((D_MODEL, D_MODEL)), _const_spec((1, D_MODEL)), _const_spec((D_MODEL, D_MODEL))]
    return pl.pallas_call(
        functools.partial(_mix_kernel, n_tiles=n_tiles),
        grid=(n_tiles + 2,),
        in_specs=[hbm_spec, full(xs), full(sa), full(sb), full(h0)] + weight_specs,
        out_specs=[hbm_spec, full(hist_shape), full(hist_shape), full(h_shape),
                   full(xs), full(sa), full(sb), full(h0), full(q_shape)],
        out_shape=[like(x), hist_shape, hist_shape, h_shape, like(xs), like(sa), like(sb), like(h0), q_shape],
        scratch_shapes=[pltpu.VMEM(hist_shape.shape, F32), pltpu.VMEM(hist_shape.shape, F32),
                        pltpu.VMEM(h_shape.shape, F32),
                        pltpu.VMEM((2, n_sub, N_MIX_CHUNKS, MIX_STEPS * batch, N_MIX_PARTS * MIX_CHUNK), F32),
                        pltpu.VMEM((3, MIX_BLOCK_STEPS, batch, D_MODEL), F32),
                        pltpu.VMEM((2, MIX_BLOCK_STEPS, batch, D_MODEL), F32),
                        pltpu.SemaphoreType.DMA((3, batch)), pltpu.SemaphoreType.DMA((2, batch)),
                        pltpu.VMEM((D_MODEL, N_MIX_PARTS * D_CONV), BF16),
                        pltpu.VMEM((N_MIX_CHUNKS, MIX_CHUNK, 2 * MIX_CHUNK), BF16),
                        pltpu.VMEM((D_MODEL, D_MODEL), BF16)],
        compiler_params=_params("arbitrary"),
        name="mix",
    )(x, xs, sa, sb, h0, *weights, g_q, wq)


def _head_row_copies(buf_ref, hbm_ref, sem_ref, which, slot, step):
    copies = []
    for hd in range(N_XHEADS):
        for half in range(XHEAD_DIM // LANES):
            col = (hd * (XHEAD_DIM // LANES) + half) * LANES
            sub = half * N_XHEADS + hd
            src = buf_ref.at[slot, :, pl.ds(col, LANES)]
            dst = hbm_ref.at[pl.ds(step * KV_ROW_TILE, KV_ROW_TILE), sub, :]
            copies.append(pltpu.make_async_copy(src, dst, sem_ref.at[which, slot, sub]))
    return copies


def _kv_kernel(m_ref, g_ref, wk_ref, wv_ref, kb_ref, vb_ref, kt_hbm, vt_hbm, kbuf_ref, vbuf_ref, sem_ref, *, n_steps):
    i = pl.program_id(0)
    slot = i % 2
    m = _rms(m_ref[...], g_ref[...]).astype(BF16)
    k = _dot(m, wk_ref[...].astype(BF16))
    v = _dot(m, wv_ref[...].astype(BF16))
    kb_ref[...] = k.astype(BF16)
    vb_ref[...] = v.astype(BF16)
    kbuf_ref[slot] = k
    vbuf_ref[slot] = v

    def copies(s, step):
        return (_head_row_copies(kbuf_ref, kt_hbm, sem_ref, 0, s, step)
                + _head_row_copies(vbuf_ref, vt_hbm, sem_ref, 1, s, step))

    for cp in copies(slot, i):
        cp.start()

    @pl.when(i >= 1)
    def _():
        for cp in copies(1 - slot, i - 1):
            cp.wait()

    @pl.when(i == n_steps - 1)
    def _():
        for cp in copies(slot, i):
            cp.wait()


def _memory_kv(mem, g_mem, wk, wv):
    rows = mem.shape[0]
    n_steps = rows // KV_ROW_TILE
    row_spec = pl.BlockSpec((KV_ROW_TILE, D_MODEL), lambda i: (i, 0))
    hbm_spec = pl.BlockSpec(memory_space=pl.ANY)
    natural = jax.ShapeDtypeStruct((rows, D_MODEL), BF16)
    tiled = jax.ShapeDtypeStruct((rows, HEAD_ROWS, LANES), F32)
    buf = pltpu.VMEM((2, KV_ROW_TILE, D_MODEL), F32)
    return pl.pallas_call(
        functools.partial(_kv_kernel, n_steps=n_steps),
        grid=(n_steps,),
        in_specs=[row_spec, _const_spec((1, D_MODEL)), _const_spec((D_MODEL, D_MODEL)),
                  _const_spec((D_MODEL, D_MODEL))],
        out_specs=[row_spec, row_spec, hbm_spec, hbm_spec],
        out_shape=[natural, natural, tiled, tiled],
        scratch_shapes=[buf, buf, pltpu.SemaphoreType.DMA((2, 2, HEAD_ROWS))],
        compiler_params=_params("arbitrary"),
        name="memory_kv",
    )(mem, g_mem, wk, wv)


def _merge_heads(a, lead_shape):
    a = a.reshape(lead_shape + (XHEAD_DIM // LANES, N_XHEADS, LANES))
    return jnp.swapaxes(a, -3, -2).reshape(lead_shape + (N_XHEADS, XHEAD_DIM))


def _softmax_rows(s):
    e = jnp.exp(s - jnp.max(s, axis=-1, keepdims=True))
    return e * (1.0 / jnp.sum(e, axis=-1, keepdims=True))


def _split_heads(a):
    n = a.shape[0]
    a = a.reshape(n, -1, N_XHEADS, 2, LANES)
    return jnp.swapaxes(a, 2, 3).reshape(n, -1, LANES)


def _attend_one(qv, k, v, ones_ref):
    rows = N_MEM * HEAD_ROWS
    prod = (k.reshape(N_MEM, HEAD_ROWS, LANES) * (qv * (XHEAD_DIM ** -0.5))[None]).reshape(rows, LANES).astype(BF16)
    pair = jnp.concatenate([prod[:rows // 2], prod[rows // 2:]], axis=-1)
    sums = _dot(pair, ones_ref[...])
    part = jnp.concatenate([sums[:, :LANES], sums[:, LANES:]], axis=0).reshape(N_MEM, HEAD_ROWS, LANES)
    s = part + pltpu.roll(part, N_XHEADS, 1)
    e = jnp.exp(s - jnp.max(s, axis=0, keepdims=True))
    den = jnp.sum(e, axis=0)
    num = jnp.sum(e * v.reshape(N_MEM, HEAD_ROWS, LANES), axis=0)
    return num * (1.0 / den)


def _xattn_kernel(x_ref, k_ref, v_ref, xs_ref, qs_ref, ones_ref, gpre_ref, gpost_ref, wq_ref, wo_ref, ck_ref, cv_ref,
                  o_ref, os_ref, attn_ref, wqb_ref, wob_ref, *, n_prompt):
    i = pl.program_id(0)

    @pl.when(i == 0)
    def _():
        wqb_ref[...] = wq_ref[...].astype(BF16)
        wob_ref[...] = wo_ref[...].astype(BF16)

    @pl.when(i < n_prompt)
    def _():
        def project(h, w_ref, b, as_bf16):
            row = i * XATTN_SAMPLES_PER_STEP + b
            attn_ref[row] = _attend_one(qs_ref[row], ck_ref[b], cv_ref[b], ones_ref)
            out = _dot(h, w_ref[...])
            return out.astype(BF16) if as_bf16 else out

        def scores(q):
            return [lax.dot_general(q[:, hd * XHEAD_DIM:(hd + 1) * XHEAD_DIM], k_ref[0, :, hd * XHEAD_DIM:(hd + 1) * XHEAD_DIM],
                                    (((1,), (1,)), ((), ())), preferred_element_type=F32) for hd in range(N_XHEADS)]

        def weighted(ss):
            ps = [_softmax_rows(s * (XHEAD_DIM ** -0.5)).astype(BF16) for s in ss]
            return jnp.concatenate([_dot(p, v_ref[0, :, hd * XHEAD_DIM:(hd + 1) * XHEAD_DIM]).astype(BF16)
                                    for hd, p in enumerate(ps)], axis=-1)

        ra, rb = (slice(u * XATTN_SUB_ROWS, (u + 1) * XATTN_SUB_ROWS) for u in range(2))
        xa, xb = x_ref[ra, :], x_ref[rb, :]
        ha = _rms(xa, gpre_ref[...]).astype(BF16)
        hb = _rms(xb, gpre_ref[...]).astype(BF16)
        qa = project(ha, wqb_ref, 0, True)
        sa = scores(qa)
        qb = project(hb, wqb_ref, 1, True)
        aa = weighted(sa)
        sb = scores(qb)
        ya = project(aa, wob_ref, 2, False)
        ab = weighted(sb)
        yb = project(ab, wob_ref, 3, False)
        o_ref[ra, :] = xa + _rms(ya, gpost_ref[...])
        o_ref[rb, :] = xb + _rms(yb, gpost_ref[...])

    @pl.when(i == n_prompt)
    def _():
        attn = jnp.concatenate([attn_ref[:, part * N_XHEADS + hd, :] for hd in range(N_XHEADS)
                                for part in range(XHEAD_DIM // LANES)], axis=-1)
        y = _dot(attn.astype(BF16), wob_ref[...])
        os_ref[...] = xs_ref[...] + _rms(y, gpost_ref[...])


def _xattn(x, mem_k, mem_v, xs, q_s, cache_k, cache_v, g_pre, g_post, wq, wo, batch, seq):
    n_t = seq // XATTN_ROW_TILE
    n_prompt = batch * n_t
    n_s = xs.shape[0]
    assert n_prompt * XATTN_SAMPLES_PER_STEP == n_s and XATTN_ROW_TILE == 2 * XATTN_SUB_ROWS
    tile = lambda i: jnp.minimum(i, n_prompt - 1)
    row_spec = pl.BlockSpec((XATTN_ROW_TILE, D_MODEL), lambda i: (tile(i), 0))
    kv_spec = pl.BlockSpec((1, N_MEM, D_MODEL), lambda i: (tile(i) // n_t, 0, 0))
    cache_spec = pl.BlockSpec((XATTN_SAMPLES_PER_STEP, N_MEM * HEAD_ROWS, LANES), lambda i: (tile(i), 0, 0))
    eye2 = jnp.eye(2, dtype=BF16)
    ones = jnp.kron(eye2, jnp.ones((LANES, LANES), BF16))
    return pl.pallas_call(
        functools.partial(_xattn_kernel, n_prompt=n_prompt),
        grid=(n_prompt + 1,),
        in_specs=[row_spec, kv_spec, kv_spec, _const_spec(xs.shape), _const_spec(q_s.shape),
                  _const_spec((2 * LANES, 2 * LANES)), _const_spec((1, D_MODEL)), _const_spec((1, D_MODEL)),
                  _const_spec((D_MODEL, D_MODEL)), _const_spec((D_MODEL, D_MODEL)), cache_spec, cache_spec],
        out_specs=[row_spec, _const_spec(xs.shape)],
        out_shape=[jax.ShapeDtypeStruct(x.shape, F32), jax.ShapeDtypeStruct(xs.shape, F32)],
        scratch_shapes=[pltpu.VMEM((n_s, HEAD_ROWS, LANES), F32),
                        pltpu.VMEM((D_MODEL, D_MODEL), BF16), pltpu.VMEM((D_MODEL, D_MODEL), BF16)],
        compiler_params=_params("arbitrary"),
        name="xattn",
    )(x, mem_k, mem_v, xs, q_s, ones, g_pre, g_post, wq, wo, cache_k, cache_v)


def kernel(x_prompt, x_sample, mem_prompt, cache_mem_k, cache_mem_v, state_conv_a, state_conv_b, state_lru, g_ffn1_pre, g_ffn1_post, ffn1_wg, ffn1_wu, ffn1_wd, g_mix_pre, g_mix_post, w_in, conv_a_w, conv_b_w, conv_b_b, lru_wa, lru_ba, lru_wx, lru_bx, lru_lam, w_out, g_xattn_pre, g_xattn_post, g_mem, xattn_wq, xattn_wk, xattn_wv, xattn_wo, g_ffn2_pre, g_ffn2_post, ffn2_wg, ffn2_wu, ffn2_wd):
    batch, seq, _ = x_prompt.shape
    n_s = x_sample.shape[0]
    depth = g_ffn1_pre.shape[0]
    assert depth == 1 and x_sample.shape[1] == 1
    l = 0
    row = lambda p: p[l].reshape(1, -1)

    yp = x_prompt.reshape(batch * seq, D_MODEL)
    ys = x_sample.reshape(n_s, D_MODEL // LANES, LANES)
    sa = state_conv_a[l].reshape(n_s, CONV_A_WIDTH - 1, D_CONV // LANES, LANES)
    sa = jnp.swapaxes(sa, 1, 2).reshape(n_s, (D_CONV // LANES) * (CONV_A_WIDTH - 1), LANES)
    sb = jnp.swapaxes(state_conv_b[l], 0, 1)

    ffn1 = (row(g_ffn1_pre), row(g_ffn1_post), ffn1_wg[l], ffn1_wu[l], ffn1_wd[l])
    ffn2 = (row(g_ffn2_pre), row(g_ffn2_post), ffn2_wg[l], ffn2_wu[l], ffn2_wd[l])
    mix_w = (row(g_mix_pre), row(g_mix_post), w_in[l], conv_a_w[l], conv_b_w[l], row(conv_b_b), lru_wa[l], lru_wx[l],
             row(lru_ba), row(lru_bx), row(lru_lam), w_out[l])

    mk_b, mv_b, mk, mv = _memory_kv(mem_prompt.reshape(batch * N_MEM, D_MODEL), row(g_mem), xattn_wk[l], xattn_wv[l])
    yp, mk_b, mv_b = lax.optimization_barrier((yp, mk_b, mv_b))

    yp, ys = _ffn(yp, ys, *ffn1, sample_out_tiles=False)

    yp, tail_a, tail_b, tail_h, ys, new_a, new_b, new_h, q_s = _mix(
        yp.reshape(batch, seq, D_MODEL), ys, sa, sb, state_lru[l], mix_w, row(g_xattn_pre), xattn_wq[l], batch, seq)
    yp = yp.reshape(batch * seq, D_MODEL)
    tail_a = tail_a.reshape(CONV_B_WIDTH - 1, batch, D_CONV)[CONV_B_WIDTH - CONV_A_WIDTH:].transpose(1, 0, 2)
    tail_b = tail_b.reshape(CONV_B_WIDTH - 1, batch, D_LRU).transpose(1, 0, 2)
    new_a = jnp.swapaxes(new_a.reshape(n_s, D_CONV // LANES, CONV_A_WIDTH - 1, LANES), 1, 2)
    new_a = new_a.reshape(1, n_s, CONV_A_WIDTH - 1, D_CONV)
    new_b = jnp.swapaxes(new_b, 0, 1)[None]

    yp, ys = _xattn(yp, mk_b.reshape(batch, N_MEM, D_MODEL), mv_b.reshape(batch, N_MEM, D_MODEL), ys, q_s,
                    _split_heads(cache_mem_k[l]), _split_heads(cache_mem_v[l]),
                    row(g_xattn_pre), row(g_xattn_post), xattn_wq[l], xattn_wo[l], batch, seq)

    yp, ys = _ffn(yp, ys, *ffn2, sample_out_tiles=True)

    kv_lead = (1, batch, N_MEM)
    return (yp.reshape(batch, seq, D_MODEL), ys.reshape(n_s, 1, D_MODEL),
            _merge_heads(mk, kv_lead), _merge_heads(mv, kv_lead),
            tail_a[None], tail_b[None], tail_h[None], new_a, new_b, new_h[None])
```

```python
import functools
import math

import jax
import jax.numpy as jnp
from jax import lax
from jax.experimental import pallas as pl
from jax.experimental.pallas import tpu as pltpu

D_MODEL = 1024
D_CONV = 512
D_LRU = 512
N_LRU_HEADS = 8
LRU_HEAD_DIM = D_LRU // N_LRU_HEADS
LRU_C = 8.0
CONV_A_WIDTH = 3
CONV_B_WIDTH = 4
N_MIX_PARTS = 5
FFN_DIM = 2816
N_MEM = 256
N_XHEADS = 4
XHEAD_DIM = D_MODEL // N_XHEADS
RMS_EPS = 1e-6

F32 = jnp.float32
BF16 = jnp.bfloat16

V7X_VMEM_LIMIT_BYTES = 56 * 1024 * 1024
SUBLANES = 8
LANES = 128
HEAD_ROWS = N_XHEADS * XHEAD_DIM // LANES

FFN_ROW_TILE = 1024
FFN_COL_CHUNK = 256
FFN_CHUNKS = FFN_DIM // FFN_COL_CHUNK
MIX_STEPS = 32
MIX_BLOCK_STEPS = 64
MIX_CHUNK = 256
N_MIX_CHUNKS = D_CONV // MIX_CHUNK
XATTN_ROW_TILE = 512
XATTN_SUB_ROWS = 256
KV_ROW_TILE = 512
XATTN_SAMPLES_PER_STEP = 4
XATTN_CACHE_SLOTS = 3


def _rms(x, g):
    y = x * lax.rsqrt(jnp.mean(x * x, axis=-1, keepdims=True) + RMS_EPS)
    return y * g


def _dot(a, b):
    return jnp.dot(a, b, preferred_element_type=F32)


def _sigmoid(x):
    return 1.0 / (1.0 + jnp.exp(-x))


def _gelu_tanh(x):
    c = math.sqrt(2.0 / math.pi)
    neg_two_z = x * ((-2.0 * c * 0.044715) * (x * x) - 2.0 * c)
    return x / (1.0 + jnp.exp(neg_two_z))


def _one_minus_exp2x(x):
    t = jnp.tanh(x)
    return (-2.0 * t) / (1.0 - t)


def _log1p(w):
    u = 1.0 + w
    return jnp.where(u == 1.0, w, jnp.log(u) * w / (u - 1.0))


def _softplus(x):
    return jnp.maximum(x, 0.0) + _log1p(jnp.exp(-jnp.abs(x)))


def _const_spec(shape):
    zeros = (0,) * len(shape)
    return pl.BlockSpec(shape, lambda *_: zeros, pipeline_mode=pl.Buffered(1))


def _params(*sem):
    return pltpu.CompilerParams(dimension_semantics=sem, vmem_limit_bytes=V7X_VMEM_LIMIT_BYTES)


def _lane_tiles_to_rows(ref):
    return jnp.concatenate([ref[:, j, :] for j in range(ref.shape[1])], axis=-1)


def _rows_to_lane_tiles(ref, val):
    for j in range(ref.shape[1]):
        ref[:, j, :] = val[:, j * LANES:(j + 1) * LANES]


def _ffn_up(h, wgb_ref, wub_ref, act_ref, c):
    rows = h.shape[0]
    sl = slice(c * FFN_COL_CHUNK, (c + 1) * FFN_COL_CHUNK)
    g = _dot(h, wgb_ref[:, sl])
    u = _dot(h, wub_ref[:, sl])
    act_ref[0:rows, sl] = ((g * _sigmoid(g)) * u).astype(BF16)


def _ffn_rows(x, gpre_ref, gpost_ref, wgb_ref, wub_ref, wdb_ref, act_ref, before_chunk=None, before_down=None):
    rows = x.shape[0]
    h = _rms(x, gpre_ref[...]).astype(BF16)
    for c in range(FFN_CHUNKS):
        if before_chunk is not None:
            before_chunk(c)
        _ffn_up(h, wgb_ref, wub_ref, act_ref, c)
    if before_down is not None:
        before_down()
    y = _dot(act_ref[0:rows, :], wdb_ref[...])
    return x + 0.5 * _rms(y, gpost_ref[...])


def _ffn_weight_copy(hbm_ref, stage_ref, sem_ref, which, c, by_columns):
    if by_columns:
        src = hbm_ref.at[:, pl.ds(c * FFN_COL_CHUNK, FFN_COL_CHUNK)]
    else:
        src = hbm_ref.at[pl.ds(c * FFN_COL_CHUNK, FFN_COL_CHUNK), :]
    return pltpu.make_async_copy(src, stage_ref.at[c % 2], sem_ref.at[which, c % 2])


def _ffn_kernel(xp_ref, xs_ref, gpre_ref, gpost_ref, wg_hbm, wu_hbm, wd_hbm, op_ref, os_ref,
                wgb_ref, wub_ref, wdb_ref, sg_ref, su_ref, sd_ref, sem_ref, act_ref, *, n_prompt):
    i = pl.program_id(0)
    weights = (gpre_ref, gpost_ref, wgb_ref, wub_ref, wdb_ref, act_ref)
    mats = ((wg_hbm, sg_ref, wgb_ref, True), (wu_hbm, su_ref, wub_ref, True), (wd_hbm, sd_ref, wdb_ref, False))

    def copy(m, c):
        hbm, stage, _, by_columns = mats[m]
        return _ffn_weight_copy(hbm, stage, sem_ref, m, c, by_columns)

    def land(m, c):
        _, stage, dst, by_columns = mats[m]
        copy(m, c).wait()
        sl = slice(c * FFN_COL_CHUNK, (c + 1) * FFN_COL_CHUNK)
        if by_columns:
            dst[:, sl] = stage[c % 2].astype(BF16)
        else:
            dst[sl, :] = stage[c % 2].astype(BF16)
        if c + 2 < FFN_CHUNKS:
            copy(m, c + 2).start()

    @pl.when(i == 0)
    def _():
        for c in range(2):
            for m in range(3):
                copy(m, c).start()

        def before_chunk(c):
            for m in range(3):
                land(m, c)

        op_ref[...] = _ffn_rows(xp_ref[...], *weights, before_chunk=before_chunk)

    @pl.when((i > 0) & (i < n_prompt))
    def _():
        op_ref[...] = _ffn_rows(xp_ref[...], *weights)

    @pl.when(i == n_prompt)
    def _():
        xs = _lane_tiles_to_rows(xs_ref) if len(xs_ref.shape) == 3 else xs_ref[...]
        out = _ffn_rows(xs, *weights)
        if len(os_ref.shape) == 3:
            _rows_to_lane_tiles(os_ref, out)
        else:
            os_ref[...] = out


def _ffn(xp, xs, g_pre, g_post, wg, wu, wd, sample_out_tiles):
    n_prompt = xp.shape[0] // FFN_ROW_TILE
    n_s = xs.shape[0]
    out_s = (n_s, D_MODEL // LANES, LANES) if sample_out_tiles else (n_s, D_MODEL)
    prompt_spec = pl.BlockSpec((FFN_ROW_TILE, D_MODEL), lambda i: (jnp.minimum(i, n_prompt - 1), 0))
    hbm_spec = pl.BlockSpec(memory_space=pl.ANY)
    return pl.pallas_call(
        functools.partial(_ffn_kernel, n_prompt=n_prompt),
        grid=(n_prompt + 1,),
        in_specs=[prompt_spec, _const_spec(xs.shape), _const_spec((1, D_MODEL)), _const_spec((1, D_MODEL)),
                  hbm_spec, hbm_spec, hbm_spec],
        out_specs=[prompt_spec, _const_spec(out_s)],
        out_shape=[jax.ShapeDtypeStruct(xp.shape, F32), jax.ShapeDtypeStruct(out_s, F32)],
        scratch_shapes=[pltpu.VMEM((D_MODEL, FFN_DIM), BF16), pltpu.VMEM((D_MODEL, FFN_DIM), BF16),
                        pltpu.VMEM((FFN_DIM, D_MODEL), BF16),
                        pltpu.VMEM((2, D_MODEL, FFN_COL_CHUNK), F32), pltpu.VMEM((2, D_MODEL, FFN_COL_CHUNK), F32),
                        pltpu.VMEM((2, FFN_COL_CHUNK, D_MODEL), F32),
                        pltpu.SemaphoreType.DMA((3, 2)),
                        pltpu.VMEM((FFN_ROW_TILE, FFN_DIM), BF16)],
        compiler_params=_params("arbitrary"),
        name="ffn",
    )(xp, xs, g_pre, g_post, wg, wu, wd)


def _chunk(ref, r, c):
    return ref[r:r + 1, c * MIX_CHUNK:(c + 1) * MIX_CHUNK]


def _mix_project(h, win_ref, c):
    parts = []
    for s in range(N_MIX_PARTS):
        o = s * D_CONV + c * MIX_CHUNK
        parts.append(_dot(h, win_ref[:, o:o + MIX_CHUNK].astype(BF16)))
    return jnp.concatenate(parts, axis=-1)


def _mix_out(ya, yb, wout_ref, c):
    ra = c * MIX_CHUNK
    rb = D_CONV + c * MIX_CHUNK
    return (_dot(ya.astype(BF16), wout_ref[ra:ra + MIX_CHUNK, :].astype(BF16))
            + _dot(yb.astype(BF16), wout_ref[rb:rb + MIX_CHUNK, :].astype(BF16)))


def _mix_front(zc, c, prev_a, prev_b, caw_ref, cbw_ref, cbb_ref, wax_ref):
    gb, gc, xa, xb, gg = (zc[:, s * MIX_CHUNK:(s + 1) * MIX_CHUNK] for s in range(N_MIX_PARTS))
    v = gc * xa
    ca = prev_a(v, 2) * _chunk(caw_ref, 0, c) + prev_a(v, 1) * _chunk(caw_ref, 1, c) + v * _chunk(caw_ref, 2, c)
    ya = gb * ca
    cb = (prev_b(xb, 3) * _chunk(cbw_ref, 0, c) + prev_b(xb, 2) * _chunk(cbw_ref, 1, c)
          + prev_b(xb, 1) * _chunk(cbw_ref, 2, c) + xb * _chunk(cbw_ref, 3, c))
    cb = cb + _chunk(cbb_ref, 0, c)
    gates = _dot(cb.astype(BF16), wax_ref[c].astype(BF16))
    return v, xb, ya, cb, gates, gg


def _lru_coeffs(gates, c, ba_ref, bx_ref, softplus_neg_lam):
    r = _sigmoid(gates[:, :MIX_CHUNK] + _chunk(ba_ref, 0, c))
    i = _sigmoid(gates[:, MIX_CHUNK:] + _chunk(bx_ref, 0, c))
    log_a = (-LRU_C * r) * softplus_neg_lam[:, c * MIX_CHUNK:(c + 1) * MIX_CHUNK]
    a = jnp.exp(log_a)
    mult = jnp.sqrt(_one_minus_exp2x(log_a))
    return a, mult, i


def _seq_copies(hbm_ref, buf_ref, sem_ref, slot, step_block, to_vmem):
    copies = []
    for b in range(SUBLANES):
        hbm = hbm_ref.at[b, pl.ds(step_block * MIX_BLOCK_STEPS, MIX_BLOCK_STEPS), :]
        vmem = buf_ref.at[slot, :, b, :]
        src, dst = (hbm, vmem) if to_vmem else (vmem, hbm)
        copies.append(pltpu.make_async_copy(src, dst, sem_ref.at[slot, b]))
    return copies


def _mix_kernel(x_hbm, xs_ref, sa_ref, sb_ref, h0_ref, gpre_ref, gpost_ref, win_ref, caw_ref, cbw_ref, cbb_ref,
                wa_ref, wx_ref, ba_ref, bx_ref, lam_ref, wout_ref, gq_ref, wq_ref,
                o_hbm, ta_ref, tb_ref, hl_ref, os_ref, na_ref, nb_ref, hs_ref, qs_ref,
                hista_ref, histb_ref, hc_ref, z_ref, xbuf_ref, obuf_ref, xsem_ref, osem_ref, winb_ref, waxb_ref,
                woutb_ref, *, n_tiles):
    i = pl.program_id(0)
    nb = SUBLANES
    rows = MIX_STEPS * nb
    hist_rows = (CONV_B_WIDTH - 1) * nb
    items = [(u, c) for u in range(MIX_BLOCK_STEPS // MIX_STEPS) for c in range(N_MIX_CHUNKS)]

    @pl.when(i == 0)
    def _():
        hista_ref[...] = jnp.zeros_like(hista_ref)
        histb_ref[...] = jnp.zeros_like(histb_ref)
        hc_ref[...] = jnp.zeros_like(hc_ref)
        for cp in _seq_copies(x_hbm, xbuf_ref, xsem_ref, 0, 0, True):
            cp.start()
        winb_ref[...] = win_ref[...].astype(BF16)
        woutb_ref[...] = wout_ref[...].astype(BF16)
        waxb_ref[...] = jnp.zeros_like(waxb_ref)
        heads_per_chunk = MIX_CHUNK // LRU_HEAD_DIM
        for hd in range(N_LRU_HEADS):
            c, j = divmod(hd, heads_per_chunk)
            rsl = slice(j * LRU_HEAD_DIM, (j + 1) * LRU_HEAD_DIM)
            waxb_ref[c, rsl, rsl] = wa_ref[hd].astype(BF16)
            waxb_ref[c, rsl, MIX_CHUNK + j * LRU_HEAD_DIM:MIX_CHUNK + (j + 1) * LRU_HEAD_DIM] = wx_ref[hd].astype(BF16)

    @pl.when(i + 1 < n_tiles)
    def _():
        for cp in _seq_copies(x_hbm, xbuf_ref, xsem_ref, (i + 1) % 3, i + 1, True):
            cp.start()

    @pl.when(i < n_tiles)
    def _():
        for cp in _seq_copies(x_hbm, xbuf_ref, xsem_ref, i % 3, i, True):
            cp.wait()

    @pl.when((i >= 3) & (i <= n_tiles))
    def _():
        for cp in _seq_copies(o_hbm, obuf_ref, osem_ref, (i - 3) % 2, i - 3, False):
            cp.wait()

    def prev(hist):
        return lambda cur, k: jnp.concatenate([hist[hist_rows - k * nb:, :], cur[:rows - k * nb, :]], axis=0)

    def x_rows(tile, u):
        return xbuf_ref[tile % 3, u * MIX_STEPS:(u + 1) * MIX_STEPS].reshape(rows, D_MODEL)

    def run(proj_tile, mix_tile):
        proj_tile, proj_slot = proj_tile if proj_tile is not None else (None, None)
        mix_tile, mix_slot = mix_tile if mix_tile is not None else (None, None)
        normed = {}
        if mix_tile is not None:
            sp = _softplus(-lam_ref[...])
            row = lax.broadcasted_iota(jnp.int32, (rows, MIX_CHUNK), 0)
            first_rows = jnp.where(mix_tile == 0, nb, 0)
            hist_a = hista_ref[...]
            hist_b = histb_ref[...]
            hcur = hc_ref[...]

        def project(u, c):
            if u not in normed:
                normed[u] = _rms(x_rows(proj_tile, u), gpre_ref[...]).astype(BF16)
            z_ref[proj_slot, u, c] = _mix_project(normed[u], winb_ref, c)

        for u, c in items:
            if mix_tile is None:
                project(u, c)
                continue
            if c == 0:
                y = None
                tails_a, tails_b, h_last = [], [], []
            csl = slice(c * MIX_CHUNK, (c + 1) * MIX_CHUNK)
            zc = z_ref[mix_slot, u, c]
            v, xb, ya, cb, gates, gg = _mix_front(zc, c, prev(hist_a[:, csl]), prev(hist_b[:, csl]),
                                                  caw_ref, cbw_ref, cbb_ref, waxb_ref)
            if proj_tile is not None:
                project(u, c)
            a, mult, gate_in = _lru_coeffs(gates, c, ba_ref, bx_ref, sp)
            if u == 0:
                mult = jnp.where(row < first_rows, 1.0, mult)
            b = (mult * gate_in) * cb
            hc = hcur[:, csl]
            hs = []
            for j in range(MIX_STEPS):
                sl = slice(j * nb, (j + 1) * nb)
                hc = a[sl, :] * hc + b[sl, :]
                hs.append(hc)
            yb = _gelu_tanh(gg) * jnp.concatenate(hs, axis=0)
            yc = _mix_out(ya, yb, woutb_ref, c)
            y = yc if y is None else y + yc
            tails_a.append(v[rows - hist_rows:, :])
            tails_b.append(xb[rows - hist_rows:, :])
            h_last.append(hc)
            if c == N_MIX_CHUNKS - 1:
                hist_a = jnp.concatenate(tails_a, axis=-1)
                hist_b = jnp.concatenate(tails_b, axis=-1)
                hcur = jnp.concatenate(h_last, axis=-1)
                out = x_rows(mix_tile, u) + _rms(y, gpost_ref[...])
                obuf_ref[mix_tile % 2, u * MIX_STEPS:(u + 1) * MIX_STEPS] = out.reshape(MIX_STEPS, nb, D_MODEL)
        if mix_tile is not None:
            hista_ref[...] = hist_a
            histb_ref[...] = hist_b
            hc_ref[...] = hcur

    @pl.when(i == 0)
    def _():
        run((i, 0), None)

    for parity in range(2):
        @pl.when((i > 0) & (i < n_tiles) & (i % 2 == parity))
        def _():
            run((i, parity), (i - 1, 1 - parity))

    @pl.when(i == n_tiles)
    def _():
        run(None, (i - 1, (n_tiles - 1) % 2))
        ta_ref[...] = hista_ref[...]
        tb_ref[...] = histb_ref[...]
        hl_ref[...] = hc_ref[...]

    @pl.when(i == n_tiles + 1)
    def _():
        _mix_sample_rows(xs_ref, sa_ref, sb_ref, h0_ref, gpre_ref, gpost_ref, winb_ref, caw_ref, cbw_ref, cbb_ref,
                         waxb_ref, ba_ref, bx_ref, lam_ref, woutb_ref, gq_ref, wq_ref,
                         os_ref, na_ref, nb_ref, hs_ref, qs_ref)

    @pl.when((i >= 1) & (i <= n_tiles))
    def _():
        for cp in _seq_copies(o_hbm, obuf_ref, osem_ref, (i - 1) % 2, i - 1, False):
            cp.start()

    @pl.when(i == n_tiles)
    def _():
        for tile in range(max(n_tiles - 2, 0), n_tiles):
            for cp in _seq_copies(o_hbm, obuf_ref, osem_ref, tile % 2, tile, False):
                cp.wait()


def _mix_sample_rows(x_ref, sa_ref, sb_ref, h0_ref, gpre_ref, gpost_ref, win_ref, caw_ref, cbw_ref, cbb_ref,
                     wax_ref, ba_ref, bx_ref, lam_ref, wout_ref, gq_ref, wq_ref,
                     o_ref, na_ref, nb_ref, hl_ref, q_ref):
    tiles_per_chunk = MIX_CHUNK // LANES
    n_a = CONV_A_WIDTH - 1
    x = x_ref[...]
    h = _rms(x, gpre_ref[...]).astype(BF16)
    sp = _softplus(-lam_ref[...])
    y = None
    for c in range(N_MIX_CHUNKS):
        csl = slice(c * MIX_CHUNK, (c + 1) * MIX_CHUNK)
        lane_tiles = range(c * tiles_per_chunk, (c + 1) * tiles_per_chunk)
        zc = _mix_project(h, win_ref, c)

        def state_a(k):
            return jnp.concatenate([sa_ref[:, j * n_a + (n_a - k), :] for j in lane_tiles], axis=-1)

        def state_b(k):
            return sb_ref[CONV_B_WIDTH - 1 - k, :, csl]

        v, xb, ya, cb, gates, gg = _mix_front(zc, c, lambda cur, k: state_a(k), lambda cur, k: state_b(k),
                                              caw_ref, cbw_ref, cbb_ref, wax_ref)
        a, mult, gate_in = _lru_coeffs(gates, c, ba_ref, bx_ref, sp)
        hnew = a * h0_ref[:, csl] + (mult * gate_in) * cb
        yb = _gelu_tanh(gg) * hnew
        yc = _mix_out(ya, yb, wout_ref, c)
        y = yc if y is None else y + yc

        for t, j in enumerate(lane_tiles):
            for age in range(n_a - 1):
                na_ref[:, j * n_a + age, :] = sa_ref[:, j * n_a + age + 1, :]
            na_ref[:, j * n_a + n_a - 1, :] = v[:, t * LANES:(t + 1) * LANES]
        for age in range(CONV_B_WIDTH - 2):
            nb_ref[age, :, csl] = sb_ref[age + 1, :, csl]
        nb_ref[CONV_B_WIDTH - 2, :, csl] = xb
        hl_ref[:, csl] = hnew
    out = x + _rms(y, gpost_ref[...])
    o_ref[...] = out
    q = _dot(_rms(out, gq_ref[...]).astype(BF16), wq_ref[...].astype(BF16))
    for hd in range(N_XHEADS):
        for half in range(XHEAD_DIM // LANES):
            col = hd * XHEAD_DIM + half * LANES
            q_ref[:, half * N_XHEADS + hd, :] = q[:, col:col + LANES]


def _mix(x, xs, sa, sb, h0, weights, g_q, wq, batch, seq):
    assert batch == SUBLANES and seq % MIX_BLOCK_STEPS == 0
    n_tiles = seq // MIX_BLOCK_STEPS
    n_sub = MIX_BLOCK_STEPS // MIX_STEPS
    n_s = xs.shape[0]
    hist_rows = (CONV_B_WIDTH - 1) * batch
    hbm_spec = pl.BlockSpec(memory_space=pl.ANY)
    hist_shape = jax.ShapeDtypeStruct((hist_rows, D_CONV), F32)
    h_shape = jax.ShapeDtypeStruct((batch, D_LRU), F32)
    q_shape = jax.ShapeDtypeStruct((n_s, HEAD_ROWS, LANES), F32)
    like = lambda a: jax.ShapeDtypeStruct(a.shape, F32)
    full = lambda a: _const_spec(a.shape)
    weight_specs = [_const_spec((1, D_MODEL)), _const_spec((1, D_MODEL)), _const_spec((D_MODEL, N_MIX_PARTS * D_CONV)),
                    _const_spec((CONV_A_WIDTH, D_CONV)), _const_spec((CONV_B_WIDTH, D_LRU)), _const_spec((1, D_LRU)),
                    _const_spec((N_LRU_HEADS, LRU_HEAD_DIM, LRU_HEAD_DIM)),
                    _const_spec((N_LRU_HEADS, LRU_HEAD_DIM, LRU_HEAD_DIM)),
                    _const_spec((1, D_LRU)), _const_spec((1, D_LRU)), _const_spec((1, D_LRU)),
                    _const_spec((D_MODEL, D_MODEL)), _const_spec((1, D_MODEL)), _const_spec((D_MODEL, D_MODEL))]
    return pl.pallas_call(
        functools.partial(_mix_kernel, n_tiles=n_tiles),
        grid=(n_tiles + 2,),
        in_specs=[hbm_spec, full(xs), full(sa), full(sb), full(h0)] + weight_specs,
        out_specs=[hbm_spec, full(hist_shape), full(hist_shape), full(h_shape),
                   full(xs), full(sa), full(sb), full(h0), full(q_shape)],
        out_shape=[like(x), hist_shape, hist_shape, h_shape, like(xs), like(sa), like(sb), like(h0), q_shape],
        scratch_shapes=[pltpu.VMEM(hist_shape.shape, F32), pltpu.VMEM(hist_shape.shape, F32),
                        pltpu.VMEM(h_shape.shape, F32),
                        pltpu.VMEM((2, n_sub, N_MIX_CHUNKS, MIX_STEPS * batch, N_MIX_PARTS * MIX_CHUNK), F32),
                        pltpu.VMEM((3, MIX_BLOCK_STEPS, batch, D_MODEL), F32),
                        pltpu.VMEM((2, MIX_BLOCK_STEPS, batch, D_MODEL), F32),
                        pltpu.SemaphoreType.DMA((3, batch)), pltpu.SemaphoreType.DMA((2, batch)),
                        pltpu.VMEM((D_MODEL, N_MIX_PARTS * D_CONV), BF16),
                        pltpu.VMEM((N_MIX_CHUNKS, MIX_CHUNK, 2 * MIX_CHUNK), BF16),
                        pltpu.VMEM((D_MODEL, D_MODEL), BF16)],
        compiler_params=_params("arbitrary"),
        name="mix",
    )(x, xs, sa, sb, h0, *weights, g_q, wq)


def _head_row_copies(buf_ref, hbm_ref, sem_ref, which, slot, step):
    copies = []
    for hd in range(N_XHEADS):
        for half in range(XHEAD_DIM // LANES):
            col = (hd * (XHEAD_DIM // LANES) + half) * LANES
            sub = half * N_XHEADS + hd
            src = buf_ref.at[slot, :, pl.ds(col, LANES)]
            dst = hbm_ref.at[pl.ds(step * KV_ROW_TILE, KV_ROW_TILE), sub, :]
            copies.append(pltpu.make_async_copy(src, dst, sem_ref.at[which, slot, sub]))
    return copies


def _kv_kernel(m_ref, g_ref, wk_ref, wv_ref, kb_ref, vb_ref, kt_hbm, vt_hbm, kbuf_ref, vbuf_ref, sem_ref, *, n_steps):
    i = pl.program_id(0)
    slot = i % 2
    m = _rms(m_ref[...], g_ref[...]).astype(BF16)
    k = _dot(m, wk_ref[...].astype(BF16))
    v = _dot(m, wv_ref[...].astype(BF16))
    kb_ref[...] = k.astype(BF16)
    vb_ref[...] = v.astype(BF16)
    kbuf_ref[slot] = k
    vbuf_ref[slot] = v

    def copies(s, step):
        return (_head_row_copies(kbuf_ref, kt_hbm, sem_ref, 0, s, step)
                + _head_row_copies(vbuf_ref, vt_hbm, sem_ref, 1, s, step))

    for cp in copies(slot, i):
        cp.start()

    @pl.when(i >= 1)
    def _():
        for cp in copies(1 - slot, i - 1):
            cp.wait()

    @pl.when(i == n_steps - 1)
    def _():
        for cp in copies(slot, i):
            cp.wait()


def _memory_kv(mem, g_mem, wk, wv):
    rows = mem.shape[0]
    n_steps = rows // KV_ROW_TILE
    row_spec = pl.BlockSpec((KV_ROW_TILE, D_MODEL), lambda i: (i, 0))
    hbm_spec = pl.BlockSpec(memory_space=pl.ANY)
    natural = jax.ShapeDtypeStruct((rows, D_MODEL), BF16)
    tiled = jax.ShapeDtypeStruct((rows, HEAD_ROWS, LANES), F32)
    buf = pltpu.VMEM((2, KV_ROW_TILE, D_MODEL), F32)
    return pl.pallas_call(
        functools.partial(_kv_kernel, n_steps=n_steps),
        grid=(n_steps,),
        in_specs=[row_spec, _const_spec((1, D_MODEL)), _const_spec((D_MODEL, D_MODEL)),
                  _const_spec((D_MODEL, D_MODEL))],
        out_specs=[row_spec, row_spec, hbm_spec, hbm_spec],
        out_shape=[natural, natural, tiled, tiled],
        scratch_shapes=[buf, buf, pltpu.SemaphoreType.DMA((2, 2, HEAD_ROWS))],
        compiler_params=_params("arbitrary"),
        name="memory_kv",
    )(mem, g_mem, wk, wv)


def _merge_heads(a, lead_shape):
    a = a.reshape(lead_shape + (XHEAD_DIM // LANES, N_XHEADS, LANES))
    return jnp.swapaxes(a, -3, -2).reshape(lead_shape + (N_XHEADS, XHEAD_DIM))


def _softmax_rows(s):
    e = jnp.exp(s - jnp.max(s, axis=-1, keepdims=True))
    return e * (1.0 / jnp.sum(e, axis=-1, keepdims=True))


def _split_heads(a):
    n = a.shape[0]
    a = a.reshape(n, -1, N_XHEADS, 2, LANES)
    return jnp.swapaxes(a, 2, 3).reshape(n, -1, LANES)


def _attend_one(qv, k, v, ones_ref):
    rows = N_MEM * HEAD_ROWS
    prod = (k.reshape(N_MEM, HEAD_ROWS, LANES) * (qv * (XHEAD_DIM ** -0.5))[None]).reshape(rows, LANES).astype(BF16)
    pair = jnp.concatenate([prod[:rows // 2], prod[rows // 2:]], axis=-1)
    sums = _dot(pair, ones_ref[...])
    part = jnp.concatenate([sums[:, :LANES], sums[:, LANES:]], axis=0).reshape(N_MEM, HEAD_ROWS, LANES)
    s = part + pltpu.roll(part, N_XHEADS, 1)
    e = jnp.exp(s - jnp.max(s, axis=0, keepdims=True))
    den = jnp.sum(e, axis=0)
    num = jnp.sum(e * v.reshape(N_MEM, HEAD_ROWS, LANES), axis=0)
    return num * (1.0 / den)


def _cache_copies(ck_hbm, cv_hbm, kbuf_ref, vbuf_ref, sem_ref, step):
    rows = pl.ds(step * XATTN_SAMPLES_PER_STEP, XATTN_SAMPLES_PER_STEP)
    slot = step % XATTN_CACHE_SLOTS
    return [pltpu.make_async_copy(ck_hbm.at[rows], kbuf_ref.at[slot], sem_ref.at[0, slot]),
            pltpu.make_async_copy(cv_hbm.at[rows], vbuf_ref.at[slot], sem_ref.at[1, slot])]


def _xattn_kernel(x_ref, k_ref, v_ref, xs_ref, qs_ref, ones_ref, gpre_ref, gpost_ref, wq_ref, wo_ref, ck_hbm, cv_hbm,
                  o_ref, os_ref, attn_ref, wqb_ref, wob_ref, kbuf_ref, vbuf_ref, sem_ref, *, n_prompt):
    i = pl.program_id(0)

    @pl.when(i == 0)
    def _():
        wqb_ref[...] = wq_ref[...].astype(BF16)
        wob_ref[...] = wo_ref[...].astype(BF16)
        for step in range(XATTN_CACHE_SLOTS - 1):
            for cp in _cache_copies(ck_hbm, cv_hbm, kbuf_ref, vbuf_ref, sem_ref, step):
                cp.start()

    @pl.when(i + XATTN_CACHE_SLOTS - 1 < n_prompt)
    def _():
        for cp in _cache_copies(ck_hbm, cv_hbm, kbuf_ref, vbuf_ref, sem_ref, i + XATTN_CACHE_SLOTS - 1):
            cp.start()

    @pl.when(i < n_prompt)
    def _():
        for cp in _cache_copies(ck_hbm, cv_hbm, kbuf_ref, vbuf_ref, sem_ref, i):
            cp.wait()
        ck_ref = kbuf_ref.at[i % XATTN_CACHE_SLOTS]
        cv_ref = vbuf_ref.at[i % XATTN_CACHE_SLOTS]

        def project(h, w_ref, b, as_bf16):
            row = i * XATTN_SAMPLES_PER_STEP + b
            attn_ref[row] = _attend_one(qs_ref[row], ck_ref[b], cv_ref[b], ones_ref)
            out = _dot(h, w_ref[...])
            return out.astype(BF16) if as_bf16 else out

        def scores(q):
            return [lax.dot_general(q[:, hd * XHEAD_DIM:(hd + 1) * XHEAD_DIM], k_ref[0, :, hd * XHEAD_DIM:(hd + 1) * XHEAD_DIM],
                                    (((1,), (1,)), ((), ())), preferred_element_type=F32) for hd in range(N_XHEADS)]

        def weighted(ss):
            ps = [_softmax_rows(s * (XHEAD_DIM ** -0.5)).astype(BF16) for s in ss]
            return jnp.concatenate([_dot(p, v_ref[0, :, hd * XHEAD_DIM:(hd + 1) * XHEAD_DIM]).astype(BF16)
                                    for hd, p in enumerate(ps)], axis=-1)

        ra, rb = (slice(u * XATTN_SUB_ROWS, (u + 1) * XATTN_SUB_ROWS) for u in range(2))
        xa, xb = x_ref[ra, :], x_ref[rb, :]
        ha = _rms(xa, gpre_ref[...]).astype(BF16)
        hb = _rms(xb, gpre_ref[...]).astype(BF16)
        qa = project(ha, wqb_ref, 0, True)
        sa = scores(qa)
        qb = project(hb, wqb_ref, 1, True)
        aa = weighted(sa)
        sb = scores(qb)
        ya = project(aa, wob_ref, 2, False)
        ab = weighted(sb)
        yb = project(ab, wob_ref, 3, False)
        o_ref[ra, :] = xa + _rms(ya, gpost_ref[...])
        o_ref[rb, :] = xb + _rms(yb, gpost_ref[...])

    @pl.when(i == n_prompt)
    def _():
        attn = jnp.concatenate([attn_ref[:, part * N_XHEADS + hd, :] for hd in range(N_XHEADS)
                                for part in range(XHEAD_DIM // LANES)], axis=-1)
        y = _dot(attn.astype(BF16), wob_ref[...])
        os_ref[...] = xs_ref[...] + _rms(y, gpost_ref[...])


def _xattn(x, mem_k, mem_v, xs, q_s, cache_k, cache_v, g_pre, g_post, wq, wo, batch, seq):
    n_t = seq // XATTN_ROW_TILE
    n_prompt = batch * n_t
    n_s = xs.shape[0]
    assert n_prompt * XATTN_SAMPLES_PER_STEP == n_s and XATTN_ROW_TILE == 2 * XATTN_SUB_ROWS
    tile = lambda i: jnp.minimum(i, n_prompt - 1)
    row_spec = pl.BlockSpec((XATTN_ROW_TILE, D_MODEL), lambda i: (tile(i), 0))
    kv_spec = pl.BlockSpec((1, N_MEM, D_MODEL), lambda i: (tile(i) // n_t, 0, 0))
    hbm_spec = pl.BlockSpec(memory_space=pl.ANY)
    cache_buf = pltpu.VMEM((XATTN_CACHE_SLOTS, XATTN_SAMPLES_PER_STEP, N_MEM * HEAD_ROWS, LANES), F32)
    eye2 = jnp.eye(2, dtype=BF16)
    ones = jnp.kron(eye2, jnp.ones((LANES, LANES), BF16))
    return pl.pallas_call(
        functools.partial(_xattn_kernel, n_prompt=n_prompt),
        grid=(n_prompt + 1,),
        in_specs=[row_spec, kv_spec, kv_spec, _const_spec(xs.shape), _const_spec(q_s.shape),
                  _const_spec((2 * LANES, 2 * LANES)), _const_spec((1, D_MODEL)), _const_spec((1, D_MODEL)),
                  _const_spec((D_MODEL, D_MODEL)), _const_spec((D_MODEL, D_MODEL)), hbm_spec, hbm_spec],
        out_specs=[row_spec, _const_spec(xs.shape)],
        out_shape=[jax.ShapeDtypeStruct(x.shape, F32), jax.ShapeDtypeStruct(xs.shape, F32)],
        scratch_shapes=[pltpu.VMEM((n_s, HEAD_ROWS, LANES), F32),
                        pltpu.VMEM((D_MODEL, D_MODEL), BF16), pltpu.VMEM((D_MODEL, D_MODEL), BF16),
                        cache_buf, cache_buf, pltpu.SemaphoreType.DMA((2, XATTN_CACHE_SLOTS))],
        compiler_params=_params("arbitrary"),
        name="xattn",
    )(x, mem_k, mem_v, xs, q_s, ones, g_pre, g_post, wq, wo, cache_k, cache_v)


def kernel(x_prompt, x_sample, mem_prompt, cache_mem_k, cache_mem_v, state_conv_a, state_conv_b, state_lru, g_ffn1_pre, g_ffn1_post, ffn1_wg, ffn1_wu, ffn1_wd, g_mix_pre, g_mix_post, w_in, conv_a_w, conv_b_w, conv_b_b, lru_wa, lru_ba, lru_wx, lru_bx, lru_lam, w_out, g_xattn_pre, g_xattn_post, g_mem, xattn_wq, xattn_wk, xattn_wv, xattn_wo, g_ffn2_pre, g_ffn2_post, ffn2_wg, ffn2_wu, ffn2_wd):
    batch, seq, _ = x_prompt.shape
    n_s = x_sample.shape[0]
    depth = g_ffn1_pre.shape[0]
    assert depth == 1 and x_sample.shape[1] == 1
    l = 0
    row = lambda p: p[l].reshape(1, -1)

    yp = x_prompt.reshape(batch * seq, D_MODEL)
    ys = x_sample.reshape(n_s, D_MODEL // LANES, LANES)
    sa = state_conv_a[l].reshape(n_s, CONV_A_WIDTH - 1, D_CONV // LANES, LANES)
    sa = jnp.swapaxes(sa, 1, 2).reshape(n_s, (D_CONV // LANES) * (CONV_A_WIDTH - 1), LANES)
    sb = jnp.swapaxes(state_conv_b[l], 0, 1)

    ffn1 = (row(g_ffn1_pre), row(g_ffn1_post), ffn1_wg[l], ffn1_wu[l], ffn1_wd[l])
    ffn2 = (row(g_ffn2_pre), row(g_ffn2_post), ffn2_wg[l], ffn2_wu[l], ffn2_wd[l])
    mix_w = (row(g_mix_pre), row(g_mix_post), w_in[l], conv_a_w[l], conv_b_w[l], row(conv_b_b), lru_wa[l], lru_wx[l],
             row(lru_ba), row(lru_bx), row(lru_lam), w_out[l])

    mk_b, mv_b, mk, mv = _memory_kv(mem_prompt.reshape(batch * N_MEM, D_MODEL), row(g_mem), xattn_wk[l], xattn_wv[l])

    yp, ys = _ffn(yp, ys, *ffn1, sample_out_tiles=False)

    yp, tail_a, tail_b, tail_h, ys, new_a, new_b, new_h, q_s = _mix(
        yp.reshape(batch, seq, D_MODEL), ys, sa, sb, state_lru[l], mix_w, row(g_xattn_pre), xattn_wq[l], batch, seq)
    yp = yp.reshape(batch * seq, D_MODEL)
    tail_a = tail_a.reshape(CONV_B_WIDTH - 1, batch, D_CONV)[CONV_B_WIDTH - CONV_A_WIDTH:].transpose(1, 0, 2)
    tail_b = tail_b.reshape(CONV_B_WIDTH - 1, batch, D_LRU).transpose(1, 0, 2)
    new_a = jnp.swapaxes(new_a.reshape(n_s, D_CONV // LANES, CONV_A_WIDTH - 1, LANES), 1, 2)
    new_a = new_a.reshape(1, n_s, CONV_A_WIDTH - 1, D_CONV)
    new_b = jnp.swapaxes(new_b, 0, 1)[None]

    yp, ys = _xattn(yp, mk_b.reshape(batch, N_MEM, D_MODEL), mv_b.reshape(batch, N_MEM, D_MODEL), ys, q_s,
                    _split_heads(cache_mem_k[l]), _split_heads(cache_mem_v[l]),
                    row(g_xattn_pre), row(g_xattn_post), xattn_wq[l], xattn_wo[l], batch, seq)

    yp, ys = _ffn(yp, ys, *ffn2, sample_out_tiles=True)

    kv_lead = (1, batch, N_MEM)
    return (yp.reshape(batch, seq, D_MODEL), ys.reshape(n_s, 1, D_MODEL),
            _merge_heads(mk, kv_lead), _merge_heads(mv, kv_lead),
            tail_a[None], tail_b[None], tail_h[None], new_a, new_b, new_h[None])
```

```python
import functools
import math

import jax
import jax.numpy as jnp
from jax import lax
from jax.experimental import pallas as pl
from jax.experimental.pallas import tpu as pltpu

D_MODEL = 1024
D_CONV = 512
D_LRU = 512
N_LRU_HEADS = 8
LRU_HEAD_DIM = D_LRU // N_LRU_HEADS
LRU_C = 8.0
CONV_A_WIDTH = 3
CONV_B_WIDTH = 4
N_MIX_PARTS = 5
FFN_DIM = 2816
N_MEM = 256
N_XHEADS = 4
XHEAD_DIM = D_MODEL // N_XHEADS
RMS_EPS = 1e-6

F32 = jnp.float32
BF16 = jnp.bfloat16

V7X_VMEM_LIMIT_BYTES = 56 * 1024 * 1024
SUBLANES = 8
LANES = 128
HEAD_ROWS = N_XHEADS * XHEAD_DIM // LANES

FFN_ROW_TILE = 1024
FFN_COL_CHUNK = 256
FFN_CHUNKS = FFN_DIM // FFN_COL_CHUNK
FFN_SUB_ROWS = 512
MIX_STEPS = 32
MIX_BLOCK_STEPS = 64
MIX_CHUNK = 256
N_MIX_CHUNKS = D_CONV // MIX_CHUNK
XATTN_ROW_TILE = 512
XATTN_SUB_ROWS = 256
KV_ROW_TILE = 512
XATTN_SAMPLES_PER_STEP = 4


def _rms(x, g):
    y = x * lax.rsqrt(jnp.mean(x * x, axis=-1, keepdims=True) + RMS_EPS)
    return y * g


def _dot(a, b):
    return jnp.dot(a, b, preferred_element_type=F32)


def _sigmoid(x):
    return 1.0 / (1.0 + jnp.exp(-x))


def _gelu_tanh(x):
    c = math.sqrt(2.0 / math.pi)
    neg_two_z = x * ((-2.0 * c * 0.044715) * (x * x) - 2.0 * c)
    return x / (1.0 + jnp.exp(neg_two_z))


def _one_minus_exp2x(x):
    t = jnp.tanh(x)
    return (-2.0 * t) / (1.0 - t)


def _log1p(w):
    u = 1.0 + w
    return jnp.where(u == 1.0, w, jnp.log(u) * w / (u - 1.0))


def _softplus(x):
    return jnp.maximum(x, 0.0) + _log1p(jnp.exp(-jnp.abs(x)))


def _const_spec(shape):
    zeros = (0,) * len(shape)
    return pl.BlockSpec(shape, lambda *_: zeros, pipeline_mode=pl.Buffered(1))


def _params(*sem):
    return pltpu.CompilerParams(dimension_semantics=sem, vmem_limit_bytes=V7X_VMEM_LIMIT_BYTES)


def _lane_tiles_to_rows(ref):
    return jnp.concatenate([ref[:, j, :] for j in range(ref.shape[1])], axis=-1)


def _rows_to_lane_tiles(ref, val):
    for j in range(ref.shape[1]):
        ref[:, j, :] = val[:, j * LANES:(j + 1) * LANES]


def _ffn_rows(x, gpre_ref, gpost_ref, wgb_ref, wub_ref, wdb_ref, act_ref, before_chunk=None):
    rows = x.shape[0]
    n_sub = 2 if rows >= 2 * FFN_SUB_ROWS else 1
    sub = rows // n_sub
    xs = [x[u * sub:(u + 1) * sub, :] for u in range(n_sub)]
    hs = [_rms(xu, gpre_ref[...]).astype(BF16) for xu in xs]
    for c in range(FFN_CHUNKS):
        if before_chunk is not None:
            before_chunk(c)
        sl = slice(c * FFN_COL_CHUNK, (c + 1) * FFN_COL_CHUNK)
        for u, h in enumerate(hs):
            g = _dot(h, wgb_ref[:, sl])
            up = _dot(h, wub_ref[:, sl])
            act_ref[u * sub:(u + 1) * sub, sl] = ((g * _sigmoid(g)) * up).astype(BF16)
    ys = [_dot(act_ref[u * sub:(u + 1) * sub, :], wdb_ref[...]) for u in range(n_sub)]
    return jnp.concatenate([xu + 0.5 * _rms(y, gpost_ref[...]) for xu, y in zip(xs, ys)], axis=0)


def _ffn_weight_copy(hbm_ref, stage_ref, sem_ref, which, c, by_columns):
    if by_columns:
        src = hbm_ref.at[:, pl.ds(c * FFN_COL_CHUNK, FFN_COL_CHUNK)]
    else:
        src = hbm_ref.at[pl.ds(c * FFN_COL_CHUNK, FFN_COL_CHUNK), :]
    return pltpu.make_async_copy(src, stage_ref, sem_ref.at[which])


def _ffn_kernel(xp_ref, xs_ref, gpre_ref, gpost_ref, wg_hbm, wu_hbm, wd_hbm, op_ref, os_ref,
                wgb_ref, wub_ref, wdb_ref, sg_ref, su_ref, sd_ref, sem_ref, act_ref, *, n_prompt):
    i = pl.program_id(0)
    weights = (gpre_ref, gpost_ref, wgb_ref, wub_ref, wdb_ref, act_ref)
    mats = ((wg_hbm, sg_ref, wgb_ref, True), (wu_hbm, su_ref, wub_ref, True), (wd_hbm, sd_ref, wdb_ref, False))

    def copy(m, c):
        hbm, stage, _, by_columns = mats[m]
        return _ffn_weight_copy(hbm, stage, sem_ref, m, c, by_columns)

    def land(m, c):
        _, stage, dst, by_columns = mats[m]
        copy(m, c).wait()
        sl = slice(c * FFN_COL_CHUNK, (c + 1) * FFN_COL_CHUNK)
        if by_columns:
            dst[:, sl] = stage[...].astype(BF16)
        else:
            dst[sl, :] = stage[...].astype(BF16)
        if c + 1 < FFN_CHUNKS:
            copy(m, c + 1).start()

    @pl.when(i == 0)
    def _():
        for m in range(3):
            copy(m, 0).start()

        def before_chunk(c):
            for m in range(3):
                land(m, c)

        op_ref[...] = _ffn_rows(xp_ref[...], *weights, before_chunk=before_chunk)

    @pl.when((i > 0) & (i < n_prompt))
    def _():
        op_ref[...] = _ffn_rows(xp_ref[...], *weights)

    @pl.when(i == n_prompt)
    def _():
        xs = _lane_tiles_to_rows(xs_ref) if len(xs_ref.shape) == 3 else xs_ref[...]
        out = _ffn_rows(xs, *weights)
        if len(os_ref.shape) == 3:
            _rows_to_lane_tiles(os_ref, out)
        else:
            os_ref[...] = out


def _ffn(xp, xs, g_pre, g_post, wg, wu, wd, sample_out_tiles):
    n_prompt = xp.shape[0] // FFN_ROW_TILE
    n_s = xs.shape[0]
    out_s = (n_s, D_MODEL // LANES, LANES) if sample_out_tiles else (n_s, D_MODEL)
    prompt_spec = pl.BlockSpec((FFN_ROW_TILE, D_MODEL), lambda i: (jnp.minimum(i, n_prompt - 1), 0))
    hbm_spec = pl.BlockSpec(memory_space=pl.ANY)
    return pl.pallas_call(
        functools.partial(_ffn_kernel, n_prompt=n_prompt),
        grid=(n_prompt + 1,),
        in_specs=[prompt_spec, _const_spec(xs.shape), _const_spec((1, D_MODEL)), _const_spec((1, D_MODEL)),
                  hbm_spec, hbm_spec, hbm_spec],
        out_specs=[prompt_spec, _const_spec(out_s)],
        out_shape=[jax.ShapeDtypeStruct(xp.shape, F32), jax.ShapeDtypeStruct(out_s, F32)],
        scratch_shapes=[pltpu.VMEM((D_MODEL, FFN_DIM), BF16), pltpu.VMEM((D_MODEL, FFN_DIM), BF16),
                        pltpu.VMEM((FFN_DIM, D_MODEL), BF16),
                        pltpu.VMEM((D_MODEL, FFN_COL_CHUNK), F32), pltpu.VMEM((D_MODEL, FFN_COL_CHUNK), F32),
                        pltpu.VMEM((FFN_COL_CHUNK, D_MODEL), F32),
                        pltpu.SemaphoreType.DMA((3,)),
                        pltpu.VMEM((FFN_ROW_TILE, FFN_DIM), BF16)],
        compiler_params=_params("arbitrary"),
        name="ffn",
    )(xp, xs, g_pre, g_post, wg, wu, wd)


def _chunk(ref, r, c):
    return ref[r:r + 1, c * MIX_CHUNK:(c + 1) * MIX_CHUNK]


def _mix_project(h, win_ref, c):
    parts = []
    for s in range(N_MIX_PARTS):
        o = s * D_CONV + c * MIX_CHUNK
        parts.append(_dot(h, win_ref[:, o:o + MIX_CHUNK].astype(BF16)))
    return jnp.concatenate(parts, axis=-1)


def _mix_out(ya, yb, wout_ref, c):
    ra = c * MIX_CHUNK
    rb = D_CONV + c * MIX_CHUNK
    return (_dot(ya.astype(BF16), wout_ref[ra:ra + MIX_CHUNK, :].astype(BF16))
            + _dot(yb.astype(BF16), wout_ref[rb:rb + MIX_CHUNK, :].astype(BF16)))


def _mix_front(zc, c, prev_a, prev_b, caw_ref, cbw_ref, cbb_ref, wax_ref):
    gb, gc, xa, xb, gg = (zc[:, s * MIX_CHUNK:(s + 1) * MIX_CHUNK] for s in range(N_MIX_PARTS))
    v = gc * xa
    ca = prev_a(v, 2) * _chunk(caw_ref, 0, c) + prev_a(v, 1) * _chunk(caw_ref, 1, c) + v * _chunk(caw_ref, 2, c)
    ya = gb * ca
    cb = (prev_b(xb, 3) * _chunk(cbw_ref, 0, c) + prev_b(xb, 2) * _chunk(cbw_ref, 1, c)
          + prev_b(xb, 1) * _chunk(cbw_ref, 2, c) + xb * _chunk(cbw_ref, 3, c))
    cb = cb + _chunk(cbb_ref, 0, c)
    gates = _dot(cb.astype(BF16), wax_ref[c].astype(BF16))
    return v, xb, ya, cb, gates, gg


def _lru_coeffs(gates, c, ba_ref, bx_ref, softplus_neg_lam):
    r = _sigmoid(gates[:, :MIX_CHUNK] + _chunk(ba_ref, 0, c))
    i = _sigmoid(gates[:, MIX_CHUNK:] + _chunk(bx_ref, 0, c))
    log_a = (-LRU_C * r) * softplus_neg_lam[:, c * MIX_CHUNK:(c + 1) * MIX_CHUNK]
    a = jnp.exp(log_a)
    mult = jnp.sqrt(_one_minus_exp2x(log_a))
    return a, mult, i


def _seq_copies(hbm_ref, buf_ref, sem_ref, slot, step_block, to_vmem):
    copies = []
    for b in range(SUBLANES):
        hbm = hbm_ref.at[b, pl.ds(step_block * MIX_BLOCK_STEPS, MIX_BLOCK_STEPS), :]
        vmem = buf_ref.at[slot, :, b, :]
        src, dst = (hbm, vmem) if to_vmem else (vmem, hbm)
        copies.append(pltpu.make_async_copy(src, dst, sem_ref.at[slot, b]))
    return copies


def _mix_kernel(x_hbm, xs_ref, sa_ref, sb_ref, h0_ref, gpre_ref, gpost_ref, win_ref, caw_ref, cbw_ref, cbb_ref,
                wa_ref, wx_ref, ba_ref, bx_ref, lam_ref, wout_ref, gq_ref, wq_ref,
                o_hbm, ta_ref, tb_ref, hl_ref, os_ref, na_ref, nb_ref, hs_ref, qs_ref,
                hista_ref, histb_ref, hc_ref, z_ref, xbuf_ref, obuf_ref, xsem_ref, osem_ref, winb_ref, waxb_ref,
                woutb_ref, *, n_tiles):
    i = pl.program_id(0)
    nb = SUBLANES
    rows = MIX_STEPS * nb
    hist_rows = (CONV_B_WIDTH - 1) * nb
    items = [(u, c) for u in range(MIX_BLOCK_STEPS // MIX_STEPS) for c in range(N_MIX_CHUNKS)]

    @pl.when(i == 0)
    def _():
        hista_ref[...] = jnp.zeros_like(hista_ref)
        histb_ref[...] = jnp.zeros_like(histb_ref)
        hc_ref[...] = jnp.zeros_like(hc_ref)
        for cp in _seq_copies(x_hbm, xbuf_ref, xsem_ref, 0, 0, True):
            cp.start()
        winb_ref[...] = win_ref[...].astype(BF16)
        woutb_ref[...] = wout_ref[...].astype(BF16)
        waxb_ref[...] = jnp.zeros_like(waxb_ref)
        heads_per_chunk = MIX_CHUNK // LRU_HEAD_DIM
        for hd in range(N_LRU_HEADS):
            c, j = divmod(hd, heads_per_chunk)
            rsl = slice(j * LRU_HEAD_DIM, (j + 1) * LRU_HEAD_DIM)
            waxb_ref[c, rsl, rsl] = wa_ref[hd].astype(BF16)
            waxb_ref[c, rsl, MIX_CHUNK + j * LRU_HEAD_DIM:MIX_CHUNK + (j + 1) * LRU_HEAD_DIM] = wx_ref[hd].astype(BF16)

    @pl.when(i + 1 < n_tiles)
    def _():
        for cp in _seq_copies(x_hbm, xbuf_ref, xsem_ref, (i + 1) % 3, i + 1, True):
            cp.start()

    @pl.when(i < n_tiles)
    def _():
        for cp in _seq_copies(x_hbm, xbuf_ref, xsem_ref, i % 3, i, True):
            cp.wait()

    @pl.when((i >= 3) & (i <= n_tiles))
    def _():
        for cp in _seq_copies(o_hbm, obuf_ref, osem_ref, (i - 3) % 2, i - 3, False):
            cp.wait()

    def prev(hist):
        return lambda cur, k: jnp.concatenate([hist[hist_rows - k * nb:, :], cur[:rows - k * nb, :]], axis=0)

    def x_rows(tile, u):
        return xbuf_ref[tile % 3, u * MIX_STEPS:(u + 1) * MIX_STEPS].reshape(rows, D_MODEL)

    def run(proj_tile, mix_tile):
        proj_tile, proj_slot = proj_tile if proj_tile is not None else (None, None)
        mix_tile, mix_slot = mix_tile if mix_tile is not None else (None, None)
        normed = {}
        if mix_tile is not None:
            sp = _softplus(-lam_ref[...])
            row = lax.broadcasted_iota(jnp.int32, (rows, MIX_CHUNK), 0)
            first_rows = jnp.where(mix_tile == 0, nb, 0)
            hist_a = hista_ref[...]
            hist_b = histb_ref[...]
            hcur = hc_ref[...]

        def project(u, c):
            if u not in normed:
                normed[u] = _rms(x_rows(proj_tile, u), gpre_ref[...]).astype(BF16)
            z_ref[proj_slot, u, c] = _mix_project(normed[u], winb_ref, c)

        for u, c in items:
            if mix_tile is None:
                project(u, c)
                continue
            if c == 0:
                y = None
                tails_a, tails_b, h_last = [], [], []
            csl = slice(c * MIX_CHUNK, (c + 1) * MIX_CHUNK)
            zc = z_ref[mix_slot, u, c]
            v, xb, ya, cb, gates, gg = _mix_front(zc, c, prev(hist_a[:, csl]), prev(hist_b[:, csl]),
                                                  caw_ref, cbw_ref, cbb_ref, waxb_ref)
            if proj_tile is not None:
                project(u, c)
            a, mult, gate_in = _lru_coeffs(gates, c, ba_ref, bx_ref, sp)
            if u == 0:
                mult = jnp.where(row < first_rows, 1.0, mult)
            b = (mult * gate_in) * cb
            hc = hcur[:, csl]
            hs = []
            for j in range(MIX_STEPS):
                sl = slice(j * nb, (j + 1) * nb)
                hc = a[sl, :] * hc + b[sl, :]
                hs.append(hc)
            yb = _gelu_tanh(gg) * jnp.concatenate(hs, axis=0)
            yc = _mix_out(ya, yb, woutb_ref, c)
            y = yc if y is None else y + yc
            tails_a.append(v[rows - hist_rows:, :])
            tails_b.append(xb[rows - hist_rows:, :])
            h_last.append(hc)
            if c == N_MIX_CHUNKS - 1:
                hist_a = jnp.concatenate(tails_a, axis=-1)
                hist_b = jnp.concatenate(tails_b, axis=-1)
                hcur = jnp.concatenate(h_last, axis=-1)
                out = x_rows(mix_tile, u) + _rms(y, gpost_ref[...])
                obuf_ref[mix_tile % 2, u * MIX_STEPS:(u + 1) * MIX_STEPS] = out.reshape(MIX_STEPS, nb, D_MODEL)
        if mix_tile is not None:
            hista_ref[...] = hist_a
            histb_ref[...] = hist_b
            hc_ref[...] = hcur

    @pl.when(i == 0)
    def _():
        run((i, 0), None)

    for parity in range(2):
        @pl.when((i > 0) & (i < n_tiles) & (i % 2 == parity))
        def _():
            run((i, parity), (i - 1, 1 - parity))

    @pl.when(i == n_tiles)
    def _():
        run(None, (i - 1, (n_tiles - 1) % 2))
        ta_ref[...] = hista_ref[...]
        tb_ref[...] = histb_ref[...]
        hl_ref[...] = hc_ref[...]

    @pl.when(i == n_tiles + 1)
    def _():
        _mix_sample_rows(xs_ref, sa_ref, sb_ref, h0_ref, gpre_ref, gpost_ref, winb_ref, caw_ref, cbw_ref, cbb_ref,
                         waxb_ref, ba_ref, bx_ref, lam_ref, woutb_ref, gq_ref, wq_ref,
                         os_ref, na_ref, nb_ref, hs_ref, qs_ref)

    @pl.when((i >= 1) & (i <= n_tiles))
    def _():
        for cp in _seq_copies(o_hbm, obuf_ref, osem_ref, (i - 1) % 2, i - 1, False):
            cp.start()

    @pl.when(i == n_tiles)
    def _():
        for tile in range(max(n_tiles - 2, 0), n_tiles):
            for cp in _seq_copies(o_hbm, obuf_ref, osem_ref, tile % 2, tile, False):
                cp.wait()


def _mix_sample_rows(x_ref, sa_ref, sb_ref, h0_ref, gpre_ref, gpost_ref, win_ref, caw_ref, cbw_ref, cbb_ref,
                     wax_ref, ba_ref, bx_ref, lam_ref, wout_ref, gq_ref, wq_ref,
                     o_ref, na_ref, nb_ref, hl_ref, q_ref):
    tiles_per_chunk = MIX_CHUNK // LANES
    n_a = CONV_A_WIDTH - 1
    x = x_ref[...]
    h = _rms(x, gpre_ref[...]).astype(BF16)
    sp = _softplus(-lam_ref[...])
    y = None
    for c in range(N_MIX_CHUNKS):
        csl = slice(c * MIX_CHUNK, (c + 1) * MIX_CHUNK)
        lane_tiles = range(c * tiles_per_chunk, (c + 1) * tiles_per_chunk)
        zc = _mix_project(h, win_ref, c)

        def state_a(k):
            return jnp.concatenate([sa_ref[:, j * n_a + (n_a - k), :] for j in lane_tiles], axis=-1)

        def state_b(k):
            return sb_ref[CONV_B_WIDTH - 1 - k, :, csl]

        v, xb, ya, cb, gates, gg = _mix_front(zc, c, lambda cur, k: state_a(k), lambda cur, k: state_b(k),
                                              caw_ref, cbw_ref, cbb_ref, wax_ref)
        a, mult, gate_in = _lru_coeffs(gates, c, ba_ref, bx_ref, sp)
        hnew = a * h0_ref[:, csl] + (mult * gate_in) * cb
        yb = _gelu_tanh(gg) * hnew
        yc = _mix_out(ya, yb, wout_ref, c)
        y = yc if y is None else y + yc

        for t, j in enumerate(lane_tiles):
            for age in range(n_a - 1):
                na_ref[:, j * n_a + age, :] = sa_ref[:, j * n_a + age + 1, :]
            na_ref[:, j * n_a + n_a - 1, :] = v[:, t * LANES:(t + 1) * LANES]
        for age in range(CONV_B_WIDTH - 2):
            nb_ref[age, :, csl] = sb_ref[age + 1, :, csl]
        nb_ref[CONV_B_WIDTH - 2, :, csl] = xb
        hl_ref[:, csl] = hnew
    out = x + _rms(y, gpost_ref[...])
    o_ref[...] = out
    q = _dot(_rms(out, gq_ref[...]).astype(BF16), wq_ref[...].astype(BF16))
    for hd in range(N_XHEADS):
        for half in range(XHEAD_DIM // LANES):
            col = hd * XHEAD_DIM + half * LANES
            q_ref[:, half * N_XHEADS + hd, :] = q[:, col:col + LANES]


def _mix(x, xs, sa, sb, h0, weights, g_q, wq, batch, seq):
    assert batch == SUBLANES and seq % MIX_BLOCK_STEPS == 0
    n_tiles = seq // MIX_BLOCK_STEPS
    n_sub = MIX_BLOCK_STEPS // MIX_STEPS
    n_s = xs.shape[0]
    hist_rows = (CONV_B_WIDTH - 1) * batch
    hbm_spec = pl.BlockSpec(memory_space=pl.ANY)
    hist_shape = jax.ShapeDtypeStruct((hist_rows, D_CONV), F32)
    h_shape = jax.ShapeDtypeStruct((batch, D_LRU), F32)
    q_shape = jax.ShapeDtypeStruct((n_s, HEAD_ROWS, LANES), F32)
    like = lambda a: jax.ShapeDtypeStruct(a.shape, F32)
    full = lambda a: _const_spec(a.shape)
    weight_specs = [_const_spec((1, D_MODEL)), _const_spec((1, D_MODEL)), _const_spec((D_MODEL, N_MIX_PARTS * D_CONV)),
                    _const_spec((CONV_A_WIDTH, D_CONV)), _const_spec((CONV_B_WIDTH, D_LRU)), _const_spec((1, D_LRU)),
                    _const_spec((N_LRU_HEADS, LRU_HEAD_DIM, LRU_HEAD_DIM)),
                    _const_spec((N_LRU_HEADS, LRU_HEAD_DIM, LRU_HEAD_DIM)),
                    _const_spec((1, D_LRU)), _const_spec((1, D_LRU)), _const_spec((1, D_LRU)),
                    _const_spec((D_MODEL, D_MODEL)), _const_spec((1, D_MODEL)), _const_spec((D_MODEL, D_MODEL))]
    return pl.pallas_call(
        functools.partial(_mix_kernel, n_tiles=n_tiles),
        grid=(n_tiles + 2,),
        in_specs=[hbm_spec, full(xs), full(sa), full(sb), full(h0)] + weight_specs,
        out_specs=[hbm_spec, full(hist_shape), full(hist_shape), full(h_shape),
                   full(xs), full(sa), full(sb), full(h0), full(q_shape)],
        out_shape=[like(x), hist_shape, hist_shape, h_shape, like(xs), like(sa), like(sb), like(h0), q_shape],
        scratch_shapes=[pltpu.VMEM(hist_shape.shape, F32), pltpu.VMEM(hist_shape.shape, F32),
                        pltpu.VMEM(h_shape.shape, F32),
                        pltpu.VMEM((2, n_sub, N_MIX_CHUNKS, MIX_STEPS * batch, N_MIX_PARTS * MIX_CHUNK), F32),
                        pltpu.VMEM((3, MIX_BLOCK_STEPS, batch, D_MODEL), F32),
                        pltpu.VMEM((2, MIX_BLOCK_STEPS, batch, D_MODEL), F32),
                        pltpu.SemaphoreType.DMA((3, batch)), pltpu.SemaphoreType.DMA((2, batch)),
                        pltpu.VMEM((D_MODEL, N_MIX_PARTS * D_CONV), BF16),
                        pltpu.VMEM((N_MIX_CHUNKS, MIX_CHUNK, 2 * MIX_CHUNK), BF16),
                        pltpu.VMEM((D_MODEL, D_MODEL), BF16)],
        compiler_params=_params("arbitrary"),
        name="mix",
    )(x, xs, sa, sb, h0, *weights, g_q, wq)


def _head_row_copies(buf_ref, hbm_ref, sem_ref, which, slot, step):
    copies = []
    for hd in range(N_XHEADS):
        for half in range(XHEAD_DIM // LANES):
            col = (hd * (XHEAD_DIM // LANES) + half) * LANES
            sub = half * N_XHEADS + hd
            src = buf_ref.at[slot, :, pl.ds(col, LANES)]
            dst = hbm_ref.at[pl.ds(step * KV_ROW_TILE, KV_ROW_TILE), sub, :]
            copies.append(pltpu.make_async_copy(src, dst, sem_ref.at[which, slot, sub]))
    return copies


def _kv_kernel(m_ref, g_ref, wk_ref, wv_ref, kb_ref, vb_ref, kt_hbm, vt_hbm, kbuf_ref, vbuf_ref, sem_ref, *, n_steps):
    i = pl.program_id(0)
    slot = i % 2
    m = _rms(m_ref[...], g_ref[...]).astype(BF16)
    k = _dot(m, wk_ref[...].astype(BF16))
    v = _dot(m, wv_ref[...].astype(BF16))
    kb_ref[...] = k.astype(BF16)
    vb_ref[...] = v.astype(BF16)
    kbuf_ref[slot] = k
    vbuf_ref[slot] = v

    def copies(s, step):
        return (_head_row_copies(kbuf_ref, kt_hbm, sem_ref, 0, s, step)
                + _head_row_copies(vbuf_ref, vt_hbm, sem_ref, 1, s, step))

    for cp in copies(slot, i):
        cp.start()

    @pl.when(i >= 1)
    def _():
        for cp in copies(1 - slot, i - 1):
            cp.wait()

    @pl.when(i == n_steps - 1)
    def _():
        for cp in copies(slot, i):
            cp.wait()


def _memory_kv(mem, g_mem, wk, wv):
    rows = mem.shape[0]
    n_steps = rows // KV_ROW_TILE
    row_spec = pl.BlockSpec((KV_ROW_TILE, D_MODEL), lambda i: (i, 0))
    hbm_spec = pl.BlockSpec(memory_space=pl.ANY)
    natural = jax.ShapeDtypeStruct((rows, D_MODEL), BF16)
    tiled = jax.ShapeDtypeStruct((rows, HEAD_ROWS, LANES), F32)
    buf = pltpu.VMEM((2, KV_ROW_TILE, D_MODEL), F32)
    return pl.pallas_call(
        functools.partial(_kv_kernel, n_steps=n_steps),
        grid=(n_steps,),
        in_specs=[row_spec, _const_spec((1, D_MODEL)), _const_spec((D_MODEL, D_MODEL)),
                  _const_spec((D_MODEL, D_MODEL))],
        out_specs=[row_spec, row_spec, hbm_spec, hbm_spec],
        out_shape=[natural, natural, tiled, tiled],
        scratch_shapes=[buf, buf, pltpu.SemaphoreType.DMA((2, 2, HEAD_ROWS))],
        compiler_params=_params("arbitrary"),
        name="memory_kv",
    )(mem, g_mem, wk, wv)


def _merge_heads(a, lead_shape):
    a = a.reshape(lead_shape + (XHEAD_DIM // LANES, N_XHEADS, LANES))
    return jnp.swapaxes(a, -3, -2).reshape(lead_shape + (N_XHEADS, XHEAD_DIM))


def _softmax_rows(s):
    e = jnp.exp(s - jnp.max(s, axis=-1, keepdims=True))
    return e * (1.0 / jnp.sum(e, axis=-1, keepdims=True))


def _split_heads(a):
    n = a.shape[0]
    a = a.reshape(n, -1, N_XHEADS, 2, LANES)
    return jnp.swapaxes(a, 2, 3).reshape(n, -1, LANES)


def _attend_one(qv, k, v, ones_ref):
    rows = N_MEM * HEAD_ROWS
    prod = (k.reshape(N_MEM, HEAD_ROWS, LANES) * (qv * (XHEAD_DIM ** -0.5))[None]).reshape(rows, LANES).astype(BF16)
    pair = jnp.concatenate([prod[:rows // 2], prod[rows // 2:]], axis=-1)
    sums = _dot(pair, ones_ref[...])
    part = jnp.concatenate([sums[:, :LANES], sums[:, LANES:]], axis=0).reshape(N_MEM, HEAD_ROWS, LANES)
    s = part + pltpu.roll(part, N_XHEADS, 1)
    e = jnp.exp(s - jnp.max(s, axis=0, keepdims=True))
    den = jnp.sum(e, axis=0)
    num = jnp.sum(e * v.reshape(N_MEM, HEAD_ROWS, LANES), axis=0)
    return num * (1.0 / den)


def _xattn_kernel(x_ref, k_ref, v_ref, xs_ref, qs_ref, ones_ref, gpre_ref, gpost_ref, wq_ref, wo_ref, ck_ref, cv_ref,
                  o_ref, os_ref, attn_ref, wqb_ref, wob_ref, *, n_prompt):
    i = pl.program_id(0)

    @pl.when(i == 0)
    def _():
        wqb_ref[...] = wq_ref[...].astype(BF16)
        wob_ref[...] = wo_ref[...].astype(BF16)

    @pl.when(i < n_prompt)
    def _():
        def project(h, w_ref, b, as_bf16):
            row = i * XATTN_SAMPLES_PER_STEP + b
            attn_ref[row] = _attend_one(qs_ref[row], ck_ref[b], cv_ref[b], ones_ref)
            out = _dot(h, w_ref[...])
            return out.astype(BF16) if as_bf16 else out

        def scores(q):
            return [lax.dot_general(q[:, hd * XHEAD_DIM:(hd + 1) * XHEAD_DIM], k_ref[0, :, hd * XHEAD_DIM:(hd + 1) * XHEAD_DIM],
                                    (((1,), (1,)), ((), ())), preferred_element_type=F32) for hd in range(N_XHEADS)]

        def weighted(ss):
            ps = [_softmax_rows(s * (XHEAD_DIM ** -0.5)).astype(BF16) for s in ss]
            return jnp.concatenate([_dot(p, v_ref[0, :, hd * XHEAD_DIM:(hd + 1) * XHEAD_DIM]).astype(BF16)
                                    for hd, p in enumerate(ps)], axis=-1)

        ra, rb = (slice(u * XATTN_SUB_ROWS, (u + 1) * XATTN_SUB_ROWS) for u in range(2))
        xa, xb = x_ref[ra, :], x_ref[rb, :]
        ha = _rms(xa, gpre_ref[...]).astype(BF16)
        hb = _rms(xb, gpre_ref[...]).astype(BF16)
        qa = project(ha, wqb_ref, 0, True)
        sa = scores(qa)
        qb = project(hb, wqb_ref, 1, True)
        aa = weighted(sa)
        sb = scores(qb)
        ya = project(aa, wob_ref, 2, False)
        ab = weighted(sb)
        yb = project(ab, wob_ref, 3, False)
        o_ref[ra, :] = xa + _rms(ya, gpost_ref[...])
        o_ref[rb, :] = xb + _rms(yb, gpost_ref[...])

    @pl.when(i == n_prompt)
    def _():
        attn = jnp.concatenate([attn_ref[:, part * N_XHEADS + hd, :] for hd in range(N_XHEADS)
                                for part in range(XHEAD_DIM // LANES)], axis=-1)
        y = _dot(attn.astype(BF16), wob_ref[...])
        os_ref[...] = xs_ref[...] + _rms(y, gpost_ref[...])


def _xattn(x, mem_k, mem_v, xs, q_s, cache_k, cache_v, g_pre, g_post, wq, wo, batch, seq):
    n_t = seq // XATTN_ROW_TILE
    n_prompt = batch * n_t
    n_s = xs.shape[0]
    assert n_prompt * XATTN_SAMPLES_PER_STEP == n_s and XATTN_ROW_TILE == 2 * XATTN_SUB_ROWS
    tile = lambda i: jnp.minimum(i, n_prompt - 1)
    row_spec = pl.BlockSpec((XATTN_ROW_TILE, D_MODEL), lambda i: (tile(i), 0))
    kv_spec = pl.BlockSpec((1, N_MEM, D_MODEL), lambda i: (tile(i) // n_t, 0, 0))
    cache_spec = pl.BlockSpec((XATTN_SAMPLES_PER_STEP, N_MEM * HEAD_ROWS, LANES), lambda i: (tile(i), 0, 0))
    eye2 = jnp.eye(2, dtype=BF16)
    ones = jnp.kron(eye2, jnp.ones((LANES, LANES), BF16))
    return pl.pallas_call(
        functools.partial(_xattn_kernel, n_prompt=n_prompt),
        grid=(n_prompt + 1,),
        in_specs=[row_spec, kv_spec, kv_spec, _const_spec(xs.shape), _const_spec(q_s.shape),
                  _const_spec((2 * LANES, 2 * LANES)), _const_spec((1, D_MODEL)), _const_spec((1, D_MODEL)),
                  _const_spec((D_MODEL, D_MODEL)), _const_spec((D_MODEL, D_MODEL)), cache_spec, cache_spec],
        out_specs=[row_spec, _const_spec(xs.shape)],
        out_shape=[jax.ShapeDtypeStruct(x.shape, F32), jax.ShapeDtypeStruct(xs.shape, F32)],
        scratch_shapes=[pltpu.VMEM((n_s, HEAD_ROWS, LANES), F32),
                        pltpu.VMEM((D_MODEL, D_MODEL), BF16), pltpu.VMEM((D_MODEL, D_MODEL), BF16)],
        compiler_params=_params("arbitrary"),
        name="xattn",
    )(x, mem_k, mem_v, xs, q_s, ones, g_pre, g_post, wq, wo, cache_k, cache_v)


def kernel(x_prompt, x_sample, mem_prompt, cache_mem_k, cache_mem_v, state_conv_a, state_conv_b, state_lru, g_ffn1_pre, g_ffn1_post, ffn1_wg, ffn1_wu, ffn1_wd, g_mix_pre, g_mix_post, w_in, conv_a_w, conv_b_w, conv_b_b, lru_wa, lru_ba, lru_wx, lru_bx, lru_lam, w_out, g_xattn_pre, g_xattn_post, g_mem, xattn_wq, xattn_wk, xattn_wv, xattn_wo, g_ffn2_pre, g_ffn2_post, ffn2_wg, ffn2_wu, ffn2_wd):
    batch, seq, _ = x_prompt.shape
    n_s = x_sample.shape[0]
    depth = g_ffn1_pre.shape[0]
    assert depth == 1 and x_sample.shape[1] == 1
    l = 0
    row = lambda p: p[l].reshape(1, -1)

    yp = x_prompt.reshape(batch * seq, D_MODEL)
    ys = x_sample.reshape(n_s, D_MODEL // LANES, LANES)
    sa = state_conv_a[l].reshape(n_s, CONV_A_WIDTH - 1, D_CONV // LANES, LANES)
    sa = jnp.swapaxes(sa, 1, 2).reshape(n_s, (D_CONV // LANES) * (CONV_A_WIDTH - 1), LANES)
    sb = jnp.swapaxes(state_conv_b[l], 0, 1)

    ffn1 = (row(g_ffn1_pre), row(g_ffn1_post), ffn1_wg[l], ffn1_wu[l], ffn1_wd[l])
    ffn2 = (row(g_ffn2_pre), row(g_ffn2_post), ffn2_wg[l], ffn2_wu[l], ffn2_wd[l])
    mix_w = (row(g_mix_pre), row(g_mix_post), w_in[l], conv_a_w[l], conv_b_w[l], row(conv_b_b), lru_wa[l], lru_wx[l],
             row(lru_ba), row(lru_bx), row(lru_lam), w_out[l])

    mk_b, mv_b, mk, mv = _memory_kv(mem_prompt.reshape(batch * N_MEM, D_MODEL), row(g_mem), xattn_wk[l], xattn_wv[l])

    yp, ys = _ffn(yp, ys, *ffn1, sample_out_tiles=False)

    yp, tail_a, tail_b, tail_h, ys, new_a, new_b, new_h, q_s = _mix(
        yp.reshape(batch, seq, D_MODEL), ys, sa, sb, state_lru[l], mix_w, row(g_xattn_pre), xattn_wq[l], batch, seq)
    yp = yp.reshape(batch * seq, D_MODEL)
    tail_a = tail_a.reshape(CONV_B_WIDTH - 1, batch, D_CONV)[CONV_B_WIDTH - CONV_A_WIDTH:].transpose(1, 0, 2)
    tail_b = tail_b.reshape(CONV_B_WIDTH - 1, batch, D_LRU).transpose(1, 0, 2)
    new_a = jnp.swapaxes(new_a.reshape(n_s, D_CONV // LANES, CONV_A_WIDTH - 1, LANES), 1, 2)
    new_a = new_a.reshape(1, n_s, CONV_A_WIDTH - 1, D_CONV)
    new_b = jnp.swapaxes(new_b, 0, 1)[None]

    yp, ys = _xattn(yp, mk_b.reshape(batch, N_MEM, D_MODEL), mv_b.reshape(batch, N_MEM, D_MODEL), ys, q_s,
                    _split_heads(cache_mem_k[l]), _split_heads(cache_mem_v[l]),
                    row(g_xattn_pre), row(g_xattn_post), xattn_wq[l], xattn_wo[l], batch, seq)

    yp, ys = _ffn(yp, ys, *ffn2, sample_out_tiles=True)

    kv_lead = (1, batch, N_MEM)
    return (yp.reshape(batch, seq, D_MODEL), ys.reshape(n_s, 1, D_MODEL),
            _merge_heads(mk, kv_lead), _merge_heads(mv, kv_lead),
            tail_a[None], tail_b[None], tail_h[None], new_a, new_b, new_h[None])
```
